```python
import jax, jax.numpy as jnp
from jax import lax
import numpy as np

D_MODEL = 1024
BATCH = 2
SEQ = 8192
DEPTH = 2
DEC_BATCH = 32
DEC_SEQ = 4
PAST_LEN = 8192
PAGE_SIZE = 128

HEAD_DIM = 64
CONV_CH = D_MODEL // 4
CONV_WIDTH = 31
RWKV_HEADS = (3 * D_MODEL // 8) // HEAD_DIM
RWKV_DIM = RWKV_HEADS * HEAD_DIM
ATT_HEADS = (D_MODEL - CONV_CH - RWKV_DIM) // HEAD_DIM
ATT_DIM = ATT_HEADS * HEAD_DIM
MIX_DIM = CONV_CH + RWKV_DIM + ATT_DIM
DECAY_LORA = 64
AAA_LORA = 64
GATE_LORA = 128
CONV_COLS = 2 * CONV_CH
RWKV_COLS = 3 * RWKV_DIM + DECAY_LORA + AAA_LORA + GATE_LORA
ATT_COLS = 3 * ATT_DIM
IN_COLS = CONV_COLS + RWKV_COLS + ATT_COLS
ATT_BLOCK = 128
ATT_BIAS_INIT = -8.0
PEER_HEADS = 8
PEER_DK = 256
N_KEYS = 128
N_EXPERTS = N_KEYS * N_KEYS
PEER_TOPK = 16
PEER_BLOCK = 128
RMS_EPS = 1e-5
LN_EPS = 1e-5
GN_EPS = 64e-5

kernel_name = 'hymba_conv_rwkv7_stickbreak_peer_step'


def rmsnorm(x, g):
    xf = x.astype(jnp.float32)
    y = xf * lax.rsqrt(jnp.mean(xf * xf, axis=-1, keepdims=True) + RMS_EPS)
    return (y * g.astype(jnp.float32)).astype(x.dtype)


def layernorm_f32(x, g, b, eps):
    xf = x.astype(jnp.float32)
    mu = jnp.mean(xf, axis=-1, keepdims=True)
    var = jnp.mean(jnp.square(xf - mu), axis=-1, keepdims=True)
    return (xf - mu) * lax.rsqrt(var + eps) * g.astype(jnp.float32) + b.astype(jnp.float32)


def conv_mixer(p, buf, conv_w, conv_b, ln_g, ln_b):
    glu = p[..., :CONV_CH] * jax.nn.sigmoid(p[..., CONV_CH:])
    ext = jnp.concatenate([buf.astype(glu.dtype), glu], axis=1)
    y = lax.conv_general_dilated(ext, conv_w[:, None, :].astype(glu.dtype), window_strides=(1,),
                                 padding='VALID', dimension_numbers=('NWC', 'WIO', 'NWC'),
                                 feature_group_count=CONV_CH)
    y = layernorm_f32(y + conv_b, ln_g, ln_b, LN_EPS)
    y = y * jax.nn.sigmoid(y)
    return y.astype(p.dtype), ext[:, -(CONV_WIDTH - 1):]


def _wkv_step(S, inp):
    r, w, k, v, kk, b = inp
    sa = jnp.einsum('bhvk,bhk->bhv', S, kk)
    S = S * w[:, :, None, :] - sa[..., None] * b[:, :, None, :] + v[..., None] * k[:, :, None, :]
    return S, jnp.einsum('bhvk,bhk->bhv', S, r)


def rwkv_mixer(p, shift_prev, wkv0, mu, w0, w_up, a0, a_up, g_up, k_k, k_a, r_k, ln_g, ln_b):
    B, T, _ = p.shape
    f32 = jnp.float32
    pf = p.astype(f32)
    prev = jnp.concatenate([shift_prev.astype(f32)[:, None, :], pf[:, :-1]], axis=1)
    pm = pf + (prev - pf) * mu.astype(f32)
    o = [0, RWKV_DIM, 2 * RWKV_DIM, 3 * RWKV_DIM, 3 * RWKV_DIM + DECAY_LORA,
         3 * RWKV_DIM + DECAY_LORA + AAA_LORA, RWKV_COLS]
    r, k, v, xw, xa, xg = [pm[..., o[i]:o[i + 1]] for i in range(6)]
    w_log = -jax.nn.softplus(-(w0.astype(f32) + jnp.tanh(xw) @ w_up.astype(f32))) - 0.5
    decay = jnp.exp(-jnp.exp(w_log))
    a = jax.nn.sigmoid(a0.astype(f32) + xa @ a_up.astype(f32))
    g = jax.nn.sigmoid(xg) @ g_up.astype(f32)
    kk = k * k_k.astype(f32)
    k = k * (1.0 + (a - 1.0) * k_a.astype(f32))
    hs = lambda t: t.reshape(B, T, RWKV_HEADS, HEAD_DIM)
    r, k, v, decay, a, kk = hs(r), hs(k), hs(v), hs(decay), hs(a), hs(kk)
    kk = kk / jnp.maximum(jnp.sqrt(jnp.sum(kk * kk, axis=-1, keepdims=True)), 1e-12)
    xs = tuple(jnp.moveaxis(t, 1, 0) for t in (r, decay, k, v, kk, kk * a))
    S, ys = lax.scan(_wkv_step, wkv0.astype(f32), xs)
    y = jnp.moveaxis(ys, 0, 1)
    ym = jnp.mean(y, axis=-1, keepdims=True)
    yv = jnp.mean(jnp.square(y - ym), axis=-1, keepdims=True)
    yn = (y - ym) * lax.rsqrt(yv + GN_EPS)
    yn = yn * ln_g.astype(f32).reshape(RWKV_HEADS, HEAD_DIM) + ln_b.astype(f32).reshape(RWKV_HEADS, HEAD_DIM)
    bonus = jnp.sum(r * k * r_k.astype(f32), axis=-1, keepdims=True) * v
    out = (yn + bonus).reshape(B, T, RWKV_DIM) * g
    return out.astype(p.dtype), p[:, -1], S


def sb_block(q, k, v, q_pos, k_pos, bias):
    z = jnp.einsum('bqhd,bkhd->bhqk', q, k).astype(jnp.float32) * (HEAD_DIM ** -0.5)
    z = z + bias.astype(jnp.float32)[None, :, None, None]
    causal = k_pos[None, :] < q_pos[:, None]
    log_rest = jnp.where(causal, jax.nn.log_sigmoid(-z), 0.0)
    rest_after = lax.cumsum(log_rest, axis=3, reverse=True) - log_rest
    a = jnp.where(causal, jnp.exp(jax.nn.log_sigmoid(z) + rest_after), 0.0)
    return jnp.einsum('bhqk,bkhd->bqhd', a.astype(v.dtype), v)


def sb_attention(q, k, v, past_len, bias):
    B, T, H, Dh = q.shape
    k_pos = jnp.arange(k.shape[1])
    q_pos = past_len + jnp.arange(T)
    if T > ATT_BLOCK and T % ATT_BLOCK == 0:
        nb = T // ATT_BLOCK
        qb = jnp.moveaxis(q.reshape(B, nb, ATT_BLOCK, H, Dh), 1, 0)
        pb = q_pos.reshape(nb, ATT_BLOCK)
        ob = lax.map(lambda qp: sb_block(qp[0], k, v, qp[1], k_pos, bias), (qb, pb))
        return jnp.moveaxis(ob, 0, 1).reshape(B, T, H, Dh)
    return sb_block(q, k, v, q_pos, k_pos, bias)


def peer_block(xb, w_query, sub_keys, u_tab, v_tab):
    T = xb.shape[0]
    q = (xb @ w_query).reshape(T, PEER_HEADS, 2, PEER_DK // 2)
    s = jnp.einsum('thcd,hcnd->thcn', q, sub_keys).astype(jnp.float32)
    s1, i1 = lax.top_k(s[:, :, 0], PEER_TOPK)
    s2, i2 = lax.top_k(s[:, :, 1], PEER_TOPK)
    cand_s = (s1[..., :, None] + s2[..., None, :]).reshape(T, PEER_HEADS, PEER_TOPK * PEER_TOPK)
    cand_i = (i1[..., :, None] * N_KEYS + i2[..., None, :]).reshape(T, PEER_HEADS, PEER_TOPK * PEER_TOPK)
    top_s, pos = lax.top_k(cand_s, PEER_TOPK)
    e = jnp.take_along_axis(cand_i, pos, axis=-1)
    gate = jax.nn.softmax(top_s, axis=-1)
    act = jax.nn.gelu(jnp.einsum('thkd,td->thk', u_tab[e], xb).astype(jnp.float32))
    return jnp.einsum('thk,thkd->td', (gate * act).astype(xb.dtype), v_tab[e])


def peer(x, w_query, sub_keys, u_tab, v_tab):
    n, d = x.shape
    pad = (-n) % PEER_BLOCK
    xp = jnp.pad(x, ((0, pad), (0, 0))).reshape(-1, PEER_BLOCK, d)
    yb = lax.map(lambda xb: peer_block(xb, w_query, sub_keys, u_tab, v_tab), xp)
    return yb.reshape(-1, d)[:n]


def trunk_layer(x, conv_buf, shift_prev, wkv_state, k_past, v_past, past_len,
                norm1_g, w_in, conv_w, conv_b, conv_ln_g, conv_ln_b,
                rwkv_mu, rwkv_w0, rwkv_w_up, rwkv_a0, rwkv_a_up, rwkv_g_up,
                rwkv_k_k, rwkv_k_a, rwkv_r_k, rwkv_ln_g, rwkv_ln_b, att_bias,
                w_out, norm2_g, peer_w_query, peer_sub_keys, peer_u, peer_v):
    B, T, D = x.shape
    h = rmsnorm(x, norm1_g)
    p = h @ w_in
    p_conv = p[..., :CONV_COLS]
    p_rwkv = p[..., CONV_COLS:CONV_COLS + RWKV_COLS]
    p_att = p[..., CONV_COLS + RWKV_COLS:]
    y_conv, conv_new = conv_mixer(p_conv, conv_buf, conv_w, conv_b, conv_ln_g, conv_ln_b)
    y_rwkv, shift_new, wkv_new = rwkv_mixer(p_rwkv, shift_prev, wkv_state, rwkv_mu, rwkv_w0, rwkv_w_up,
                                            rwkv_a0, rwkv_a_up, rwkv_g_up, rwkv_k_k, rwkv_k_a,
                                            rwkv_r_k, rwkv_ln_g, rwkv_ln_b)
    q, k, v = [p_att[..., i * ATT_DIM:(i + 1) * ATT_DIM].reshape(B, T, ATT_HEADS, HEAD_DIM) for i in range(3)]
    if k_past is None:
        k_all, v_all = k, v
    else:
        k_all = jnp.concatenate([k_past.astype(k.dtype), k], axis=1)
        v_all = jnp.concatenate([v_past.astype(v.dtype), v], axis=1)
    y_att = sb_attention(q, k_all, v_all, past_len, att_bias).reshape(B, T, ATT_DIM)
    mix = jnp.concatenate([y_conv, y_rwkv, y_att], axis=-1)
    x = x + mix @ w_out
    h2 = rmsnorm(x, norm2_g)
    x = x + peer(h2.reshape(B * T, D), peer_w_query, peer_sub_keys, peer_u, peer_v).reshape(B, T, D)
    return x, (k, v, conv_new, shift_new, wkv_new.astype(x.dtype))


def setup_inputs(seed: int = 0) -> dict:
    key = jax.random.key(seed)
    ks = jax.random.split(key, 40)
    f32 = jnp.float32
    n_pages = PAST_LEN // PAGE_SIZE
    n_used = DEC_BATCH * n_pages
    n_pool = n_used + max(1, n_used // 4)

    def nrm(k, shape, scale):
        return jax.random.normal(k, shape, f32) * scale

    def gain(k, shape):
        return 1.0 + nrm(k, shape, 0.02)

    page_table = jax.random.permutation(ks[7], n_pool)[:n_used].reshape(DEC_BATCH, n_pages).astype(jnp.int32)
    return {
        'x_prompt': nrm(ks[0], (BATCH, SEQ, D_MODEL), 1.0),
        'x_sample': nrm(ks[1], (DEC_BATCH, DEC_SEQ, D_MODEL), 1.0),
        'cache_k': nrm(ks[2], (DEPTH, n_pool, PAGE_SIZE, ATT_HEADS, HEAD_DIM), 1.0),
        'cache_v': nrm(ks[3], (DEPTH, n_pool, PAGE_SIZE, ATT_HEADS, HEAD_DIM), 1.0),
        'state_conv': nrm(ks[4], (DEPTH, DEC_BATCH, CONV_WIDTH - 1, CONV_CH), 0.5),
        'state_shift': nrm(ks[5], (DEPTH, DEC_BATCH, RWKV_COLS), 1.0),
        'state_wkv': nrm(ks[6], (DEPTH, DEC_BATCH, RWKV_HEADS, HEAD_DIM, HEAD_DIM), 1.0),
        'page_table': page_table,
        'norm1_g': gain(ks[8], (DEPTH, D_MODEL)),
        'w_in': nrm(ks[9], (DEPTH, D_MODEL, IN_COLS), D_MODEL ** -0.5),
        'conv_w': nrm(ks[10], (DEPTH, CONV_WIDTH, CONV_CH), CONV_WIDTH ** -0.5),
        'conv_b': nrm(ks[11], (DEPTH, CONV_CH), 0.02),
        'conv_ln_g': gain(ks[12], (DEPTH, CONV_CH)),
        'conv_ln_b': nrm(ks[13], (DEPTH, CONV_CH), 0.02),
        'rwkv_mu': jax.random.uniform(ks[14], (DEPTH, RWKV_COLS), f32),
        'rwkv_w0': jax.random.uniform(ks[15], (DEPTH, RWKV_DIM), f32, -4.0, -1.0),
        'rwkv_w_up': nrm(ks[16], (DEPTH, DECAY_LORA, RWKV_DIM), 0.5 * DECAY_LORA ** -0.5),
        'rwkv_a0': nrm(ks[17], (DEPTH, RWKV_DIM), 0.1),
        'rwkv_a_up': nrm(ks[18], (DEPTH, AAA_LORA, RWKV_DIM), 0.5 * AAA_LORA ** -0.5),
        'rwkv_g_up': nrm(ks[19], (DEPTH, GATE_LORA, RWKV_DIM), GATE_LORA ** -0.5),
        'rwkv_k_k': 0.85 + nrm(ks[20], (DEPTH, RWKV_DIM), 0.02),
        'rwkv_k_a': gain(ks[21], (DEPTH, RWKV_DIM)),
        'rwkv_r_k': nrm(ks[22], (DEPTH, RWKV_HEADS, HEAD_DIM), 0.1),
        'rwkv_ln_g': gain(ks[23], (DEPTH, RWKV_DIM)),
        'rwkv_ln_b': nrm(ks[24], (DEPTH, RWKV_DIM), 0.02),
        'att_bias': ATT_BIAS_INIT + nrm(ks[32], (DEPTH, ATT_HEADS), 0.5),
        'w_out': nrm(ks[25], (DEPTH, MIX_DIM, D_MODEL), MIX_DIM ** -0.5),
        'norm2_g': gain(ks[26], (DEPTH, D_MODEL)),
        'peer_w_query': nrm(ks[27], (DEPTH, D_MODEL, PEER_HEADS * PEER_DK), D_MODEL ** -0.5),
        'peer_sub_keys': nrm(ks[28], (DEPTH, PEER_HEADS, 2, N_KEYS, PEER_DK // 2), (PEER_DK // 2) ** -0.5),
        'peer_u': nrm(ks[29], (DEPTH, N_EXPERTS, D_MODEL), D_MODEL ** -0.5),
        'peer_v': nrm(ks[30], (DEPTH, N_EXPERTS, D_MODEL), PEER_HEADS ** -0.5),
        'final_g': gain(ks[31], (D_MODEL,)),
    }


def reference(x_prompt, x_sample, cache_k, cache_v, state_conv, state_shift, state_wkv, page_table,
              norm1_g, w_in, conv_w, conv_b, conv_ln_g, conv_ln_b,
              rwkv_mu, rwkv_w0, rwkv_w_up, rwkv_a0, rwkv_a_up, rwkv_g_up,
              rwkv_k_k, rwkv_k_a, rwkv_r_k, rwkv_ln_g, rwkv_ln_b, att_bias,
              w_out, norm2_g, peer_w_query, peer_sub_keys, peer_u, peer_v, final_g):
    n_pages = PAST_LEN // PAGE_SIZE
    past_len = n_pages * PAGE_SIZE
    layer_params = (norm1_g, w_in, conv_w, conv_b, conv_ln_g, conv_ln_b,
                    rwkv_mu, rwkv_w0, rwkv_w_up, rwkv_a0, rwkv_a_up, rwkv_g_up,
                    rwkv_k_k, rwkv_k_a, rwkv_r_k, rwkv_ln_g, rwkv_ln_b, att_bias,
                    w_out, norm2_g, peer_w_query, peer_sub_keys, peer_u, peer_v)
    xp, xs = x_prompt, x_sample
    bp, bs = xp.shape[0], xs.shape[0]
    st_p = [[] for _ in range(5)]
    st_s = [[] for _ in range(5)]
    for li in range(DEPTH):
        lw = [a[li] for a in layer_params]
        xp, new_p = trunk_layer(xp, jnp.zeros((bp, CONV_WIDTH - 1, CONV_CH), xp.dtype),
                                jnp.zeros((bp, RWKV_COLS), xp.dtype),
                                jnp.zeros((bp, RWKV_HEADS, HEAD_DIM, HEAD_DIM), jnp.float32),
                                None, None, 0, *lw)
        k_past = cache_k[li][page_table].reshape(bs, past_len, ATT_HEADS, HEAD_DIM)
        v_past = cache_v[li][page_table].reshape(bs, past_len, ATT_HEADS, HEAD_DIM)
        xs, new_s = trunk_layer(xs, state_conv[li], state_shift[li], state_wkv[li],
                                k_past, v_past, past_len, *lw)
        for j in range(5):
            st_p[j].append(new_p[j])
            st_s[j].append(new_s[j])
    y_prompt = rmsnorm(xp, final_g)
    y_sample = rmsnorm(xs, final_g)
    k_p, v_p, conv_p, shift_p, wkv_p = [jnp.stack(s, axis=0) for s in st_p]
    k_s, v_s, conv_s, shift_s, wkv_s = [jnp.stack(s, axis=0) for s in st_s]
    return (y_prompt, y_sample, k_p, v_p, conv_p, shift_p, wkv_p, k_s, v_s, conv_s, shift_s, wkv_s)
```

```python
import functools

import jax
import jax.numpy as jnp
from jax import lax
from jax.experimental import pallas as pl
from jax.experimental.pallas import tpu as pltpu

F32 = jnp.float32
BF16 = jnp.bfloat16

D_MODEL = 1024
HEAD_DIM = 64
CONV_CH = 256
CONV_WIDTH = 31
CONV_HALO = 32
RWKV_HEADS = 6
RWKV_DIM = 384
ATT_HEADS = 6
ATT_DIM = 384
DECAY_LORA = 64
AAA_LORA = 64
GATE_LORA = 128
CONV_COLS = 512
RWKV_COLS = 1408
IN_COLS = 3072
PAGE_SIZE = 128
PEER_HEADS = 8
PEER_DK = 256
N_KEYS = 128
PEER_TOPK = 16
RMS_EPS = 1e-5
LN_EPS = 1e-5
GN_EPS = 64e-5

ATT_TILE = 256
PAGES_PER_STEP = 8
Q_PAD = 8
PEER_ROWS = 8
VMEM_LIMIT = 56 * 1024 * 1024


def _cparams(sem):
    return pltpu.CompilerParams(dimension_semantics=sem, vmem_limit_bytes=VMEM_LIMIT)


def _sigmoid(x):
    return 1.0 / (1.0 + jnp.exp(-x))


def _softplus(x):
    return jnp.maximum(x, 0.0) + jnp.log(1.0 + jnp.exp(-jnp.abs(x)))


def _dot(a, b):
    return jnp.dot(a, b, preferred_element_type=F32)


def _dot_nt(a, b):
    return lax.dot_general(a, b, (((1,), (1,)), ((), ())), preferred_element_type=F32)


def _split2(x):
    hi = x.astype(BF16)
    lo = (x - hi.astype(F32)).astype(BF16)
    return hi, lo


def _dot3(a, b):
    ah, al = _split2(a)
    bh, bl = _split2(b)
    return _dot(ah, bh) + (_dot(ah, bl) + _dot(al, bh))


def _dot_exact_rhs(x, m):
    hi, lo = _split2(x)
    return _dot(hi, m) + _dot(lo, m)


def _rmsnorm(x, g):
    return x * lax.rsqrt(jnp.mean(x * x, axis=-1, keepdims=True) + RMS_EPS) * g


def _head_blockdiag(n):
    r = lax.broadcasted_iota(jnp.int32, (n, n), 0) // HEAD_DIM
    c = lax.broadcasted_iota(jnp.int32, (n, n), 1) // HEAD_DIM
    return (r == c).astype(BF16)


def _suffix_matrix(n):
    r = lax.broadcasted_iota(jnp.int32, (n, n), 0)
    c = lax.broadcasted_iota(jnp.int32, (n, n), 1)
    return (r > c).astype(BF16)


def _row_spec(tm, cols):
    return pl.BlockSpec((tm, cols), lambda i: (i, 0))


def _const_spec(shape):
    return pl.BlockSpec(shape, lambda *_: (0,) * len(shape))


def _inproj_body(x_ref, g_ref, w_ref, conv_ref, rwkv_ref, k_ref, v_ref, qb_ref, kb_ref, vb_ref):
    h = _rmsnorm(x_ref[...], g_ref[...]).astype(BF16)
    o0, o1, o2, o3 = CONV_COLS, CONV_COLS + RWKV_COLS, CONV_COLS + RWKV_COLS + ATT_DIM, IN_COLS - ATT_DIM
    conv_ref[...] = _dot(h, w_ref[:, 0:o0])
    rwkv_ref[...] = _dot(h, w_ref[:, o0:o1])
    q = _dot(h, w_ref[:, o1:o2])
    k = _dot(h, w_ref[:, o2:o3])
    v = _dot(h, w_ref[:, o3:IN_COLS])
    k_ref[...] = k
    v_ref[...] = v
    qb_ref[...] = (q * (HEAD_DIM ** -0.5)).astype(BF16)
    kb_ref[...] = k.astype(BF16)
    vb_ref[...] = v.astype(BF16)


def _inproj(x, g, w_b, tm):
    n = x.shape[0]
    widths = (CONV_COLS, RWKV_COLS, ATT_DIM, ATT_DIM, ATT_DIM, ATT_DIM, ATT_DIM)
    dtypes = (F32, F32, F32, F32, BF16, BF16, BF16)
    return pl.pallas_call(
        _inproj_body,
        grid=(n // tm,),
        in_specs=[_row_spec(tm, D_MODEL), _const_spec((1, D_MODEL)), _const_spec((D_MODEL, IN_COLS))],
        out_specs=[_row_spec(tm, c) for c in widths],
        out_shape=[jax.ShapeDtypeStruct((n, c), d) for c, d in zip(widths, dtypes)],
        compiler_params=_cparams(("arbitrary",)),
        name="inproj",
    )(x, g.reshape(1, -1), w_b)


def _conv_body(p_ref, buf_ref, w_ref, cb_ref, lg_ref, lb_ref, y_ref, nb_ref, ext_ref, *, tt):
    keep = CONV_WIDTH - 1
    lead = CONV_HALO - keep

    @pl.when(pl.program_id(1) == 0)
    def _():
        ext_ref[0:CONV_HALO, :] = jnp.zeros((CONV_HALO, CONV_CH), F32)
        ext_ref[lead:CONV_HALO, :] = buf_ref[0]

    p = p_ref[0]
    ext_ref[CONV_HALO:CONV_HALO + tt, :] = p[:, :CONV_CH] * _sigmoid(p[:, CONV_CH:])
    acc = jnp.zeros((tt, CONV_CH), F32)
    for j in range(CONV_WIDTH):
        acc = acc + w_ref[j:j + 1, :] * ext_ref[lead + j:lead + j + tt, :]
    y = acc + cb_ref[...]
    mu = jnp.mean(y, axis=-1, keepdims=True)
    d = y - mu
    var = jnp.mean(d * d, axis=-1, keepdims=True)
    y = d * lax.rsqrt(var + LN_EPS) * lg_ref[...] + lb_ref[...]
    y_ref[0] = (y * _sigmoid(y)).astype(y_ref.dtype)
    nb_ref[0] = ext_ref[tt + lead:tt + CONV_HALO, :]
    tail = ext_ref[tt:tt + CONV_HALO, :]
    ext_ref[0:CONV_HALO, :] = tail


def _conv_mixer(p, buf, w, cb, lg, lb, tt):
    b, t, _ = p.shape
    keep = CONV_WIDTH - 1
    return pl.pallas_call(
        functools.partial(_conv_body, tt=tt),
        grid=(b, t // tt),
        in_specs=[pl.BlockSpec((1, tt, CONV_COLS), lambda i, j: (i, j, 0)),
                  pl.BlockSpec((1, keep, CONV_CH), lambda i, j: (i, 0, 0)),
                  _const_spec((CONV_WIDTH, CONV_CH)), _const_spec((1, CONV_CH)),
                  _const_spec((1, CONV_CH)), _const_spec((1, CONV_CH))],
        out_specs=[pl.BlockSpec((1, tt, CONV_CH), lambda i, j: (i, j, 0)),
                   pl.BlockSpec((1, keep, CONV_CH), lambda i, j: (i, 0, 0))],
        out_shape=[jax.ShapeDtypeStruct((b, t, CONV_CH), BF16),
                   jax.ShapeDtypeStruct((b, keep, CONV_CH), F32)],
        scratch_shapes=[pltpu.VMEM((tt + CONV_HALO, CONV_CH), F32)],
        compiler_params=_cparams(("arbitrary", "arbitrary")),
        name="conv_mixer",
    )(p, buf, w, cb.reshape(1, -1), lg.reshape(1, -1), lb.reshape(1, -1))


def _rwkv_pre_body(p_ref, prev_ref, mu_ref, w0_ref, wup_ref, a0_ref, aup_ref, gup_ref, kk_ref, ka_ref, rk_ref,
                   r_out, w_out, k_out, v_out, kk_out, b_out, g_out, bonus_out):
    p = p_ref[...]
    pm = p + (prev_ref[...] - p) * mu_ref[...]
    d = RWKV_DIM
    r = pm[:, 0:d]
    k = pm[:, d:2 * d]
    v = pm[:, 2 * d:3 * d]
    xw = pm[:, 3 * d:3 * d + DECAY_LORA]
    xa = pm[:, 3 * d + DECAY_LORA:3 * d + DECAY_LORA + AAA_LORA]
    xg = pm[:, 3 * d + DECAY_LORA + AAA_LORA:RWKV_COLS]
    bd = _head_blockdiag(d)
    w_log = -_softplus(-(w0_ref[...] + _dot3(jnp.tanh(xw), wup_ref[...]))) - 0.5
    decay = jnp.exp(-jnp.exp(w_log))
    a = _sigmoid(a0_ref[...] + _dot3(xa, aup_ref[...]))
    g = _dot3(_sigmoid(xg), gup_ref[...])
    kk = k * kk_ref[...]
    k = k * (1.0 + (a - 1.0) * ka_ref[...])
    kk = kk / jnp.maximum(jnp.sqrt(_dot_exact_rhs(kk * kk, bd)), 1e-12)
    r_out[...] = r
    w_out[...] = decay
    k_out[...] = k
    v_out[...] = v
    kk_out[...] = kk
    b_out[...] = kk * a
    g_out[...] = g
    bonus_out[...] = _dot_exact_rhs(r * k * rk_ref[...], bd) * v


def _rwkv_pre(p, prev, mu, w0, w_up, a0, a_up, g_up, k_k, k_a, r_k, tm):
    n = p.shape[0]
    vec = lambda a: a.reshape(1, -1)
    d = RWKV_DIM
    return pl.pallas_call(
        _rwkv_pre_body,
        grid=(n // tm,),
        in_specs=[_row_spec(tm, RWKV_COLS), _row_spec(tm, RWKV_COLS), _const_spec((1, RWKV_COLS)),
                  _const_spec((1, d)), _const_spec((DECAY_LORA, d)), _const_spec((1, d)),
                  _const_spec((AAA_LORA, d)), _const_spec((GATE_LORA, d)),
                  _const_spec((1, d)), _const_spec((1, d)), _const_spec((1, d))],
        out_specs=[_row_spec(tm, d)] * 8,
        out_shape=[jax.ShapeDtypeStruct((n, d), F32)] * 8,
        compiler_params=_cparams(("arbitrary",)),
        name="rwkv_pre",
    )(p, prev, vec(mu), vec(w0), w_up, vec(a0), a_up, g_up, vec(k_k), vec(k_a), vec(r_k))


def _wkv_body(r_ref, w_ref, k_ref, v_ref, kk_ref, b_ref, s0_ref, y_ref, sT_ref, s_scr, *, nc, tc):
    ti = pl.program_id(1)

    @pl.when(ti == 0)
    def _():
        s_scr[...] = s0_ref[...]

    eye = (lax.broadcasted_iota(jnp.int32, (HEAD_DIM, HEAD_DIM), 0)
           == lax.broadcasted_iota(jnp.int32, (HEAD_DIM, HEAD_DIM), 1))

    def step(t, carry):
        for c in range(nc):
            row = lambda ref: ref[c, pl.ds(t, 1), :]
            s = s_scr[c]
            sa = jnp.sum(s * row(kk_ref), axis=1, keepdims=True)
            vcol = jnp.sum(jnp.where(eye, row(v_ref), 0.0), axis=1, keepdims=True)
            s = s * row(w_ref) - sa * row(b_ref) + vcol * row(k_ref)
            s_scr[c] = s
            ycol = jnp.sum(s * row(r_ref), axis=1, keepdims=True)
            y_ref[c, pl.ds(t, 1), :] = jnp.sum(jnp.where(eye, ycol, 0.0), axis=0, keepdims=True)
        return carry

    lax.fori_loop(0, tc, step, 0)

    @pl.when(ti == pl.num_programs(1) - 1)
    def _():
        sT_ref[...] = s_scr[...]


def _wkv(r, w, k, v, kk, b, s0, nc, tc):
    n, t, _ = r.shape
    seq = pl.BlockSpec((nc, tc, HEAD_DIM), lambda i, j: (i, j, 0))
    st = pl.BlockSpec((nc, HEAD_DIM, HEAD_DIM), lambda i, j: (i, 0, 0))
    return pl.pallas_call(
        functools.partial(_wkv_body, nc=nc, tc=tc),
        grid=(n // nc, t // tc),
        in_specs=[seq] * 6 + [st],
        out_specs=[seq, st],
        out_shape=[jax.ShapeDtypeStruct((n, t, HEAD_DIM), F32),
                   jax.ShapeDtypeStruct((n, HEAD_DIM, HEAD_DIM), F32)],
        scratch_shapes=[pltpu.VMEM((nc, HEAD_DIM, HEAD_DIM), F32)],
        compiler_params=_cparams(("arbitrary", "arbitrary")),
        name="wkv",
    )(r, w, k, v, kk, b, s0)


def _sb_block(q, kb, vb, bias, rest, umat, mask):
    z = _dot_nt(q, kb) + bias
    sp = _softplus(z)
    log_rest = -sp
    if mask is not None:
        log_rest = jnp.where(mask, log_rest, 0.0)
    hi, lo = _split2(log_rest)
    after = _dot(hi, umat) + _dot(lo, umat)
    a = jnp.exp((z - sp) + after + rest)
    if mask is not None:
        a = jnp.where(mask, a, 0.0)
    out = _dot(a.astype(BF16), vb)
    return out, after[:, 0:1] + log_rest[:, 0:1]


def _attn_prompt_body(q_ref, k_ref, v_ref, bias_ref, o_ref, *, tile):
    qi = pl.program_id(2)
    q = q_ref[0, 0]
    bias = bias_ref[0]
    umat = _suffix_matrix(tile)
    causal = (lax.broadcasted_iota(jnp.int32, (tile, tile), 1)
              < lax.broadcasted_iota(jnp.int32, (tile, tile), 0))

    def kv(j):
        start = pl.multiple_of(j * tile, tile)
        return k_ref[0, 0, pl.ds(start, tile), :], v_ref[0, 0, pl.ds(start, tile), :]

    kb, vb = kv(qi)
    acc, rest = _sb_block(q, kb, vb, bias, jnp.zeros((tile, 1), F32), umat, causal)

    def body(jj, carry):
        acc, rest = carry
        kb, vb = kv(qi - 1 - jj)
        out, tot = _sb_block(q, kb, vb, bias, rest, umat, None)
        return acc + out, rest + tot

    acc, rest = lax.fori_loop(0, qi, body, (acc, rest))
    o_ref[0, 0] = acc.astype(o_ref.dtype)


def _attn_prompt(qb, kb, vb, bias):
    b, h, t, d = qb.shape
    tile = ATT_TILE
    bias_b = jnp.broadcast_to(bias.astype(F32)[:, None, None], (h, 1, tile))
    qspec = pl.BlockSpec((1, 1, tile, d), lambda i, j, l: (i, j, l, 0))
    kvspec = pl.BlockSpec((1, 1, t, d), lambda i, j, l: (i, j, 0, 0))
    return pl.pallas_call(
        functools.partial(_attn_prompt_body, tile=tile),
        grid=(b, h, t // tile),
        in_specs=[qspec, kvspec, kvspec, pl.BlockSpec((1, 1, tile), lambda i, j, l: (j, 0, 0))],
        out_specs=qspec,
        out_shape=jax.ShapeDtypeStruct((b, h, t, d), BF16),
        compiler_params=_cparams(("arbitrary", "arbitrary", "arbitrary")),
        name="attn_prompt",
    )(qb, kb, vb, bias_b)


def _attn_decode_body(pt_ref, q_ref, bias_ref, kn_ref, vn_ref, *refs, npg, nq):
    k_refs = refs[:npg]
    v_refs = refs[npg:2 * npg]
    o_ref, rest_ref, acc_ref = refs[2 * npg:]
    jb = pl.program_id(1)
    rows = ATT_HEADS * Q_PAD
    q = q_ref[0]
    bias = bias_ref[...]
    umat = _suffix_matrix(PAGE_SIZE)

    def visit(kf, vf, mask):
        out, tot = _sb_block(q, kf.astype(BF16), vf.astype(BF16), bias, rest_ref[...], umat, mask)
        acc_ref[...] += out
        rest_ref[...] += jnp.broadcast_to(tot, (rows, PAGE_SIZE))

    @pl.when(jb == 0)
    def _():
        rest_ref[...] = jnp.zeros((rows, PAGE_SIZE), F32)
        acc_ref[...] = jnp.zeros((rows, ATT_DIM), F32)
        kidx = lax.broadcasted_iota(jnp.int32, (rows, PAGE_SIZE), 1)
        qidx = lax.broadcasted_iota(jnp.int32, (rows, PAGE_SIZE), 0) % Q_PAD
        visit(kn_ref[0], vn_ref[0], (kidx < qidx) & (kidx < nq))

    for s in range(npg):
        visit(k_refs[s][0, 0], v_refs[s][0, 0], None)

    @pl.when(jb == pl.num_programs(1) - 1)
    def _():
        head_of_row = lax.broadcasted_iota(jnp.int32, (rows, ATT_DIM), 0) // Q_PAD
        head_of_col = lax.broadcasted_iota(jnp.int32, (rows, ATT_DIM), 1) // HEAD_DIM
        kept = jnp.where(head_of_row == head_of_col, acc_ref[...], 0.0)
        out = kept[0:Q_PAD]
        for h in range(1, ATT_HEADS):
            out = out + kept[h * Q_PAD:(h + 1) * Q_PAD]
        o_ref[0] = out.astype(o_ref.dtype)


def _attn_decode(li, qb, k_new, v_new, cache_k, cache_v, page_table, bias):
    b, t, _ = qb.shape
    n_pages = page_table.shape[1]
    npg = PAGES_PER_STEP
    rows = ATT_HEADS * Q_PAD
    qpad = jnp.pad(qb, ((0, 0), (0, Q_PAD - t), (0, 0)))
    head_of_col = jnp.arange(ATT_DIM) // HEAD_DIM
    qbd = jnp.where(head_of_col[None, None, None, :] == jnp.arange(ATT_HEADS)[None, :, None, None],
                    qpad[:, None], jnp.zeros((), BF16)).reshape(b, rows, ATT_DIM)
    bias_b = jnp.broadcast_to(jnp.repeat(bias.astype(F32), Q_PAD)[:, None], (rows, PAGE_SIZE))
    kn = jnp.pad(k_new, ((0, 0), (0, PAGE_SIZE - t), (0, 0)))
    vn = jnp.pad(v_new, ((0, 0), (0, PAGE_SIZE - t), (0, 0)))

    def page_spec(s):
        return pl.BlockSpec((1, 1, PAGE_SIZE, ATT_DIM),
                            lambda i, j, pt: (li, pt[i * n_pages + n_pages - 1 - (j * npg + s)], 0, 0))

    per_b = lambda r, c: pl.BlockSpec((1, r, c), lambda i, j, pt: (i, 0, 0))
    grid_spec = pltpu.PrefetchScalarGridSpec(
        num_scalar_prefetch=1,
        grid=(b, n_pages // npg),
        in_specs=[per_b(rows, ATT_DIM), pl.BlockSpec((rows, PAGE_SIZE), lambda i, j, pt: (0, 0)),
                  per_b(PAGE_SIZE, ATT_DIM), per_b(PAGE_SIZE, ATT_DIM)]
                 + [page_spec(s) for s in range(npg)] * 2,
        out_specs=per_b(Q_PAD, ATT_DIM),
        scratch_shapes=[pltpu.VMEM((rows, PAGE_SIZE), F32), pltpu.VMEM((rows, ATT_DIM), F32)],
    )
    out = pl.pallas_call(
        functools.partial(_attn_decode_body, npg=npg, nq=t),
        grid_spec=grid_spec,
        out_shape=jax.ShapeDtypeStruct((b, Q_PAD, ATT_DIM), BF16),
        compiler_params=_cparams(("arbitrary", "arbitrary")),
        name="attn_decode",
    )(page_table.reshape(-1), qbd, bias_b, kn, vn, *([cache_k] * npg), *([cache_v] * npg))
    return out[:, :t]


def _outproj_body(x_ref, yc_ref, yw_ref, bonus_ref, g_ref, ya_ref, lng_ref, lnb_ref, w_ref, o_ref):
    bd = _head_blockdiag(RWKV_DIM)
    y = yw_ref[...]
    inv = 1.0 / HEAD_DIM
    d = y - _dot_exact_rhs(y, bd) * inv
    var = _dot_exact_rhs(d * d, bd) * inv
    yn = d * lax.rsqrt(var + GN_EPS) * lng_ref[...] + lnb_ref[...]
    yr = ((yn + bonus_ref[...]) * g_ref[...]).astype(BF16)
    c0, c1 = CONV_CH, CONV_CH + RWKV_DIM
    o_ref[...] = (x_ref[...] + _dot(yc_ref[...], w_ref[0:c0, :]) + _dot(yr, w_ref[c0:c1, :])
                  + _dot(ya_ref[...], w_ref[c1:D_MODEL, :]))


def _outproj(x, yc, yw, bonus, g, ya, ln_g, ln_b, w_b, tm):
    n = x.shape[0]
    return pl.pallas_call(
        _outproj_body,
        grid=(n // tm,),
        in_specs=[_row_spec(tm, D_MODEL), _row_spec(tm, CONV_CH), _row_spec(tm, RWKV_DIM), _row_spec(tm, RWKV_DIM),
                  _row_spec(tm, RWKV_DIM), _row_spec(tm, ATT_DIM), _const_spec((1, RWKV_DIM)),
                  _const_spec((1, RWKV_DIM)), _const_spec((D_MODEL, D_MODEL))],
        out_specs=_row_spec(tm, D_MODEL),
        out_shape=jax.ShapeDtypeStruct((n, D_MODEL), F32),
        compiler_params=_cparams(("arbitrary",)),
        name="outproj",
    )(x, yc, yw, bonus, g, ya, ln_g.reshape(1, -1), ln_b.reshape(1, -1), w_b)


def _extract_top(x, n, lane):
    width = x.shape[1]
    rank = jnp.full(x.shape, float(n), F32)
    vals = []
    for i in range(n):
        m = jnp.max(x, axis=1, keepdims=True)
        first = jnp.min(jnp.where(x == m, lane, float(width)), axis=1, keepdims=True)
        hit = lane == first
        rank = jnp.where(hit, float(i), rank)
        x = jnp.where(hit, -jnp.inf, x)
        vals.append(m)
    return vals, rank


def _peer_route_body(x_ref, g_ref, wq_ref, sk_ref, h_out, e2_out, th_out, f1_out):
    tm = x_ref.shape[0]
    kk = PEER_TOPK
    hb = _rmsnorm(x_ref[...], g_ref[...]).astype(BF16)
    h_out[...] = hb
    lane = lax.broadcasted_iota(jnp.int32, (tm, N_KEYS), 1).astype(F32)
    clane_i = lax.broadcasted_iota(jnp.int32, (tm, kk * kk), 1)
    clane = clane_i.astype(F32)
    ca = clane_i // kk
    cb = clane_i % kk
    half = PEER_DK // 2
    for h in range(PEER_HEADS):
        q = _dot(hb, wq_ref[:, h * PEER_DK:(h + 1) * PEER_DK])
        s1 = _dot_nt(q[:, :half].astype(BF16), sk_ref[2 * h])
        s2 = _dot_nt(q[:, half:].astype(BF16), sk_ref[2 * h + 1])
        v1, rank1 = _extract_top(s1, kk, lane)
        v2, rank2 = _extract_top(s2, kk, lane)
        c1 = jnp.broadcast_to(v1[kk - 1], (tm, kk * kk))
        c2 = jnp.broadcast_to(v2[kk - 1], (tm, kk * kk))
        for i in range(kk - 1):
            c1 = jnp.where(ca == i, v1[i], c1)
            c2 = jnp.where(cb == i, v2[i], c2)
        vc, rankc = _extract_top(c1 + c2, kk, clane)
        z = jnp.exp(vc[0] - vc[0])
        for i in range(1, kk):
            z = z + jnp.exp(vc[i] - vc[0])
        taken2 = jnp.where(rankc < kk, c2, jnp.inf)
        th = jnp.full((tm, N_KEYS), 2.0, F32)
        for i in range(kk):
            lo = jnp.min(jnp.where(ca == i, taken2, jnp.inf), axis=1, keepdims=True)
            the = jnp.where(lo == jnp.inf, 2.0, jnp.exp(lo - v2[0]))
            th = jnp.where(rank1 == float(i), the, th)
        sl = slice(h * N_KEYS, (h + 1) * N_KEYS)
        th_out[:, sl] = th
        f1_out[:, sl] = jnp.where(rank1 < kk, jnp.exp(s1 - v1[0]), 0.0) / z
        e2_out[:, sl] = jnp.where(rank2 < kk, jnp.exp(s2 - v2[0]), 0.0)


def _peer_route(x, g, wq_b, sk_b, tm):
    n = x.shape[0]
    wide = PEER_HEADS * N_KEYS
    return pl.pallas_call(
        _peer_route_body,
        grid=(n // tm,),
        in_specs=[_row_spec(tm, D_MODEL), _const_spec((1, D_MODEL)), _const_spec((D_MODEL, PEER_HEADS * PEER_DK)),
                  _const_spec((2 * PEER_HEADS, N_KEYS, PEER_DK // 2))],
        out_specs=[_row_spec(tm, D_MODEL), _row_spec(tm, wide), _row_spec(tm, wide), _row_spec(tm, wide)],
        out_shape=[jax.ShapeDtypeStruct((n, D_MODEL), BF16)] + [jax.ShapeDtypeStruct((n, wide), F32)] * 3,
        compiler_params=_cparams(("arbitrary",)),
        name="peer_route",
    )(x, g.reshape(1, -1), wq_b, sk_b)


def _gelu_tanh(x):
    return 0.5 * x * (1.0 + jnp.tanh(0.7978845608028654 * (x + 0.044715 * (x * x * x))))


def _peer_expert_body(x_ref, h_ref, e2_ref, th_ref, f1_ref, u_ref, v_ref, o_ref, acc_ref, w_ref, *, sub):
    ec = pl.program_id(1)
    tm = x_ref.shape[0]

    @pl.when(ec == 0)
    def _():
        acc_ref[...] = jnp.zeros(acc_ref.shape, F32)

    def rows(i, carry):
        r0 = pl.multiple_of(i * sub, sub)
        rs = pl.ds(r0, sub)
        act = _gelu_tanh(_dot_nt(h_ref[rs, :], u_ref[...]))
        th = th_ref[0, rs, :]
        f1 = f1_ref[0, rs, :]
        for r in range(PEER_ROWS):
            gate = jnp.zeros((sub, N_KEYS), F32)
            for h in range(PEER_HEADS):
                e2 = e2_ref[rs, h * N_KEYS:(h + 1) * N_KEYS]
                c = h * PEER_ROWS + r
                gate = gate + jnp.where(e2 >= th[:, c:c + 1], e2, 0.0) * f1[:, c:c + 1]
            w_ref[rs, r * N_KEYS:(r + 1) * N_KEYS] = (gate * act[:, r * N_KEYS:(r + 1) * N_KEYS]).astype(BF16)
        return carry

    lax.fori_loop(0, tm // sub, rows, 0)
    acc_ref[...] += _dot(w_ref[...], v_ref[...])

    @pl.when(ec == pl.num_programs(1) - 1)
    def _():
        o_ref[...] = x_ref[...] + acc_ref[...]


def _peer_expert(x, hb, e2, th, f1, u_b, v_b, tm):
    n = x.shape[0]
    te = PEER_ROWS * N_KEYS
    ne = u_b.shape[0] // te
    cols = PEER_HEADS * PEER_ROWS
    regroup = lambda a: a.reshape(n, PEER_HEADS, ne, PEER_ROWS).transpose(2, 0, 1, 3).reshape(ne, n, cols)
    tok = lambda c: pl.BlockSpec((tm, c), lambda i, e: (i, 0))
    col = pl.BlockSpec((1, tm, cols), lambda i, e: (e, i, 0))
    tab = pl.BlockSpec((te, D_MODEL), lambda i, e: (e, 0))
    return pl.pallas_call(
        functools.partial(_peer_expert_body, sub=min(tm, 128)),
        grid=(n // tm, ne),
        in_specs=[tok(D_MODEL), tok(D_MODEL), tok(PEER_HEADS * N_KEYS), col, col, tab, tab],
        out_specs=tok(D_MODEL),
        out_shape=jax.ShapeDtypeStruct((n, D_MODEL), F32),
        scratch_shapes=[pltpu.VMEM((tm, D_MODEL), F32), pltpu.VMEM((tm, te), BF16)],
        compiler_params=_cparams(("arbitrary", "arbitrary")),
        name="peer_expert",
    )(x, hb, e2, regroup(th), regroup(f1), u_b, v_b)


def _final_norm_body(x_ref, g_ref, o_ref):
    o_ref[...] = _rmsnorm(x_ref[...], g_ref[...])


def _final_norm(x, g, tm):
    n = x.shape[0]
    return pl.pallas_call(
        _final_norm_body,
        grid=(n // tm,),
        in_specs=[_row_spec(tm, D_MODEL), _const_spec((1, D_MODEL))],
        out_specs=_row_spec(tm, D_MODEL),
        out_shape=jax.ShapeDtypeStruct((n, D_MODEL), F32),
        compiler_params=_cparams(("arbitrary",)),
        name="final_norm",
    )(x, g.reshape(1, -1))


def _heads_first(a, b, t):
    h = a.shape[1] // HEAD_DIM
    return a.reshape(b, t, h, HEAD_DIM).transpose(0, 2, 1, 3).reshape(b * h, t, HEAD_DIM)


def _tokens_first(a, b, t):
    h = a.shape[0] // b
    return a.reshape(b, h, t, HEAD_DIM).transpose(0, 2, 1, 3).reshape(b * t, h * HEAD_DIM)


def _layer(x, b, t, conv_buf, shift_prev, wkv0, attend, lw, tm, conv_tile, wkv_chains, wkv_tile):
    p_conv, p_rwkv, k, v, qb, kb, vb = _inproj(x, lw["norm1_g"], lw["w_in"], tm)

    y_conv, conv_new = _conv_mixer(p_conv.reshape(b, t, CONV_COLS), conv_buf, lw["conv_w"], lw["conv_b"],
                                   lw["conv_ln_g"], lw["conv_ln_b"], conv_tile)

    p3 = p_rwkv.reshape(b, t, RWKV_COLS)
    prev = jnp.concatenate([shift_prev[:, None, :], p3[:, :-1]], axis=1).reshape(b * t, RWKV_COLS)
    r, w, kr, vr, kk, bb, g, bonus = _rwkv_pre(p_rwkv, prev, lw["rwkv_mu"], lw["rwkv_w0"], lw["rwkv_w_up"],
                                               lw["rwkv_a0"], lw["rwkv_a_up"], lw["rwkv_g_up"], lw["rwkv_k_k"],
                                               lw["rwkv_k_a"], lw["rwkv_r_k"], tm)
    hf = lambda a: _heads_first(a, b, t)
    y_wkv, wkv_new = _wkv(hf(r), hf(w), hf(kr), hf(vr), hf(kk), hf(bb),
                          wkv0.reshape(b * RWKV_HEADS, HEAD_DIM, HEAD_DIM), wkv_chains, wkv_tile)
    y_wkv = _tokens_first(y_wkv, b, t)

    y_att = attend(qb, k, v, kb, vb)

    x = _outproj(x, y_conv.reshape(b * t, CONV_CH), y_wkv, bonus, g, y_att, lw["rwkv_ln_g"], lw["rwkv_ln_b"],
                 lw["w_out"], tm)
    hb, e2, th, f1 = _peer_route(x, lw["norm2_g"], lw["peer_w_query"], lw["peer_sub_keys"], min(tm, 256))
    x = _peer_expert(x, hb, e2, th, f1, lw["peer_u"], lw["peer_v"], tm)
    states = (k.reshape(b, t, ATT_HEADS, HEAD_DIM), v.reshape(b, t, ATT_HEADS, HEAD_DIM), conv_new,
              p3[:, -1], wkv_new.reshape(b, RWKV_HEADS, HEAD_DIM, HEAD_DIM))
    return x, states


def kernel(x_prompt, x_sample, cache_k, cache_v, state_conv, state_shift, state_wkv, page_table, norm1_g, w_in, conv_w, conv_b, conv_ln_g, conv_ln_b, rwkv_mu, rwkv_w0, rwkv_w_up, rwkv_a0, rwkv_a_up, rwkv_g_up, rwkv_k_k, rwkv_k_a, rwkv_r_k, rwkv_ln_g, rwkv_ln_b, att_bias, w_out, norm2_g, peer_w_query, peer_sub_keys, peer_u, peer_v, final_g):
    depth = w_in.shape[0]
    bp, tp, _ = x_prompt.shape
    bs, ts, _ = x_sample.shape
    n_pool = cache_k.shape[1]
    ck = cache_k.reshape(depth, n_pool, PAGE_SIZE, ATT_DIM)
    cv = cache_v.reshape(depth, n_pool, PAGE_SIZE, ATT_DIM)
    xp = x_prompt.reshape(bp * tp, D_MODEL)
    xs = x_sample.reshape(bs * ts, D_MODEL)
    st_p, st_s = [], []
    for li in range(depth):
        lw = dict(
            norm1_g=norm1_g[li], w_in=w_in[li].astype(BF16), conv_w=conv_w[li], conv_b=conv_b[li],
            conv_ln_g=conv_ln_g[li], conv_ln_b=conv_ln_b[li], rwkv_mu=rwkv_mu[li], rwkv_w0=rwkv_w0[li],
            rwkv_w_up=rwkv_w_up[li], rwkv_a0=rwkv_a0[li], rwkv_a_up=rwkv_a_up[li], rwkv_g_up=rwkv_g_up[li],
            rwkv_k_k=rwkv_k_k[li], rwkv_k_a=rwkv_k_a[li], rwkv_r_k=rwkv_r_k[li].reshape(-1),
            rwkv_ln_g=rwkv_ln_g[li], rwkv_ln_b=rwkv_ln_b[li], w_out=w_out[li].astype(BF16), norm2_g=norm2_g[li],
            peer_w_query=peer_w_query[li].astype(BF16),
            peer_sub_keys=peer_sub_keys[li].astype(BF16).reshape(2 * PEER_HEADS, N_KEYS, PEER_DK // 2),
            peer_u=peer_u[li].astype(BF16), peer_v=peer_v[li].astype(BF16))
        bias = att_bias[li]

        def attend_prompt(qb, k, v, kb, vb):
            hf = lambda a: a.reshape(bp, tp, ATT_HEADS, HEAD_DIM).transpose(0, 2, 1, 3)
            o = _attn_prompt(hf(qb), hf(kb), hf(vb), bias)
            return o.transpose(0, 2, 1, 3).reshape(bp * tp, ATT_DIM)

        def attend_sample(qb, k, v, kb, vb):
            o = _attn_decode(li, qb.reshape(bs, ts, ATT_DIM), k.reshape(bs, ts, ATT_DIM),
                             v.reshape(bs, ts, ATT_DIM), ck, cv, page_table, bias)
            return o.reshape(bs * ts, ATT_DIM)

        xp, new_p = _layer(xp, bp, tp, jnp.zeros((bp, CONV_WIDTH - 1, CONV_CH), F32),
                           jnp.zeros((bp, RWKV_COLS), F32), jnp.zeros((bp, RWKV_HEADS, HEAD_DIM, HEAD_DIM), F32),
                           attend_prompt, lw, tm=512, conv_tile=512, wkv_chains=bp * RWKV_HEADS, wkv_tile=256)
        xs, new_s = _layer(xs, bs, ts, state_conv[li], state_shift[li], state_wkv[li],
                           attend_sample, lw, tm=bs * ts, conv_tile=ts, wkv_chains=bp * RWKV_HEADS, wkv_tile=ts)
        st_p.append(new_p)
        st_s.append(new_s)
    y_prompt = _final_norm(xp, final_g, 512).reshape(bp, tp, D_MODEL)
    y_sample = _final_norm(xs, final_g, bs * ts).reshape(bs, ts, D_MODEL)
    stack = lambda st, j: jnp.stack([s[j] for s in st], axis=0)
    return (y_prompt, y_sample) + tuple(stack(st_p, j) for j in range(5)) + tuple(stack(st_s, j) for j in range(5))
```

```python
import functools

import jax
import jax.numpy as jnp
from jax import lax
from jax.experimental import pallas as pl
from jax.experimental.pallas import tpu as pltpu

F32 = jnp.float32
BF16 = jnp.bfloat16

D_MODEL = 1024
HEAD_DIM = 64
CONV_CH = 256
CONV_WIDTH = 31
CONV_HALO = 32
RWKV_HEADS = 6
RWKV_DIM = 384
ATT_HEADS = 6
ATT_DIM = 384
DECAY_LORA = 64
AAA_LORA = 64
GATE_LORA = 128
CONV_COLS = 512
RWKV_COLS = 1408
IN_COLS = 3072
PAGE_SIZE = 128
PEER_HEADS = 8
PEER_DK = 256
N_KEYS = 128
PEER_TOPK = 16
RMS_EPS = 1e-5
LN_EPS = 1e-5
GN_EPS = 64e-5

ATT_TILE = 256
PAGES_PER_STEP = 8
Q_PAD = 8
PEER_ROWS = 8
WKV_CHUNK_LOG2 = 6
WKV_CHUNK = 1 << WKV_CHUNK_LOG2
WKV_BLOCK = 128
VMEM_LIMIT = 56 * 1024 * 1024


def _cparams(sem):
    return pltpu.CompilerParams(dimension_semantics=sem, vmem_limit_bytes=VMEM_LIMIT)


def _sigmoid(x):
    return 1.0 / (1.0 + jnp.exp(-x))


def _softplus(x):
    return jnp.maximum(x, 0.0) + jnp.log(1.0 + jnp.exp(-jnp.abs(x)))


def _dot(a, b):
    return jnp.dot(a, b, preferred_element_type=F32)


def _dot_nt(a, b):
    return lax.dot_general(a, b, (((1,), (1,)), ((), ())), preferred_element_type=F32)


def _split2(x):
    hi = x.astype(BF16)
    lo = (x - hi.astype(F32)).astype(BF16)
    return hi, lo


def _dot3(a, b):
    ah, al = _split2(a)
    bh, bl = _split2(b)
    return _dot(ah, bh) + (_dot(ah, bl) + _dot(al, bh))


def _dot_exact_rhs(x, m):
    hi, lo = _split2(x)
    return _dot(hi, m) + _dot(lo, m)


def _rmsnorm(x, g):
    return x * lax.rsqrt(jnp.mean(x * x, axis=-1, keepdims=True) + RMS_EPS) * g


def _head_blockdiag(n):
    r = lax.broadcasted_iota(jnp.int32, (n, n), 0) // HEAD_DIM
    c = lax.broadcasted_iota(jnp.int32, (n, n), 1) // HEAD_DIM
    return (r == c).astype(BF16)


def _suffix_matrix(n):
    r = lax.broadcasted_iota(jnp.int32, (n, n), 0)
    c = lax.broadcasted_iota(jnp.int32, (n, n), 1)
    return (r > c).astype(BF16)


def _row_spec(tm, cols):
    return pl.BlockSpec((tm, cols), lambda i: (i, 0))


def _const_spec(shape):
    return pl.BlockSpec(shape, lambda *_: (0,) * len(shape))


def _inproj_body(x_ref, g_ref, w_ref, conv_ref, rwkv_ref, k_ref, v_ref, qb_ref, kb_ref, vb_ref):
    h = _rmsnorm(x_ref[...], g_ref[...]).astype(BF16)
    o0, o1, o2, o3 = CONV_COLS, CONV_COLS + RWKV_COLS, CONV_COLS + RWKV_COLS + ATT_DIM, IN_COLS - ATT_DIM
    conv_ref[...] = _dot(h, w_ref[:, 0:o0])
    rwkv_ref[...] = _dot(h, w_ref[:, o0:o1])
    q = _dot(h, w_ref[:, o1:o2])
    k = _dot(h, w_ref[:, o2:o3])
    v = _dot(h, w_ref[:, o3:IN_COLS])
    k_ref[...] = k
    v_ref[...] = v
    qb_ref[...] = (q * (HEAD_DIM ** -0.5)).astype(BF16)
    kb_ref[...] = k.astype(BF16)
    vb_ref[...] = v.astype(BF16)


def _inproj(x, g, w_b, tm):
    n = x.shape[0]
    widths = (CONV_COLS, RWKV_COLS, ATT_DIM, ATT_DIM, ATT_DIM, ATT_DIM, ATT_DIM)
    dtypes = (F32, F32, F32, F32, BF16, BF16, BF16)
    return pl.pallas_call(
        _inproj_body,
        grid=(n // tm,),
        in_specs=[_row_spec(tm, D_MODEL), _const_spec((1, D_MODEL)), _const_spec((D_MODEL, IN_COLS))],
        out_specs=[_row_spec(tm, c) for c in widths],
        out_shape=[jax.ShapeDtypeStruct((n, c), d) for c, d in zip(widths, dtypes)],
        compiler_params=_cparams(("arbitrary",)),
        name="inproj",
    )(x, g.reshape(1, -1), w_b)


def _conv_body(p_ref, buf_ref, w_ref, cb_ref, lg_ref, lb_ref, y_ref, nb_ref, ext_ref, *, tt):
    keep = CONV_WIDTH - 1
    lead = CONV_HALO - keep

    @pl.when(pl.program_id(1) == 0)
    def _():
        ext_ref[0:CONV_HALO, :] = jnp.zeros((CONV_HALO, CONV_CH), F32)
        ext_ref[lead:CONV_HALO, :] = buf_ref[0]

    p = p_ref[0]
    ext_ref[CONV_HALO:CONV_HALO + tt, :] = p[:, :CONV_CH] * _sigmoid(p[:, CONV_CH:])
    acc = jnp.zeros((tt, CONV_CH), F32)
    for j in range(CONV_WIDTH):
        acc = acc + w_ref[j:j + 1, :] * ext_ref[lead + j:lead + j + tt, :]
    y = acc + cb_ref[...]
    mu = jnp.mean(y, axis=-1, keepdims=True)
    d = y - mu
    var = jnp.mean(d * d, axis=-1, keepdims=True)
    y = d * lax.rsqrt(var + LN_EPS) * lg_ref[...] + lb_ref[...]
    y_ref[0] = (y * _sigmoid(y)).astype(y_ref.dtype)
    nb_ref[0] = ext_ref[tt + lead:tt + CONV_HALO, :]
    tail = ext_ref[tt:tt + CONV_HALO, :]
    ext_ref[0:CONV_HALO, :] = tail


def _conv_mixer(p, buf, w, cb, lg, lb, tt):
    b, t, _ = p.shape
    keep = CONV_WIDTH - 1
    return pl.pallas_call(
        functools.partial(_conv_body, tt=tt),
        grid=(b, t // tt),
        in_specs=[pl.BlockSpec((1, tt, CONV_COLS), lambda i, j: (i, j, 0)),
                  pl.BlockSpec((1, keep, CONV_CH), lambda i, j: (i, 0, 0)),
                  _const_spec((CONV_WIDTH, CONV_CH)), _const_spec((1, CONV_CH)),
                  _const_spec((1, CONV_CH)), _const_spec((1, CONV_CH))],
        out_specs=[pl.BlockSpec((1, tt, CONV_CH), lambda i, j: (i, j, 0)),
                   pl.BlockSpec((1, keep, CONV_CH), lambda i, j: (i, 0, 0))],
        out_shape=[jax.ShapeDtypeStruct((b, t, CONV_CH), BF16),
                   jax.ShapeDtypeStruct((b, keep, CONV_CH), F32)],
        scratch_shapes=[pltpu.VMEM((tt + CONV_HALO, CONV_CH), F32)],
        compiler_params=_cparams(("arbitrary", "arbitrary")),
        name="conv_mixer",
    )(p, buf, w, cb.reshape(1, -1), lg.reshape(1, -1), lb.reshape(1, -1))


def _rwkv_pre_body(p_ref, prev_ref, mu_ref, w0_ref, wup_ref, a0_ref, aup_ref, gup_ref, kk_ref, ka_ref, rk_ref,
                   r_out, w_out, k_out, v_out, kk_out, b_out, g_out, bonus_out):
    p = p_ref[...]
    pm = p + (prev_ref[...] - p) * mu_ref[...]
    d = RWKV_DIM
    r = pm[:, 0:d]
    k = pm[:, d:2 * d]
    v = pm[:, 2 * d:3 * d]
    xw = pm[:, 3 * d:3 * d + DECAY_LORA]
    xa = pm[:, 3 * d + DECAY_LORA:3 * d + DECAY_LORA + AAA_LORA]
    xg = pm[:, 3 * d + DECAY_LORA + AAA_LORA:RWKV_COLS]
    bd = _head_blockdiag(d)
    w_log = -_softplus(-(w0_ref[...] + _dot3(jnp.tanh(xw), wup_ref[...]))) - 0.5
    log_decay = -jnp.exp(w_log)
    a = _sigmoid(a0_ref[...] + _dot3(xa, aup_ref[...]))
    g = _dot3(_sigmoid(xg), gup_ref[...])
    kk = k * kk_ref[...]
    k = k * (1.0 + (a - 1.0) * ka_ref[...])
    kk = kk / jnp.maximum(jnp.sqrt(_dot_exact_rhs(kk * kk, bd)), 1e-12)
    r_out[...] = r
    w_out[...] = log_decay
    k_out[...] = k
    v_out[...] = v
    kk_out[...] = kk
    b_out[...] = kk * a
    g_out[...] = g
    bonus_out[...] = _dot_exact_rhs(r * k * rk_ref[...], bd) * v


def _rwkv_pre(p, prev, mu, w0, w_up, a0, a_up, g_up, k_k, k_a, r_k, tm):
    n = p.shape[0]
    vec = lambda a: a.reshape(1, -1)
    d = RWKV_DIM
    return pl.pallas_call(
        _rwkv_pre_body,
        grid=(n // tm,),
        in_specs=[_row_spec(tm, RWKV_COLS), _row_spec(tm, RWKV_COLS), _const_spec((1, RWKV_COLS)),
                  _const_spec((1, d)), _const_spec((DECAY_LORA, d)), _const_spec((1, d)),
                  _const_spec((AAA_LORA, d)), _const_spec((GATE_LORA, d)),
                  _const_spec((1, d)), _const_spec((1, d)), _const_spec((1, d))],
        out_specs=[_row_spec(tm, d)] * 8,
        out_shape=[jax.ShapeDtypeStruct((n, d), F32)] * 8,
        compiler_params=_cparams(("arbitrary",)),
        name="rwkv_pre",
    )(p, prev, vec(mu), vec(w0), w_up, vec(a0), a_up, g_up, vec(k_k), vec(k_a), vec(r_k))


def _wkv_body(r_ref, w_ref, k_ref, v_ref, kk_ref, b_ref, s0_ref, y_ref, sT_ref, s_scr, *, nc, tc):
    ti = pl.program_id(1)

    @pl.when(ti == 0)
    def _():
        s_scr[...] = s0_ref[...]

    eye = (lax.broadcasted_iota(jnp.int32, (HEAD_DIM, HEAD_DIM), 0)
           == lax.broadcasted_iota(jnp.int32, (HEAD_DIM, HEAD_DIM), 1))

    def step(t, carry):
        for c in range(nc):
            row = lambda ref: ref[c, pl.ds(t, 1), :]
            s = s_scr[c]
            sa = jnp.sum(s * row(kk_ref), axis=1, keepdims=True)
            vcol = jnp.sum(jnp.where(eye, row(v_ref), 0.0), axis=1, keepdims=True)
            s = s * jnp.exp(row(w_ref)) - sa * row(b_ref) + vcol * row(k_ref)
            s_scr[c] = s
            ycol = jnp.sum(s * row(r_ref), axis=1, keepdims=True)
            y_ref[c, pl.ds(t, 1), :] = jnp.sum(jnp.where(eye, ycol, 0.0), axis=0, keepdims=True)
        return carry

    lax.fori_loop(0, tc, step, 0)

    @pl.when(ti == pl.num_programs(1) - 1)
    def _():
        sT_ref[...] = s_scr[...]


def _wkv(r, w, k, v, kk, b, s0, nc, tc):
    n, t, _ = r.shape
    seq = pl.BlockSpec((nc, tc, HEAD_DIM), lambda i, j: (i, j, 0))
    st = pl.BlockSpec((nc, HEAD_DIM, HEAD_DIM), lambda i, j: (i, 0, 0))
    return pl.pallas_call(
        functools.partial(_wkv_body, nc=nc, tc=tc),
        grid=(n // nc, t // tc),
        in_specs=[seq] * 6 + [st],
        out_specs=[seq, st],
        out_shape=[jax.ShapeDtypeStruct((n, t, HEAD_DIM), F32),
                   jax.ShapeDtypeStruct((n, HEAD_DIM, HEAD_DIM), F32)],
        scratch_shapes=[pltpu.VMEM((nc, HEAD_DIM, HEAD_DIM), F32)],
        compiler_params=_cparams(("arbitrary", "arbitrary")),
        name="wkv",
    )(r, w, k, v, kk, b, s0)


def _mm(a, b):
    return _dot3(a, b)


def _mm_nt(a, b):
    ah, al = _split2(a)
    bh, bl = _split2(b)
    return _dot_nt(ah, bh) + (_dot_nt(ah, bl) + _dot_nt(al, bh))


def _split3(x):
    p1 = x.astype(BF16)
    r1 = x - p1.astype(F32)
    p2 = r1.astype(BF16)
    p3 = (r1 - p2.astype(F32)).astype(BF16)
    return p1, p2, p3


def _wkv_chunk_body(r_ref, lw_ref, k_ref, v_ref, kk_ref, b_ref, lwt_ref, kkt_ref, vt_ref, s0_ref,
                    y_ref, sT_ref, s_scr, *, nc):
    ti = pl.program_id(1)
    c_len = WKV_CHUNK

    @pl.when(ti == 0)
    def _():
        s_scr[...] = s0_ref[...]

    row = lax.broadcasted_iota(jnp.int32, (c_len, c_len), 0)
    col = lax.broadcasted_iota(jnp.int32, (c_len, c_len), 1)
    strict = row > col
    incl = row >= col
    lower = incl.astype(BF16)
    upper = (row <= col).astype(BF16)
    eye = (row == col).astype(F32)

    for j, c in [(j, c) for j in range(WKV_BLOCK // c_len) for c in range(nc)]:
        ts = slice(j * c_len, (j + 1) * c_len)
        r, lw, k, v, kk, b = (ref[c, ts, :] for ref in (r_ref, lw_ref, k_ref, v_ref, kk_ref, b_ref))
        lwt, kkt, vt = (ref[c, :, ts] for ref in (lwt_ref, kkt_ref, vt_ref))
        cum = sum(_dot(lower, piece) for piece in _split3(lw))
        cumt = sum(_dot(piece, upper) for piece in _split3(lwt))
        w_incl = jnp.exp(cum)
        w_last = w_incl[c_len - 1:c_len, :]
        inv = jnp.exp(-cum)
        qt = kk * jnp.exp(cum - lw)
        qtt = kkt * jnp.exp(cumt - lwt)
        rt = r * w_incl
        kt = k * inv
        bt = b * inv
        a_qb = jnp.where(strict, _mm_nt(qt, bt), 0.0)
        a_qk = jnp.where(strict, _mm_nt(qt, kt), 0.0)
        a_rk = jnp.where(incl, _mm_nt(rt, kt), 0.0)
        a_rb = jnp.where(incl, _mm_nt(rt, bt), 0.0)
        tinv = eye - a_qb
        power = a_qb
        for _ in range(WKV_CHUNK_LOG2 - 1):
            power = _mm(power, power)
            tinv = tinv + _mm(tinv, power)
        tq = _mm(tinv, qt)
        tav = _mm(tinv, _mm(a_qk, v))
        rq = rt - _mm(a_rb, tq)
        yv = _mm(a_rk, v) - _mm(a_rb, tav)
        tq_t = _mm_nt(qtt, tinv)
        tav_t = _mm_nt(_mm_nt(vt, a_qk), tinv)
        s = s_scr[c]
        sa_t = _mm(s, tq_t) + tav_t
        y_ref[c, ts, :] = _mm_nt(rq, s) + yv
        s_scr[c] = s * w_last + _mm(vt, kt * w_last) - _mm(sa_t, bt * w_last)

    @pl.when(ti == pl.num_programs(1) - 1)
    def _():
        sT_ref[...] = s_scr[...]


def _wkv_chunked(r, lw, k, v, kk, b, s0, nc):
    n, t, _ = r.shape
    tr = lambda a: a.transpose(0, 2, 1)
    seq = pl.BlockSpec((nc, WKV_BLOCK, HEAD_DIM), lambda i, j: (i, j, 0))
    seq_t = pl.BlockSpec((nc, HEAD_DIM, WKV_BLOCK), lambda i, j: (i, 0, j))
    st = pl.BlockSpec((nc, HEAD_DIM, HEAD_DIM), lambda i, j: (i, 0, 0))
    return pl.pallas_call(
        functools.partial(_wkv_chunk_body, nc=nc),
        grid=(n // nc, t // WKV_BLOCK),
        in_specs=[seq] * 6 + [seq_t] * 3 + [st],
        out_specs=[seq, st],
        out_shape=[jax.ShapeDtypeStruct((n, t, HEAD_DIM), F32),
                   jax.ShapeDtypeStruct((n, HEAD_DIM, HEAD_DIM), F32)],
        scratch_shapes=[pltpu.VMEM((nc, HEAD_DIM, HEAD_DIM), F32)],
        compiler_params=_cparams(("arbitrary", "arbitrary")),
        name="wkv_chunked",
    )(r, lw, k, v, kk, b, tr(lw), tr(kk), tr(v), s0)


def _sb_block(q, kb, vb, bias, rest, umat, mask):
    z = _dot_nt(q, kb) + bias
    sp = _softplus(z)
    log_rest = -sp
    if mask is not None:
        log_rest = jnp.where(mask, log_rest, 0.0)
    hi, lo = _split2(log_rest)
    after = _dot(hi, umat) + _dot(lo, umat)
    a = jnp.exp((z - sp) + after + rest)
    if mask is not None:
        a = jnp.where(mask, a, 0.0)
    out = _dot(a.astype(BF16), vb)
    return out, after[:, 0:1] + log_rest[:, 0:1]


def _attn_prompt_body(q_ref, k_ref, v_ref, bias_ref, o_ref, *, tile):
    qi = pl.program_id(2)
    q = q_ref[0, 0]
    bias = bias_ref[0]
    umat = _suffix_matrix(tile)
    causal = (lax.broadcasted_iota(jnp.int32, (tile, tile), 1)
              < lax.broadcasted_iota(jnp.int32, (tile, tile), 0))

    def kv(j):
        start = pl.multiple_of(j * tile, tile)
        return k_ref[0, 0, pl.ds(start, tile), :], v_ref[0, 0, pl.ds(start, tile), :]

    kb, vb = kv(qi)
    acc, rest = _sb_block(q, kb, vb, bias, jnp.zeros((tile, 1), F32), umat, causal)

    def body(jj, carry):
        acc, rest = carry
        kb, vb = kv(qi - 1 - jj)
        out, tot = _sb_block(q, kb, vb, bias, rest, umat, None)
        return acc + out, rest + tot

    acc, rest = lax.fori_loop(0, qi, body, (acc, rest))
    o_ref[0, 0] = acc.astype(o_ref.dtype)


def _attn_prompt(qb, kb, vb, bias):
    b, h, t, d = qb.shape
    tile = ATT_TILE
    bias_b = jnp.broadcast_to(bias.astype(F32)[:, None, None], (h, 1, tile))
    qspec = pl.BlockSpec((1, 1, tile, d), lambda i, j, l: (i, j, l, 0))
    kvspec = pl.BlockSpec((1, 1, t, d), lambda i, j, l: (i, j, 0, 0))
    return pl.pallas_call(
        functools.partial(_attn_prompt_body, tile=tile),
        grid=(b, h, t // tile),
        in_specs=[qspec, kvspec, kvspec, pl.BlockSpec((1, 1, tile), lambda i, j, l: (j, 0, 0))],
        out_specs=qspec,
        out_shape=jax.ShapeDtypeStruct((b, h, t, d), BF16),
        compiler_params=_cparams(("arbitrary", "arbitrary", "arbitrary")),
        name="attn_prompt",
    )(qb, kb, vb, bias_b)


def _attn_decode_body(pt_ref, q_ref, bias_ref, kn_ref, vn_ref, *refs, npg, nq):
    k_refs = refs[:npg]
    v_refs = refs[npg:2 * npg]
    o_ref, rest_ref, acc_ref = refs[2 * npg:]
    jb = pl.program_id(1)
    rows = ATT_HEADS * Q_PAD
    q = q_ref[0]
    bias = bias_ref[...]
    umat = _suffix_matrix(PAGE_SIZE)

    def visit(kf, vf, mask):
        out, tot = _sb_block(q, kf.astype(BF16), vf.astype(BF16), bias, rest_ref[...], umat, mask)
        acc_ref[...] += out
        rest_ref[...] += jnp.broadcast_to(tot, (rows, PAGE_SIZE))

    @pl.when(jb == 0)
    def _():
        rest_ref[...] = jnp.zeros((rows, PAGE_SIZE), F32)
        acc_ref[...] = jnp.zeros((rows, ATT_DIM), F32)
        kidx = lax.broadcasted_iota(jnp.int32, (rows, PAGE_SIZE), 1)
        qidx = lax.broadcasted_iota(jnp.int32, (rows, PAGE_SIZE), 0) % Q_PAD
        visit(kn_ref[0], vn_ref[0], (kidx < qidx) & (kidx < nq))

    for s in range(npg):
        visit(k_refs[s][0, 0], v_refs[s][0, 0], None)

    @pl.when(jb == pl.num_programs(1) - 1)
    def _():
        head_of_row = lax.broadcasted_iota(jnp.int32, (rows, ATT_DIM), 0) // Q_PAD
        head_of_col = lax.broadcasted_iota(jnp.int32, (rows, ATT_DIM), 1) // HEAD_DIM
        kept = jnp.where(head_of_row == head_of_col, acc_ref[...], 0.0)
        out = kept[0:Q_PAD]
        for h in range(1, ATT_HEADS):
            out = out + kept[h * Q_PAD:(h + 1) * Q_PAD]
        o_ref[0] = out.astype(o_ref.dtype)


def _attn_decode(li, qb, k_new, v_new, cache_k, cache_v, page_table, bias):
    b, t, _ = qb.shape
    n_pages = page_table.shape[1]
    npg = PAGES_PER_STEP
    rows = ATT_HEADS * Q_PAD
    qpad = jnp.pad(qb, ((0, 0), (0, Q_PAD - t), (0, 0)))
    head_of_col = jnp.arange(ATT_DIM) // HEAD_DIM
    qbd = jnp.where(head_of_col[None, None, None, :] == jnp.arange(ATT_HEADS)[None, :, None, None],
                    qpad[:, None], jnp.zeros((), BF16)).reshape(b, rows, ATT_DIM)
    bias_b = jnp.broadcast_to(jnp.repeat(bias.astype(F32), Q_PAD)[:, None], (rows, PAGE_SIZE))
    kn = jnp.pad(k_new, ((0, 0), (0, PAGE_SIZE - t), (0, 0)))
    vn = jnp.pad(v_new, ((0, 0), (0, PAGE_SIZE - t), (0, 0)))

    def page_spec(s):
        return pl.BlockSpec((1, 1, PAGE_SIZE, ATT_DIM),
                            lambda i, j, pt: (li, pt[i * n_pages + n_pages - 1 - (j * npg + s)], 0, 0))

    per_b = lambda r, c: pl.BlockSpec((1, r, c), lambda i, j, pt: (i, 0, 0))
    grid_spec = pltpu.PrefetchScalarGridSpec(
        num_scalar_prefetch=1,
        grid=(b, n_pages // npg),
        in_specs=[per_b(rows, ATT_DIM), pl.BlockSpec((rows, PAGE_SIZE), lambda i, j, pt: (0, 0)),
                  per_b(PAGE_SIZE, ATT_DIM), per_b(PAGE_SIZE, ATT_DIM)]
                 + [page_spec(s) for s in range(npg)] * 2,
        out_specs=per_b(Q_PAD, ATT_DIM),
        scratch_shapes=[pltpu.VMEM((rows, PAGE_SIZE), F32), pltpu.VMEM((rows, ATT_DIM), F32)],
    )
    out = pl.pallas_call(
        functools.partial(_attn_decode_body, npg=npg, nq=t),
        grid_spec=grid_spec,
        out_shape=jax.ShapeDtypeStruct((b, Q_PAD, ATT_DIM), BF16),
        compiler_params=_cparams(("arbitrary", "arbitrary")),
        name="attn_decode",
    )(page_table.reshape(-1), qbd, bias_b, kn, vn, *([cache_k] * npg), *([cache_v] * npg))
    return out[:, :t]


def _outproj_body(x_ref, yc_ref, yw_ref, bonus_ref, g_ref, ya_ref, lng_ref, lnb_ref, w_ref, o_ref):
    bd = _head_blockdiag(RWKV_DIM)
    y = yw_ref[...]
    inv = 1.0 / HEAD_DIM
    d = y - _dot_exact_rhs(y, bd) * inv
    var = _dot_exact_rhs(d * d, bd) * inv
    yn = d * lax.rsqrt(var + GN_EPS) * lng_ref[...] + lnb_ref[...]
    yr = ((yn + bonus_ref[...]) * g_ref[...]).astype(BF16)
    c0, c1 = CONV_CH, CONV_CH + RWKV_DIM
    o_ref[...] = (x_ref[...] + _dot(yc_ref[...], w_ref[0:c0, :]) + _dot(yr, w_ref[c0:c1, :])
                  + _dot(ya_ref[...], w_ref[c1:D_MODEL, :]))


def _outproj(x, yc, yw, bonus, g, ya, ln_g, ln_b, w_b, tm):
    n = x.shape[0]
    return pl.pallas_call(
        _outproj_body,
        grid=(n // tm,),
        in_specs=[_row_spec(tm, D_MODEL), _row_spec(tm, CONV_CH), _row_spec(tm, RWKV_DIM), _row_spec(tm, RWKV_DIM),
                  _row_spec(tm, RWKV_DIM), _row_spec(tm, ATT_DIM), _const_spec((1, RWKV_DIM)),
                  _const_spec((1, RWKV_DIM)), _const_spec((D_MODEL, D_MODEL))],
        out_specs=_row_spec(tm, D_MODEL),
        out_shape=jax.ShapeDtypeStruct((n, D_MODEL), F32),
        compiler_params=_cparams(("arbitrary",)),
        name="outproj",
    )(x, yc, yw, bonus, g, ya, ln_g.reshape(1, -1), ln_b.reshape(1, -1), w_b)


def _extract_top(x, n, idx):
    width = x.shape[0]
    orig = x
    vals = []
    for _ in range(n):
        m = jnp.max(x, axis=0, keepdims=True)
        first = jnp.min(jnp.where(x == m, idx, float(width)), axis=0, keepdims=True)
        x = jnp.where(idx == first, -jnp.inf, x)
        vals.append(m)
    return vals, x != orig


def _peer_route_body(x_ref, g_ref, wq_ref, sk_ref, h_out, e2_out, th_out, f1_out):
    tm = x_ref.shape[0]
    kk = PEER_TOPK
    hb = _rmsnorm(x_ref[...], g_ref[...]).astype(BF16)
    h_out[...] = hb
    idx = lax.broadcasted_iota(jnp.int32, (N_KEYS, tm), 0).astype(F32)
    cidx = lax.broadcasted_iota(jnp.int32, (kk * kk, tm), 0).astype(F32)
    sidx = lax.broadcasted_iota(jnp.int32, (kk, tm), 0).astype(F32)
    half = PEER_DK // 2
    for h in range(PEER_HEADS):
        q = _dot(hb, wq_ref[:, h * PEER_DK:(h + 1) * PEER_DK])
        s1 = _dot_nt(sk_ref[2 * h], q[:, :half].astype(BF16))
        s2 = _dot_nt(sk_ref[2 * h + 1], q[:, half:].astype(BF16))
        v1, in1 = _extract_top(s1, kk, idx)
        v2, in2 = _extract_top(s2, kk, idx)
        v2_stack = jnp.broadcast_to(v2[kk - 1], (kk, tm))
        for i in range(kk - 1):
            v2_stack = jnp.where(sidx == float(i), v2[i], v2_stack)
        c2 = jnp.concatenate([v2_stack] * kk, axis=0)
        c1 = jnp.concatenate([jnp.broadcast_to(v, (kk, tm)) for v in v1], axis=0)
        vc, taken = _extract_top(c1 + c2, kk, cidx)
        z = jnp.ones((1, tm), F32)
        for i in range(1, kk):
            z = z + jnp.exp(vc[i] - vc[0])
        taken2 = jnp.where(taken, c2, jnp.inf)
        th = jnp.full((N_KEYS, tm), 2.0, F32)
        for a in range(kk):
            lo = jnp.min(taken2[a * kk:(a + 1) * kk], axis=0, keepdims=True)
            the = jnp.where(lo == jnp.inf, 2.0, jnp.exp(lo - v2[0]))
            th = jnp.where(in1 & (s1 == v1[a]), the, th)
        th_out[h] = th
        f1_out[h] = jnp.where(in1, jnp.exp(s1 - v1[0]), 0.0) * (1.0 / z)
        e2_out[h] = jnp.where(in2, jnp.exp(s2 - v2[0]), 0.0)


def _peer_route(x, g, wq_b, sk_b, tm):
    n = x.shape[0]
    keyed = pl.BlockSpec((PEER_HEADS, N_KEYS, tm), lambda i: (0, 0, i))
    return pl.pallas_call(
        _peer_route_body,
        grid=(n // tm,),
        in_specs=[_row_spec(tm, D_MODEL), _const_spec((1, D_MODEL)), _const_spec((D_MODEL, PEER_HEADS * PEER_DK)),
                  _const_spec((2 * PEER_HEADS, N_KEYS, PEER_DK // 2))],
        out_specs=[_row_spec(tm, D_MODEL), keyed, keyed, keyed],
        out_shape=[jax.ShapeDtypeStruct((n, D_MODEL), BF16)]
                  + [jax.ShapeDtypeStruct((PEER_HEADS, N_KEYS, n), F32)] * 3,
        compiler_params=_cparams(("arbitrary",)),
        name="peer_route",
    )(x, g.reshape(1, -1), wq_b, sk_b)


def _gelu_tanh(x):
    return 0.5 * x * (1.0 + jnp.tanh(0.7978845608028654 * (x + 0.044715 * (x * x * x))))


def _peer_expert_body(x_ref, h_ref, e2_ref, th_ref, f1_ref, u_ref, vt_ref, o_ref, acc_ref, w_ref):
    ec = pl.program_id(1)

    @pl.when(ec == 0)
    def _():
        acc_ref[...] = jnp.zeros(acc_ref.shape, F32)

    hid = _dot_nt(u_ref[...], h_ref[...])
    for r in range(PEER_ROWS):
        rows = slice(r * N_KEYS, (r + 1) * N_KEYS)
        gate = jnp.zeros((N_KEYS, h_ref.shape[0]), F32)
        for h in range(PEER_HEADS):
            e2 = e2_ref[h]
            gate = gate + jnp.where(e2 >= th_ref[h, 0, r:r + 1, :], e2, 0.0) * f1_ref[h, 0, r:r + 1, :]
        w_ref[rows, :] = (gate * _gelu_tanh(hid[rows])).astype(BF16)
    acc_ref[...] += _dot(vt_ref[...], w_ref[...])

    @pl.when(ec == pl.num_programs(1) - 1)
    def _():
        o_ref[...] = x_ref[...] + acc_ref[...].T


def _peer_expert(x, hb, e2, th, f1, u_b, vt_b, tm):
    n = x.shape[0]
    te = PEER_ROWS * N_KEYS
    ne = u_b.shape[0] // te
    chunked = lambda a: a.reshape(PEER_HEADS, ne, PEER_ROWS, n)
    tok = lambda c: pl.BlockSpec((tm, c), lambda i, e: (i, 0))
    row = pl.BlockSpec((PEER_HEADS, 1, PEER_ROWS, tm), lambda i, e: (0, e, 0, i))
    return pl.pallas_call(
        _peer_expert_body,
        grid=(n // tm, ne),
        in_specs=[tok(D_MODEL), tok(D_MODEL), pl.BlockSpec((PEER_HEADS, N_KEYS, tm), lambda i, e: (0, 0, i)),
                  row, row, pl.BlockSpec((te, D_MODEL), lambda i, e: (e, 0)),
                  pl.BlockSpec((D_MODEL, te), lambda i, e: (0, e))],
        out_specs=tok(D_MODEL),
        out_shape=jax.ShapeDtypeStruct((n, D_MODEL), F32),
        scratch_shapes=[pltpu.VMEM((D_MODEL, tm), F32), pltpu.VMEM((te, tm), BF16)],
        compiler_params=_cparams(("arbitrary", "arbitrary")),
        name="peer_expert",
    )(x, hb, e2, chunked(th), chunked(f1), u_b, vt_b)


def _final_norm_body(x_ref, g_ref, o_ref):
    o_ref[...] = _rmsnorm(x_ref[...], g_ref[...])


def _final_norm(x, g, tm):
    n = x.shape[0]
    return pl.pallas_call(
        _final_norm_body,
        grid=(n // tm,),
        in_specs=[_row_spec(tm, D_MODEL), _const_spec((1, D_MODEL))],
        out_specs=_row_spec(tm, D_MODEL),
        out_shape=jax.ShapeDtypeStruct((n, D_MODEL), F32),
        compiler_params=_cparams(("arbitrary",)),
        name="final_norm",
    )(x, g.reshape(1, -1))


def _heads_first(a, b, t):
    h = a.shape[1] // HEAD_DIM
    return a.reshape(b, t, h, HEAD_DIM).transpose(0, 2, 1, 3).reshape(b * h, t, HEAD_DIM)


def _tokens_first(a, b, t):
    h = a.shape[0] // b
    return a.reshape(b, h, t, HEAD_DIM).transpose(0, 2, 1, 3).reshape(b * t, h * HEAD_DIM)


def _layer(x, b, t, conv_buf, shift_prev, wkv0, attend, lw, tm, conv_tile, wkv_chains, wkv_tile):
    p_conv, p_rwkv, k, v, qb, kb, vb = _inproj(x, lw["norm1_g"], lw["w_in"], tm)

    y_conv, conv_new = _conv_mixer(p_conv.reshape(b, t, CONV_COLS), conv_buf, lw["conv_w"], lw["conv_b"],
                                   lw["conv_ln_g"], lw["conv_ln_b"], conv_tile)

    p3 = p_rwkv.reshape(b, t, RWKV_COLS)
    prev = jnp.concatenate([shift_prev[:, None, :], p3[:, :-1]], axis=1).reshape(b * t, RWKV_COLS)
    r, w, kr, vr, kk, bb, g, bonus = _rwkv_pre(p_rwkv, prev, lw["rwkv_mu"], lw["rwkv_w0"], lw["rwkv_w_up"],
                                               lw["rwkv_a0"], lw["rwkv_a_up"], lw["rwkv_g_up"], lw["rwkv_k_k"],
                                               lw["rwkv_k_a"], lw["rwkv_r_k"], tm)
    hf = lambda a: _heads_first(a, b, t)
    s0 = wkv0.reshape(b * RWKV_HEADS, HEAD_DIM, HEAD_DIM)
    if t % WKV_BLOCK == 0:
        y_wkv, wkv_new = _wkv_chunked(hf(r), hf(w), hf(kr), hf(vr), hf(kk), hf(bb), s0, wkv_chains)
    else:
        y_wkv, wkv_new = _wkv(hf(r), hf(w), hf(kr), hf(vr), hf(kk), hf(bb), s0, wkv_chains, wkv_tile)
    y_wkv = _tokens_first(y_wkv, b, t)

    y_att = attend(qb, k, v, kb, vb)

    x = _outproj(x, y_conv.reshape(b * t, CONV_CH), y_wkv, bonus, g, y_att, lw["rwkv_ln_g"], lw["rwkv_ln_b"],
                 lw["w_out"], tm)
    hb, e2, th, f1 = _peer_route(x, lw["norm2_g"], lw["peer_w_query"], lw["peer_sub_keys"], min(tm, 256))
    x = _peer_expert(x, hb, e2, th, f1, lw["peer_u"], lw["peer_vt"], tm)
    states = (k.reshape(b, t, ATT_HEADS, HEAD_DIM), v.reshape(b, t, ATT_HEADS, HEAD_DIM), conv_new,
              p3[:, -1], wkv_new.reshape(b, RWKV_HEADS, HEAD_DIM, HEAD_DIM))
    return x, states


def kernel(x_prompt, x_sample, cache_k, cache_v, state_conv, state_shift, state_wkv, page_table, norm1_g, w_in, conv_w, conv_b, conv_ln_g, conv_ln_b, rwkv_mu, rwkv_w0, rwkv_w_up, rwkv_a0, rwkv_a_up, rwkv_g_up, rwkv_k_k, rwkv_k_a, rwkv_r_k, rwkv_ln_g, rwkv_ln_b, att_bias, w_out, norm2_g, peer_w_query, peer_sub_keys, peer_u, peer_v, final_g):
    depth = w_in.shape[0]
    bp, tp, _ = x_prompt.shape
    bs, ts, _ = x_sample.shape
    n_pool = cache_k.shape[1]
    ck = cache_k.reshape(depth, n_pool, PAGE_SIZE, ATT_DIM)
    cv = cache_v.reshape(depth, n_pool, PAGE_SIZE, ATT_DIM)
    xp = x_prompt.reshape(bp * tp, D_MODEL)
    xs = x_sample.reshape(bs * ts, D_MODEL)
    st_p, st_s = [], []
    for li in range(depth):
        lw = dict(
            norm1_g=norm1_g[li], w_in=w_in[li].astype(BF16), conv_w=conv_w[li], conv_b=conv_b[li],
            conv_ln_g=conv_ln_g[li], conv_ln_b=conv_ln_b[li], rwkv_mu=rwkv_mu[li], rwkv_w0=rwkv_w0[li],
            rwkv_w_up=rwkv_w_up[li], rwkv_a0=rwkv_a0[li], rwkv_a_up=rwkv_a_up[li], rwkv_g_up=rwkv_g_up[li],
            rwkv_k_k=rwkv_k_k[li], rwkv_k_a=rwkv_k_a[li], rwkv_r_k=rwkv_r_k[li].reshape(-1),
            rwkv_ln_g=rwkv_ln_g[li], rwkv_ln_b=rwkv_ln_b[li], w_out=w_out[li].astype(BF16), norm2_g=norm2_g[li],
            peer_w_query=peer_w_query[li].astype(BF16),
            peer_sub_keys=peer_sub_keys[li].astype(BF16).reshape(2 * PEER_HEADS, N_KEYS, PEER_DK // 2),
            peer_u=peer_u[li].astype(BF16), peer_vt=peer_v[li].astype(BF16).T)
        bias = att_bias[li]

        def attend_prompt(qb, k, v, kb, vb):
            hf = lambda a: a.reshape(bp, tp, ATT_HEADS, HEAD_DIM).transpose(0, 2, 1, 3)
            o = _attn_prompt(hf(qb), hf(kb), hf(vb), bias)
            return o.transpose(0, 2, 1, 3).reshape(bp * tp, ATT_DIM)

        def attend_sample(qb, k, v, kb, vb):
            o = _attn_decode(li, qb.reshape(bs, ts, ATT_DIM), k.reshape(bs, ts, ATT_DIM),
                             v.reshape(bs, ts, ATT_DIM), ck, cv, page_table, bias)
            return o.reshape(bs * ts, ATT_DIM)

        xp, new_p = _layer(xp, bp, tp, jnp.zeros((bp, CONV_WIDTH - 1, CONV_CH), F32),
                           jnp.zeros((bp, RWKV_COLS), F32), jnp.zeros((bp, RWKV_HEADS, HEAD_DIM, HEAD_DIM), F32),
                           attend_prompt, lw, tm=512, conv_tile=512, wkv_chains=4, wkv_tile=256)
        xs, new_s = _layer(xs, bs, ts, state_conv[li], state_shift[li], state_wkv[li],
                           attend_sample, lw, tm=bs * ts, conv_tile=ts, wkv_chains=bp * RWKV_HEADS, wkv_tile=ts)
        st_p.append(new_p)
        st_s.append(new_s)
    y_prompt = _final_norm(xp, final_g, 512).reshape(bp, tp, D_MODEL)
    y_sample = _final_norm(xs, final_g, bs * ts).reshape(bs, ts, D_MODEL)
    stack = lambda st, j: jnp.stack([s[j] for s in st], axis=0)
    return (y_prompt, y_sample) + tuple(stack(st_p, j) for j in range(5)) + tuple(stack(st_s, j) for j in range(5))
```

```python
import functools

import jax
import jax.numpy as jnp
from jax import lax
from jax.experimental import pallas as pl
from jax.experimental.pallas import tpu as pltpu

F32 = jnp.float32
BF16 = jnp.bfloat16

D_MODEL = 1024
HEAD_DIM = 64
CONV_CH = 256
CONV_WIDTH = 31
CONV_HALO = 32
RWKV_HEADS = 6
RWKV_DIM = 384
ATT_HEADS = 6
ATT_DIM = 384
DECAY_LORA = 64
AAA_LORA = 64
GATE_LORA = 128
CONV_COLS = 512
RWKV_COLS = 1408
IN_COLS = 3072
PAGE_SIZE = 128
PEER_HEADS = 8
PEER_DK = 256
N_KEYS = 128
PEER_TOPK = 16
RMS_EPS = 1e-5
LN_EPS = 1e-5
GN_EPS = 64e-5

ATT_TILE = 256
PAGES_PER_STEP = 8
Q_PAD = 8
PEER_ROWS = 8
WKV_CHUNK_LOG2 = 6
WKV_CHUNK = 1 << WKV_CHUNK_LOG2
WKV_BLOCK = 128
VMEM_LIMIT = 56 * 1024 * 1024


def _cparams(sem):
    return pltpu.CompilerParams(dimension_semantics=sem, vmem_limit_bytes=VMEM_LIMIT)


def _sigmoid(x):
    return 1.0 / (1.0 + jnp.exp(-x))


def _softplus(x):
    return jnp.maximum(x, 0.0) + jnp.log(1.0 + jnp.exp(-jnp.abs(x)))


def _dot(a, b):
    return jnp.dot(a, b, preferred_element_type=F32)


def _dot_nt(a, b):
    return lax.dot_general(a, b, (((1,), (1,)), ((), ())), preferred_element_type=F32)


def _split2(x):
    hi = x.astype(BF16)
    lo = (x - hi.astype(F32)).astype(BF16)
    return hi, lo


def _dot3(a, b):
    ah, al = _split2(a)
    bh, bl = _split2(b)
    return _dot(ah, bh) + (_dot(ah, bl) + _dot(al, bh))


def _dot_exact_rhs(x, m):
    hi, lo = _split2(x)
    return _dot(hi, m) + _dot(lo, m)


def _rmsnorm(x, g):
    return x * lax.rsqrt(jnp.mean(x * x, axis=-1, keepdims=True) + RMS_EPS) * g


def _head_blockdiag(n):
    r = lax.broadcasted_iota(jnp.int32, (n, n), 0) // HEAD_DIM
    c = lax.broadcasted_iota(jnp.int32, (n, n), 1) // HEAD_DIM
    return (r == c).astype(BF16)


def _suffix_matrix(n):
    r = lax.broadcasted_iota(jnp.int32, (n, n), 0)
    c = lax.broadcasted_iota(jnp.int32, (n, n), 1)
    return (r > c).astype(BF16)


def _row_spec(tm, cols):
    return pl.BlockSpec((tm, cols), lambda i: (i, 0))


def _const_spec(shape):
    return pl.BlockSpec(shape, lambda *_: (0,) * len(shape))


def _inproj_body(x_ref, g_ref, w_ref, conv_ref, rwkv_ref, k_ref, v_ref, qb_ref, kb_ref, vb_ref):
    h = _rmsnorm(x_ref[...], g_ref[...]).astype(BF16)
    o0, o1, o2, o3 = CONV_COLS, CONV_COLS + RWKV_COLS, CONV_COLS + RWKV_COLS + ATT_DIM, IN_COLS - ATT_DIM
    conv_ref[...] = _dot(h, w_ref[:, 0:o0])
    rwkv_ref[...] = _dot(h, w_ref[:, o0:o1])
    q = _dot(h, w_ref[:, o1:o2])
    k = _dot(h, w_ref[:, o2:o3])
    v = _dot(h, w_ref[:, o3:IN_COLS])
    k_ref[...] = k
    v_ref[...] = v
    qb_ref[...] = (q * (HEAD_DIM ** -0.5)).astype(BF16)
    kb_ref[...] = k.astype(BF16)
    vb_ref[...] = v.astype(BF16)


def _inproj(x, g, w_b, tm):
    n = x.shape[0]
    widths = (CONV_COLS, RWKV_COLS, ATT_DIM, ATT_DIM, ATT_DIM, ATT_DIM, ATT_DIM)
    dtypes = (F32, F32, F32, F32, BF16, BF16, BF16)
    return pl.pallas_call(
        _inproj_body,
        grid=(n // tm,),
        in_specs=[_row_spec(tm, D_MODEL), _const_spec((1, D_MODEL)), _const_spec((D_MODEL, IN_COLS))],
        out_specs=[_row_spec(tm, c) for c in widths],
        out_shape=[jax.ShapeDtypeStruct((n, c), d) for c, d in zip(widths, dtypes)],
        compiler_params=_cparams(("arbitrary",)),
        name="inproj",
    )(x, g.reshape(1, -1), w_b)


def _conv_body(p_ref, buf_ref, w_ref, cb_ref, lg_ref, lb_ref, y_ref, nb_ref, ext_ref, *, tt):
    keep = CONV_WIDTH - 1
    lead = CONV_HALO - keep

    @pl.when(pl.program_id(1) == 0)
    def _():
        ext_ref[0:CONV_HALO, :] = jnp.zeros((CONV_HALO, CONV_CH), F32)
        ext_ref[lead:CONV_HALO, :] = buf_ref[0]

    p = p_ref[0]
    ext_ref[CONV_HALO:CONV_HALO + tt, :] = p[:, :CONV_CH] * _sigmoid(p[:, CONV_CH:])
    acc = jnp.zeros((tt, CONV_CH), F32)
    for j in range(CONV_WIDTH):
        acc = acc + w_ref[j:j + 1, :] * ext_ref[lead + j:lead + j + tt, :]
    y = acc + cb_ref[...]
    mu = jnp.mean(y, axis=-1, keepdims=True)
    d = y - mu
    var = jnp.mean(d * d, axis=-1, keepdims=True)
    y = d * lax.rsqrt(var + LN_EPS) * lg_ref[...] + lb_ref[...]
    y_ref[0] = (y * _sigmoid(y)).astype(y_ref.dtype)
    nb_ref[0] = ext_ref[tt + lead:tt + CONV_HALO, :]
    tail = ext_ref[tt:tt + CONV_HALO, :]
    ext_ref[0:CONV_HALO, :] = tail


def _conv_mixer(p, buf, w, cb, lg, lb, tt):
    b, t, _ = p.shape
    keep = CONV_WIDTH - 1
    return pl.pallas_call(
        functools.partial(_conv_body, tt=tt),
        grid=(b, t // tt),
        in_specs=[pl.BlockSpec((1, tt, CONV_COLS), lambda i, j: (i, j, 0)),
                  pl.BlockSpec((1, keep, CONV_CH), lambda i, j: (i, 0, 0)),
                  _const_spec((CONV_WIDTH, CONV_CH)), _const_spec((1, CONV_CH)),
                  _const_spec((1, CONV_CH)), _const_spec((1, CONV_CH))],
        out_specs=[pl.BlockSpec((1, tt, CONV_CH), lambda i, j: (i, j, 0)),
                   pl.BlockSpec((1, keep, CONV_CH), lambda i, j: (i, 0, 0))],
        out_shape=[jax.ShapeDtypeStruct((b, t, CONV_CH), BF16),
                   jax.ShapeDtypeStruct((b, keep, CONV_CH), F32)],
        scratch_shapes=[pltpu.VMEM((tt + CONV_HALO, CONV_CH), F32)],
        compiler_params=_cparams(("arbitrary", "arbitrary")),
        name="conv_mixer",
    )(p, buf, w, cb.reshape(1, -1), lg.reshape(1, -1), lb.reshape(1, -1))


def _rwkv_pre_body(p_ref, prev_ref, mu_ref, w0_ref, wup_ref, a0_ref, aup_ref, gup_ref, kk_ref, ka_ref, rk_ref,
                   r_out, w_out, k_out, v_out, kk_out, b_out, g_out, bonus_out):
    p = p_ref[...]
    pm = p + (prev_ref[...] - p) * mu_ref[...]
    d = RWKV_DIM
    r = pm[:, 0:d]
    k = pm[:, d:2 * d]
    v = pm[:, 2 * d:3 * d]
    xw = pm[:, 3 * d:3 * d + DECAY_LORA]
    xa = pm[:, 3 * d + DECAY_LORA:3 * d + DECAY_LORA + AAA_LORA]
    xg = pm[:, 3 * d + DECAY_LORA + AAA_LORA:RWKV_COLS]
    bd = _head_blockdiag(d)
    w_log = -_softplus(-(w0_ref[...] + _dot3(jnp.tanh(xw), wup_ref[...]))) - 0.5
    log_decay = -jnp.exp(w_log)
    a = _sigmoid(a0_ref[...] + _dot3(xa, aup_ref[...]))
    g = _dot3(_sigmoid(xg), gup_ref[...])
    kk = k * kk_ref[...]
    k = k * (1.0 + (a - 1.0) * ka_ref[...])
    kk = kk / jnp.maximum(jnp.sqrt(_dot_exact_rhs(kk * kk, bd)), 1e-12)
    r_out[...] = r
    w_out[...] = log_decay
    k_out[...] = k
    v_out[...] = v
    kk_out[...] = kk
    b_out[...] = kk * a
    g_out[...] = g
    bonus_out[...] = _dot_exact_rhs(r * k * rk_ref[...], bd) * v


def _rwkv_pre(p, prev, mu, w0, w_up, a0, a_up, g_up, k_k, k_a, r_k, tm):
    n = p.shape[0]
    vec = lambda a: a.reshape(1, -1)
    d = RWKV_DIM
    return pl.pallas_call(
        _rwkv_pre_body,
        grid=(n // tm,),
        in_specs=[_row_spec(tm, RWKV_COLS), _row_spec(tm, RWKV_COLS), _const_spec((1, RWKV_COLS)),
                  _const_spec((1, d)), _const_spec((DECAY_LORA, d)), _const_spec((1, d)),
                  _const_spec((AAA_LORA, d)), _const_spec((GATE_LORA, d)),
                  _const_spec((1, d)), _const_spec((1, d)), _const_spec((1, d))],
        out_specs=[_row_spec(tm, d)] * 8,
        out_shape=[jax.ShapeDtypeStruct((n, d), F32)] * 8,
        compiler_params=_cparams(("arbitrary",)),
        name="rwkv_pre",
    )(p, prev, vec(mu), vec(w0), w_up, vec(a0), a_up, g_up, vec(k_k), vec(k_a), vec(r_k))


def _wkv_body(r_ref, w_ref, k_ref, v_ref, kk_ref, b_ref, s0_ref, y_ref, sT_ref, s_scr, *, nc, tc):
    ti = pl.program_id(1)

    @pl.when(ti == 0)
    def _():
        s_scr[...] = s0_ref[...]

    eye = (lax.broadcasted_iota(jnp.int32, (HEAD_DIM, HEAD_DIM), 0)
           == lax.broadcasted_iota(jnp.int32, (HEAD_DIM, HEAD_DIM), 1))

    def step(t, carry):
        for c in range(nc):
            row = lambda ref: ref[c, pl.ds(t, 1), :]
            s = s_scr[c]
            sa = jnp.sum(s * row(kk_ref), axis=1, keepdims=True)
            vcol = jnp.sum(jnp.where(eye, row(v_ref), 0.0), axis=1, keepdims=True)
            s = s * jnp.exp(row(w_ref)) - sa * row(b_ref) + vcol * row(k_ref)
            s_scr[c] = s
            ycol = jnp.sum(s * row(r_ref), axis=1, keepdims=True)
            y_ref[c, pl.ds(t, 1), :] = jnp.sum(jnp.where(eye, ycol, 0.0), axis=0, keepdims=True)
        return carry

    lax.fori_loop(0, tc, step, 0)

    @pl.when(ti == pl.num_programs(1) - 1)
    def _():
        sT_ref[...] = s_scr[...]


def _wkv(r, w, k, v, kk, b, s0, nc, tc):
    n, t, _ = r.shape
    seq = pl.BlockSpec((nc, tc, HEAD_DIM), lambda i, j: (i, j, 0))
    st = pl.BlockSpec((nc, HEAD_DIM, HEAD_DIM), lambda i, j: (i, 0, 0))
    return pl.pallas_call(
        functools.partial(_wkv_body, nc=nc, tc=tc),
        grid=(n // nc, t // tc),
        in_specs=[seq] * 6 + [st],
        out_specs=[seq, st],
        out_shape=[jax.ShapeDtypeStruct((n, t, HEAD_DIM), F32),
                   jax.ShapeDtypeStruct((n, HEAD_DIM, HEAD_DIM), F32)],
        scratch_shapes=[pltpu.VMEM((nc, HEAD_DIM, HEAD_DIM), F32)],
        compiler_params=_cparams(("arbitrary", "arbitrary")),
        name="wkv",
    )(r, w, k, v, kk, b, s0)


def _mm(a, b):
    return _dot3(a, b)


def _mm_nt(a, b):
    ah, al = _split2(a)
    bh, bl = _split2(b)
    return _dot_nt(ah, bh) + (_dot_nt(ah, bl) + _dot_nt(al, bh))


def _split3(x):
    p1 = x.astype(BF16)
    r1 = x - p1.astype(F32)
    p2 = r1.astype(BF16)
    p3 = (r1 - p2.astype(F32)).astype(BF16)
    return p1, p2, p3


def _wkv_chunk_body(r_ref, lw_ref, k_ref, v_ref, kk_ref, b_ref, lwt_ref, kkt_ref, vt_ref, s0_ref,
                    y_ref, sT_ref, s_scr, *, nc):
    ti = pl.program_id(1)
    c_len = WKV_CHUNK

    @pl.when(ti == 0)
    def _():
        s_scr[...] = s0_ref[...]

    row = lax.broadcasted_iota(jnp.int32, (c_len, c_len), 0)
    col = lax.broadcasted_iota(jnp.int32, (c_len, c_len), 1)
    strict = row > col
    incl = row >= col
    lower = incl.astype(BF16)
    upper = (row <= col).astype(BF16)
    eye = (row == col).astype(F32)

    n_sub = WKV_BLOCK // c_len
    units = [(j, c) for j in range(n_sub) for c in range(nc)]
    each = lambda f, *lists: [f(*args) for args in zip(*lists)]
    ts = lambda j: slice(j * c_len, (j + 1) * c_len)
    rows = lambda ref: [ref[c, ts(j), :] for j, c in units]
    cols = lambda ref: [ref[c, :, ts(j)] for j, c in units]
    r, lw, k, v, kk, b = (rows(ref) for ref in (r_ref, lw_ref, k_ref, v_ref, kk_ref, b_ref))
    lwt, kkt, vt = (cols(ref) for ref in (lwt_ref, kkt_ref, vt_ref))
    cum = each(lambda x: sum(_dot(lower, piece) for piece in _split3(x)), lw)
    cumt = each(lambda x: sum(_dot(piece, upper) for piece in _split3(x)), lwt)
    w_incl = each(jnp.exp, cum)
    w_last = each(lambda w: w[c_len - 1:c_len, :], w_incl)
    inv = each(lambda x: jnp.exp(-x), cum)
    qt = each(lambda x, cs, l: x * jnp.exp(cs - l), kk, cum, lw)
    qtt = each(lambda x, cs, l: x * jnp.exp(cs - l), kkt, cumt, lwt)
    rt = each(jnp.multiply, r, w_incl)
    kt = each(jnp.multiply, k, inv)
    bt = each(jnp.multiply, b, inv)
    a_qb = each(lambda x, y: jnp.where(strict, _mm_nt(x, y), 0.0), qt, bt)
    a_qk = each(lambda x, y: jnp.where(strict, _mm_nt(x, y), 0.0), qt, kt)
    a_rk = each(lambda x, y: jnp.where(incl, _mm_nt(x, y), 0.0), rt, kt)
    a_rb = each(lambda x, y: jnp.where(incl, _mm_nt(x, y), 0.0), rt, bt)
    tinv = each(lambda n: eye - n, a_qb)
    power = a_qb
    for _ in range(WKV_CHUNK_LOG2 - 1):
        power = each(_mm, power, power)
        tinv = each(lambda t, p: t + _mm(t, p), tinv, power)
    tq = each(_mm, tinv, qt)
    av = each(_mm, a_qk, v)
    av_t = each(_mm_nt, vt, a_qk)
    tq_t = each(_mm_nt, qtt, tinv)
    tav = each(_mm, tinv, av)
    tav_t = each(_mm_nt, av_t, tinv)
    ark_v = each(_mm, a_rk, v)
    rq = each(lambda x, a, y: x - _mm(a, y), rt, a_rb, tq)
    yv = each(lambda x, a, y: x - _mm(a, y), ark_v, a_rb, tav)
    gain = each(lambda x, y, w: _mm(x, y * w), vt, kt, w_last)
    bw = each(jnp.multiply, bt, w_last)
    for j in range(n_sub):
        ids = [j * nc + c for c in range(nc)]
        s = [s_scr[c] for c in range(nc)]
        sa_t = [_mm(s[c], tq_t[i]) + tav_t[i] for c, i in enumerate(ids)]
        ys = [_mm_nt(rq[i], s[c]) + yv[i] for c, i in enumerate(ids)]
        new = [s[c] * w_last[i] + gain[i] - _mm(sa_t[c], bw[i]) for c, i in enumerate(ids)]
        for c, i in enumerate(ids):
            y_ref[c, ts(j), :] = ys[c]
            s_scr[c] = new[c]

    @pl.when(ti == pl.num_programs(1) - 1)
    def _():
        sT_ref[...] = s_scr[...]


def _wkv_chunked(r, lw, k, v, kk, b, s0, nc):
    n, t, _ = r.shape
    tr = lambda a: a.transpose(0, 2, 1)
    seq = pl.BlockSpec((nc, WKV_BLOCK, HEAD_DIM), lambda i, j: (i, j, 0))
    seq_t = pl.BlockSpec((nc, HEAD_DIM, WKV_BLOCK), lambda i, j: (i, 0, j))
    st = pl.BlockSpec((nc, HEAD_DIM, HEAD_DIM), lambda i, j: (i, 0, 0))
    return pl.pallas_call(
        functools.partial(_wkv_chunk_body, nc=nc),
        grid=(n // nc, t // WKV_BLOCK),
        in_specs=[seq] * 6 + [seq_t] * 3 + [st],
        out_specs=[seq, st],
        out_shape=[jax.ShapeDtypeStruct((n, t, HEAD_DIM), F32),
                   jax.ShapeDtypeStruct((n, HEAD_DIM, HEAD_DIM), F32)],
        scratch_shapes=[pltpu.VMEM((nc, HEAD_DIM, HEAD_DIM), F32)],
        compiler_params=_cparams(("arbitrary", "arbitrary")),
        name="wkv_chunked",
    )(r, lw, k, v, kk, b, tr(lw), tr(kk), tr(v), s0)


def _sb_block(q, kb, vb, bias, rest, umat, mask):
    z = _dot_nt(q, kb) + bias
    sp = _softplus(z)
    log_rest = -sp
    if mask is not None:
        log_rest = jnp.where(mask, log_rest, 0.0)
    hi, lo = _split2(log_rest)
    after = _dot(hi, umat) + _dot(lo, umat)
    a = jnp.exp((z - sp) + after + rest)
    if mask is not None:
        a = jnp.where(mask, a, 0.0)
    out = _dot(a.astype(BF16), vb)
    return out, after[:, 0:1] + log_rest[:, 0:1]


def _attn_prompt_body(q_ref, k_ref, v_ref, bias_ref, o_ref, *, tile):
    qi = pl.program_id(1)
    heads = range(ATT_HEADS)
    hs = lambda a, h: a[:, h * HEAD_DIM:(h + 1) * HEAD_DIM]
    q_all = q_ref[0]
    qs = [hs(q_all, h) for h in heads]
    biases = [bias_ref[h] for h in heads]
    umat = _suffix_matrix(tile)
    causal = (lax.broadcasted_iota(jnp.int32, (tile, tile), 1)
              < lax.broadcasted_iota(jnp.int32, (tile, tile), 0))

    def visit(j, accs, rests, mask):
        start = pl.multiple_of(j * tile, tile)
        kb = k_ref[0, pl.ds(start, tile), :]
        vb = v_ref[0, pl.ds(start, tile), :]
        zs = [_dot_nt(qs[h], hs(kb, h)) + biases[h] for h in heads]
        sps = [_softplus(z) for z in zs]
        lrs = [-sp if mask is None else jnp.where(mask, -sp, 0.0) for sp in sps]
        parts = [_split2(lr) for lr in lrs]
        afters = [_dot(hi, umat) + _dot(lo, umat) for hi, lo in parts]
        probs = [jnp.exp((zs[h] - sps[h]) + afters[h] + rests[h]) for h in heads]
        if mask is not None:
            probs = [jnp.where(mask, a, 0.0) for a in probs]
        outs = [_dot(probs[h].astype(BF16), hs(vb, h)) for h in heads]
        new_accs = tuple(accs[h] + outs[h] for h in heads)
        new_rests = tuple(rests[h] + afters[h][:, 0:1] + lrs[h][:, 0:1] for h in heads)
        return new_accs, new_rests

    accs = tuple(jnp.zeros((tile, HEAD_DIM), F32) for _ in heads)
    rests = tuple(jnp.zeros((tile, 1), F32) for _ in heads)
    accs, rests = visit(qi, accs, rests, causal)
    accs, rests = lax.fori_loop(0, qi, lambda jj, c: visit(qi - 1 - jj, c[0], c[1], None), (accs, rests))
    r = lax.broadcasted_iota(jnp.int32, (HEAD_DIM, ATT_DIM), 0)
    c = lax.broadcasted_iota(jnp.int32, (HEAD_DIM, ATT_DIM), 1)
    out = jnp.zeros((tile, ATT_DIM), F32)
    for h in heads:
        out = out + _dot(accs[h].astype(BF16), (c == r + h * HEAD_DIM).astype(BF16))
    o_ref[0] = out.astype(o_ref.dtype)


def _attn_prompt(qb, kb, vb, bias):
    b, t, d = qb.shape
    tile = ATT_TILE
    bias_b = jnp.broadcast_to(bias.astype(F32)[:, None, None], (ATT_HEADS, 1, tile))
    qspec = pl.BlockSpec((1, tile, d), lambda i, l: (i, l, 0))
    kvspec = pl.BlockSpec((1, t, d), lambda i, l: (i, 0, 0))
    return pl.pallas_call(
        functools.partial(_attn_prompt_body, tile=tile),
        grid=(b, t // tile),
        in_specs=[qspec, kvspec, kvspec, _const_spec((ATT_HEADS, 1, tile))],
        out_specs=qspec,
        out_shape=jax.ShapeDtypeStruct((b, t, d), BF16),
        compiler_params=_cparams(("arbitrary", "arbitrary")),
        name="attn_prompt",
    )(qb, kb, vb, bias_b)


def _attn_decode_body(pt_ref, q_ref, bias_ref, kn_ref, vn_ref, *refs, npg, nq):
    k_refs = refs[:npg]
    v_refs = refs[npg:2 * npg]
    o_ref, rest_ref, acc_ref = refs[2 * npg:]
    jb = pl.program_id(1)
    rows = ATT_HEADS * Q_PAD
    q = q_ref[0]
    bias = bias_ref[...]
    umat = _suffix_matrix(PAGE_SIZE)

    def visit(kf, vf, mask):
        out, tot = _sb_block(q, kf.astype(BF16), vf.astype(BF16), bias, rest_ref[...], umat, mask)
        acc_ref[...] += out
        rest_ref[...] += jnp.broadcast_to(tot, (rows, PAGE_SIZE))

    @pl.when(jb == 0)
    def _():
        rest_ref[...] = jnp.zeros((rows, PAGE_SIZE), F32)
        acc_ref[...] = jnp.zeros((rows, ATT_DIM), F32)
        kidx = lax.broadcasted_iota(jnp.int32, (rows, PAGE_SIZE), 1)
        qidx = lax.broadcasted_iota(jnp.int32, (rows, PAGE_SIZE), 0) % Q_PAD
        visit(kn_ref[0], vn_ref[0], (kidx < qidx) & (kidx < nq))

    for s in range(npg):
        visit(k_refs[s][0, 0], v_refs[s][0, 0], None)

    @pl.when(jb == pl.num_programs(1) - 1)
    def _():
        head_of_row = lax.broadcasted_iota(jnp.int32, (rows, ATT_DIM), 0) // Q_PAD
        head_of_col = lax.broadcasted_iota(jnp.int32, (rows, ATT_DIM), 1) // HEAD_DIM
        kept = jnp.where(head_of_row == head_of_col, acc_ref[...], 0.0)
        out = kept[0:Q_PAD]
        for h in range(1, ATT_HEADS):
            out = out + kept[h * Q_PAD:(h + 1) * Q_PAD]
        o_ref[0] = out.astype(o_ref.dtype)


def _attn_decode(li, qb, k_new, v_new, cache_k, cache_v, page_table, bias):
    b, t, _ = qb.shape
    n_pages = page_table.shape[1]
    npg = PAGES_PER_STEP
    rows = ATT_HEADS * Q_PAD
    qpad = jnp.pad(qb, ((0, 0), (0, Q_PAD - t), (0, 0)))
    head_of_col = jnp.arange(ATT_DIM) // HEAD_DIM
    qbd = jnp.where(head_of_col[None, None, None, :] == jnp.arange(ATT_HEADS)[None, :, None, None],
                    qpad[:, None], jnp.zeros((), BF16)).reshape(b, rows, ATT_DIM)
    bias_b = jnp.broadcast_to(jnp.repeat(bias.astype(F32), Q_PAD)[:, None], (rows, PAGE_SIZE))
    kn = jnp.pad(k_new, ((0, 0), (0, PAGE_SIZE - t), (0, 0)))
    vn = jnp.pad(v_new, ((0, 0), (0, PAGE_SIZE - t), (0, 0)))

    def page_spec(s):
        return pl.BlockSpec((1, 1, PAGE_SIZE, ATT_DIM),
                            lambda i, j, pt: (li, pt[i * n_pages + n_pages - 1 - (j * npg + s)], 0, 0))

    per_b = lambda r, c: pl.BlockSpec((1, r, c), lambda i, j, pt: (i, 0, 0))
    grid_spec = pltpu.PrefetchScalarGridSpec(
        num_scalar_prefetch=1,
        grid=(b, n_pages // npg),
        in_specs=[per_b(rows, ATT_DIM), pl.BlockSpec((rows, PAGE_SIZE), lambda i, j, pt: (0, 0)),
                  per_b(PAGE_SIZE, ATT_DIM), per_b(PAGE_SIZE, ATT_DIM)]
                 + [page_spec(s) for s in range(npg)] * 2,
        out_specs=per_b(Q_PAD, ATT_DIM),
        scratch_shapes=[pltpu.VMEM((rows, PAGE_SIZE), F32), pltpu.VMEM((rows, ATT_DIM), F32)],
    )
    out = pl.pallas_call(
        functools.partial(_attn_decode_body, npg=npg, nq=t),
        grid_spec=grid_spec,
        out_shape=jax.ShapeDtypeStruct((b, Q_PAD, ATT_DIM), BF16),
        compiler_params=_cparams(("arbitrary", "arbitrary")),
        name="attn_decode",
    )(page_table.reshape(-1), qbd, bias_b, kn, vn, *([cache_k] * npg), *([cache_v] * npg))
    return out[:, :t]


def _outproj_body(x_ref, yc_ref, yw_ref, bonus_ref, g_ref, ya_ref, lng_ref, lnb_ref, w_ref, o_ref):
    bd = _head_blockdiag(RWKV_DIM)
    y = yw_ref[...]
    inv = 1.0 / HEAD_DIM
    d = y - _dot_exact_rhs(y, bd) * inv
    var = _dot_exact_rhs(d * d, bd) * inv
    yn = d * lax.rsqrt(var + GN_EPS) * lng_ref[...] + lnb_ref[...]
    yr = ((yn + bonus_ref[...]) * g_ref[...]).astype(BF16)
    c0, c1 = CONV_CH, CONV_CH + RWKV_DIM
    o_ref[...] = (x_ref[...] + _dot(yc_ref[...], w_ref[0:c0, :]) + _dot(yr, w_ref[c0:c1, :])
                  + _dot(ya_ref[...], w_ref[c1:D_MODEL, :]))


def _outproj(x, yc, yw, bonus, g, ya, ln_g, ln_b, w_b, tm):
    n = x.shape[0]
    return pl.pallas_call(
        _outproj_body,
        grid=(n // tm,),
        in_specs=[_row_spec(tm, D_MODEL), _row_spec(tm, CONV_CH), _row_spec(tm, RWKV_DIM), _row_spec(tm, RWKV_DIM),
                  _row_spec(tm, RWKV_DIM), _row_spec(tm, ATT_DIM), _const_spec((1, RWKV_DIM)),
                  _const_spec((1, RWKV_DIM)), _const_spec((D_MODEL, D_MODEL))],
        out_specs=_row_spec(tm, D_MODEL),
        out_shape=jax.ShapeDtypeStruct((n, D_MODEL), F32),
        compiler_params=_cparams(("arbitrary",)),
        name="outproj",
    )(x, yc, yw, bonus, g, ya, ln_g.reshape(1, -1), ln_b.reshape(1, -1), w_b)


def _extract_top(x, n, idx):
    width = x.shape[0]
    orig = x
    vals = []
    for _ in range(n):
        m = jnp.max(x, axis=0, keepdims=True)
        first = jnp.min(jnp.where(x == m, idx, float(width)), axis=0, keepdims=True)
        x = jnp.where(idx == first, -jnp.inf, x)
        vals.append(m)
    return vals, x != orig


def _peer_route_body(x_ref, g_ref, wq_ref, sk_ref, h_out, e2_out, th_out, f1_out):
    tm = x_ref.shape[0]
    kk = PEER_TOPK
    hb = _rmsnorm(x_ref[...], g_ref[...]).astype(BF16)
    h_out[...] = hb
    idx = lax.broadcasted_iota(jnp.int32, (N_KEYS, tm), 0).astype(F32)
    cidx = lax.broadcasted_iota(jnp.int32, (kk * kk, tm), 0).astype(F32)
    sidx = lax.broadcasted_iota(jnp.int32, (kk, tm), 0).astype(F32)
    half = PEER_DK // 2
    for h in range(PEER_HEADS):
        q = _dot(hb, wq_ref[:, h * PEER_DK:(h + 1) * PEER_DK])
        s1 = _dot_nt(sk_ref[2 * h], q[:, :half].astype(BF16))
        s2 = _dot_nt(sk_ref[2 * h + 1], q[:, half:].astype(BF16))
        v1, in1 = _extract_top(s1, kk, idx)
        v2, in2 = _extract_top(s2, kk, idx)
        v2_stack = jnp.broadcast_to(v2[kk - 1], (kk, tm))
        for i in range(kk - 1):
            v2_stack = jnp.where(sidx == float(i), v2[i], v2_stack)
        c2 = jnp.concatenate([v2_stack] * kk, axis=0)
        c1 = jnp.concatenate([jnp.broadcast_to(v, (kk, tm)) for v in v1], axis=0)
        vc, taken = _extract_top(c1 + c2, kk, cidx)
        z = jnp.ones((1, tm), F32)
        for i in range(1, kk):
            z = z + jnp.exp(vc[i] - vc[0])
        taken2 = jnp.where(taken, c2, jnp.inf)
        th = jnp.full((N_KEYS, tm), 2.0, F32)
        for a in range(kk):
            lo = jnp.min(taken2[a * kk:(a + 1) * kk], axis=0, keepdims=True)
            the = jnp.where(lo == jnp.inf, 2.0, jnp.exp(lo - v2[0]))
            th = jnp.where(in1 & (s1 == v1[a]), the, th)
        th_out[h] = th
        f1_out[h] = jnp.where(in1, jnp.exp(s1 - v1[0]), 0.0) * (1.0 / z)
        e2_out[h] = jnp.where(in2, jnp.exp(s2 - v2[0]), 0.0)


def _peer_route(x, g, wq_b, sk_b, tm):
    n = x.shape[0]
    keyed = pl.BlockSpec((PEER_HEADS, N_KEYS, tm), lambda i: (0, 0, i))
    return pl.pallas_call(
        _peer_route_body,
        grid=(n // tm,),
        in_specs=[_row_spec(tm, D_MODEL), _const_spec((1, D_MODEL)), _const_spec((D_MODEL, PEER_HEADS * PEER_DK)),
                  _const_spec((2 * PEER_HEADS, N_KEYS, PEER_DK // 2))],
        out_specs=[_row_spec(tm, D_MODEL), keyed, keyed, keyed],
        out_shape=[jax.ShapeDtypeStruct((n, D_MODEL), BF16)]
                  + [jax.ShapeDtypeStruct((PEER_HEADS, N_KEYS, n), F32)] * 3,
        compiler_params=_cparams(("arbitrary",)),
        name="peer_route",
    )(x, g.reshape(1, -1), wq_b, sk_b)


def _gelu_tanh(x):
    return 0.5 * x * (1.0 + jnp.tanh(0.7978845608028654 * (x + 0.044715 * (x * x * x))))


def _peer_expert_body(x_ref, h_ref, e2_ref, th_ref, f1_ref, u_ref, vt_ref, o_ref, acc_ref):
    ec = pl.program_id(1)

    @pl.when(ec == 0)
    def _():
        acc_ref[...] = jnp.zeros(acc_ref.shape, F32)

    tm = h_ref.shape[0]
    n_grp = 2 if tm % 512 == 0 else 1
    grp = tm // n_grp
    toks = [slice(g * grp, (g + 1) * grp) for g in range(n_grp)]
    hids = [_dot_nt(u_ref[...], h_ref[tk, :]) for tk in toks]
    for g, tk in enumerate(toks):
        weights = []
        for r in range(PEER_ROWS):
            gate = jnp.zeros((N_KEYS, grp), F32)
            for h in range(PEER_HEADS):
                e2 = e2_ref[h, :, tk]
                gate = gate + jnp.where(e2 >= th_ref[h, 0, r:r + 1, tk], e2, 0.0) * f1_ref[h, 0, r:r + 1, tk]
            weights.append((gate * _gelu_tanh(hids[g][r * N_KEYS:(r + 1) * N_KEYS])).astype(BF16))
        acc_ref[:, tk] += _dot(vt_ref[...], jnp.concatenate(weights, axis=0))

    @pl.when(ec == pl.num_programs(1) - 1)
    def _():
        o_ref[...] = x_ref[...] + acc_ref[...].T


def _peer_expert(x, hb, e2, th, f1, u_b, vt_b, tm):
    n = x.shape[0]
    te = PEER_ROWS * N_KEYS
    ne = u_b.shape[0] // te
    chunked = lambda a: a.reshape(PEER_HEADS, ne, PEER_ROWS, n)
    tok = lambda c: pl.BlockSpec((tm, c), lambda i, e: (i, 0))
    row = pl.BlockSpec((PEER_HEADS, 1, PEER_ROWS, tm), lambda i, e: (0, e, 0, i))
    return pl.pallas_call(
        _peer_expert_body,
        grid=(n // tm, ne),
        in_specs=[tok(D_MODEL), tok(D_MODEL), pl.BlockSpec((PEER_HEADS, N_KEYS, tm), lambda i, e: (0, 0, i)),
                  row, row, pl.BlockSpec((te, D_MODEL), lambda i, e: (e, 0)),
                  pl.BlockSpec((D_MODEL, te), lambda i, e: (0, e))],
        out_specs=tok(D_MODEL),
        out_shape=jax.ShapeDtypeStruct((n, D_MODEL), F32),
        scratch_shapes=[pltpu.VMEM((D_MODEL, tm), F32)],
        compiler_params=_cparams(("arbitrary", "arbitrary")),
        name="peer_expert",
    )(x, hb, e2, chunked(th), chunked(f1), u_b, vt_b)


def _final_norm_body(x_ref, g_ref, o_ref):
    o_ref[...] = _rmsnorm(x_ref[...], g_ref[...])


def _final_norm(x, g, tm):
    n = x.shape[0]
    return pl.pallas_call(
        _final_norm_body,
        grid=(n // tm,),
        in_specs=[_row_spec(tm, D_MODEL), _const_spec((1, D_MODEL))],
        out_specs=_row_spec(tm, D_MODEL),
        out_shape=jax.ShapeDtypeStruct((n, D_MODEL), F32),
        compiler_params=_cparams(("arbitrary",)),
        name="final_norm",
    )(x, g.reshape(1, -1))


def _heads_first(a, b, t):
    h = a.shape[1] // HEAD_DIM
    return a.reshape(b, t, h, HEAD_DIM).transpose(0, 2, 1, 3).reshape(b * h, t, HEAD_DIM)


def _tokens_first(a, b, t):
    h = a.shape[0] // b
    return a.reshape(b, h, t, HEAD_DIM).transpose(0, 2, 1, 3).reshape(b * t, h * HEAD_DIM)


def _layer(x, b, t, conv_buf, shift_prev, wkv0, attend, lw, tm, conv_tile, wkv_chains, wkv_tile):
    p_conv, p_rwkv, k, v, qb, kb, vb = _inproj(x, lw["norm1_g"], lw["w_in"], tm)

    y_conv, conv_new = _conv_mixer(p_conv.reshape(b, t, CONV_COLS), conv_buf, lw["conv_w"], lw["conv_b"],
                                   lw["conv_ln_g"], lw["conv_ln_b"], conv_tile)

    p3 = p_rwkv.reshape(b, t, RWKV_COLS)
    prev = jnp.concatenate([shift_prev[:, None, :], p3[:, :-1]], axis=1).reshape(b * t, RWKV_COLS)
    r, w, kr, vr, kk, bb, g, bonus = _rwkv_pre(p_rwkv, prev, lw["rwkv_mu"], lw["rwkv_w0"], lw["rwkv_w_up"],
                                               lw["rwkv_a0"], lw["rwkv_a_up"], lw["rwkv_g_up"], lw["rwkv_k_k"],
                                               lw["rwkv_k_a"], lw["rwkv_r_k"], tm)
    hf = lambda a: _heads_first(a, b, t)
    s0 = wkv0.reshape(b * RWKV_HEADS, HEAD_DIM, HEAD_DIM)
    if t % WKV_BLOCK == 0:
        y_wkv, wkv_new = _wkv_chunked(hf(r), hf(w), hf(kr), hf(vr), hf(kk), hf(bb), s0, wkv_chains)
    else:
        y_wkv, wkv_new = _wkv(hf(r), hf(w), hf(kr), hf(vr), hf(kk), hf(bb), s0, wkv_chains, wkv_tile)
    y_wkv = _tokens_first(y_wkv, b, t)

    y_att = attend(qb, k, v, kb, vb)

    x = _outproj(x, y_conv.reshape(b * t, CONV_CH), y_wkv, bonus, g, y_att, lw["rwkv_ln_g"], lw["rwkv_ln_b"],
                 lw["w_out"], tm)
    hb, e2, th, f1 = _peer_route(x, lw["norm2_g"], lw["peer_w_query"], lw["peer_sub_keys"], min(tm, 256))
    x = _peer_expert(x, hb, e2, th, f1, lw["peer_u"], lw["peer_vt"], tm)
    states = (k.reshape(b, t, ATT_HEADS, HEAD_DIM), v.reshape(b, t, ATT_HEADS, HEAD_DIM), conv_new,
              p3[:, -1], wkv_new.reshape(b, RWKV_HEADS, HEAD_DIM, HEAD_DIM))
    return x, states


def kernel(x_prompt, x_sample, cache_k, cache_v, state_conv, state_shift, state_wkv, page_table, norm1_g, w_in, conv_w, conv_b, conv_ln_g, conv_ln_b, rwkv_mu, rwkv_w0, rwkv_w_up, rwkv_a0, rwkv_a_up, rwkv_g_up, rwkv_k_k, rwkv_k_a, rwkv_r_k, rwkv_ln_g, rwkv_ln_b, att_bias, w_out, norm2_g, peer_w_query, peer_sub_keys, peer_u, peer_v, final_g):
    depth = w_in.shape[0]
    bp, tp, _ = x_prompt.shape
    bs, ts, _ = x_sample.shape
    n_pool = cache_k.shape[1]
    ck = cache_k.reshape(depth, n_pool, PAGE_SIZE, ATT_DIM)
    cv = cache_v.reshape(depth, n_pool, PAGE_SIZE, ATT_DIM)
    xp = x_prompt.reshape(bp * tp, D_MODEL)
    xs = x_sample.reshape(bs * ts, D_MODEL)
    st_p, st_s = [], []
    for li in range(depth):
        lw = dict(
            norm1_g=norm1_g[li], w_in=w_in[li].astype(BF16), conv_w=conv_w[li], conv_b=conv_b[li],
            conv_ln_g=conv_ln_g[li], conv_ln_b=conv_ln_b[li], rwkv_mu=rwkv_mu[li], rwkv_w0=rwkv_w0[li],
            rwkv_w_up=rwkv_w_up[li], rwkv_a0=rwkv_a0[li], rwkv_a_up=rwkv_a_up[li], rwkv_g_up=rwkv_g_up[li],
            rwkv_k_k=rwkv_k_k[li], rwkv_k_a=rwkv_k_a[li], rwkv_r_k=rwkv_r_k[li].reshape(-1),
            rwkv_ln_g=rwkv_ln_g[li], rwkv_ln_b=rwkv_ln_b[li], w_out=w_out[li].astype(BF16), norm2_g=norm2_g[li],
            peer_w_query=peer_w_query[li].astype(BF16),
            peer_sub_keys=peer_sub_keys[li].astype(BF16).reshape(2 * PEER_HEADS, N_KEYS, PEER_DK // 2),
            peer_u=peer_u[li].astype(BF16), peer_vt=peer_v[li].astype(BF16).T)
        bias = att_bias[li]

        def attend_prompt(qb, k, v, kb, vb):
            sq = lambda a: a.reshape(bp, tp, ATT_DIM)
            return _attn_prompt(sq(qb), sq(kb), sq(vb), bias).reshape(bp * tp, ATT_DIM)

        def attend_sample(qb, k, v, kb, vb):
            o = _attn_decode(li, qb.reshape(bs, ts, ATT_DIM), k.reshape(bs, ts, ATT_DIM),
                             v.reshape(bs, ts, ATT_DIM), ck, cv, page_table, bias)
            return o.reshape(bs * ts, ATT_DIM)

        xp, new_p = _layer(xp, bp, tp, jnp.zeros((bp, CONV_WIDTH - 1, CONV_CH), F32),
                           jnp.zeros((bp, RWKV_COLS), F32), jnp.zeros((bp, RWKV_HEADS, HEAD_DIM, HEAD_DIM), F32),
                           attend_prompt, lw, tm=512, conv_tile=512, wkv_chains=4, wkv_tile=256)
        xs, new_s = _layer(xs, bs, ts, state_conv[li], state_shift[li], state_wkv[li],
                           attend_sample, lw, tm=bs * ts, conv_tile=ts, wkv_chains=bp * RWKV_HEADS, wkv_tile=ts)
        st_p.append(new_p)
        st_s.append(new_s)
    y_prompt = _final_norm(xp, final_g, 512).reshape(bp, tp, D_MODEL)
    y_sample = _final_norm(xs, final_g, bs * ts).reshape(bs, ts, D_MODEL)
    stack = lambda st, j: jnp.stack([s[j] for s in st], axis=0)
    return (y_prompt, y_sample) + tuple(stack(st_p, j) for j in range(5)) + tuple(stack(st_s, j) for j in range(5))
```

```python
import functools

import jax
import jax.numpy as jnp
from jax import lax
from jax.experimental import pallas as pl
from jax.experimental.pallas import tpu as pltpu

F32 = jnp.float32
BF16 = jnp.bfloat16

D_MODEL = 1024
HEAD_DIM = 64
CONV_CH = 256
CONV_WIDTH = 31
CONV_HALO = 32
RWKV_HEADS = 6
RWKV_DIM = 384
ATT_HEADS = 6
ATT_DIM = 384
DECAY_LORA = 64
AAA_LORA = 64
GATE_LORA = 128
CONV_COLS = 512
RWKV_COLS = 1408
IN_COLS = 3072
PAGE_SIZE = 128
PEER_HEADS = 8
PEER_DK = 256
N_KEYS = 128
PEER_TOPK = 16
RMS_EPS = 1e-5
LN_EPS = 1e-5
GN_EPS = 64e-5

ATT_TILE = 256
PAGES_PER_STEP = 8
Q_PAD = 8
PEER_ROWS = 8
WKV_CHUNK_LOG2 = 6
WKV_CHUNK = 1 << WKV_CHUNK_LOG2
WKV_BLOCK = 128
VMEM_LIMIT = 56 * 1024 * 1024


def _cparams(sem):
    return pltpu.CompilerParams(dimension_semantics=sem, vmem_limit_bytes=VMEM_LIMIT)


def _sigmoid(x):
    return 1.0 / (1.0 + jnp.exp(-x))


def _softplus(x):
    return jnp.maximum(x, 0.0) + jnp.log(1.0 + jnp.exp(-jnp.abs(x)))


def _dot(a, b):
    return jnp.dot(a, b, preferred_element_type=F32)


def _dot_nt(a, b):
    return lax.dot_general(a, b, (((1,), (1,)), ((), ())), preferred_element_type=F32)


def _split2(x):
    hi = x.astype(BF16)
    lo = (x - hi.astype(F32)).astype(BF16)
    return hi, lo


def _dot3(a, b):
    ah, al = _split2(a)
    bh, bl = _split2(b)
    return _dot(ah, bh) + (_dot(ah, bl) + _dot(al, bh))


def _dot_exact_rhs(x, m):
    hi, lo = _split2(x)
    return _dot(hi, m) + _dot(lo, m)


def _rmsnorm(x, g):
    return x * lax.rsqrt(jnp.mean(x * x, axis=-1, keepdims=True) + RMS_EPS) * g


def _head_blockdiag(n):
    r = lax.broadcasted_iota(jnp.int32, (n, n), 0) // HEAD_DIM
    c = lax.broadcasted_iota(jnp.int32, (n, n), 1) // HEAD_DIM
    return (r == c).astype(BF16)


def _suffix_matrix(n):
    r = lax.broadcasted_iota(jnp.int32, (n, n), 0)
    c = lax.broadcasted_iota(jnp.int32, (n, n), 1)
    return jnp.where(r > c, -1.0, 0.0).astype(BF16)


def _row_spec(tm, cols):
    return pl.BlockSpec((tm, cols), lambda i: (i, 0))


def _const_spec(shape):
    return pl.BlockSpec(shape, lambda *_: (0,) * len(shape))


def _inproj_body(x_ref, g_ref, w_ref, conv_ref, rwkv_ref, k_ref, v_ref, qb_ref, kb_ref, vb_ref):
    h = _rmsnorm(x_ref[...], g_ref[...]).astype(BF16)
    o0, o1, o2, o3 = CONV_COLS, CONV_COLS + RWKV_COLS, CONV_COLS + RWKV_COLS + ATT_DIM, IN_COLS - ATT_DIM
    conv_ref[...] = _dot(h, w_ref[:, 0:o0])
    rwkv_ref[...] = _dot(h, w_ref[:, o0:o1])
    q = _dot(h, w_ref[:, o1:o2])
    k = _dot(h, w_ref[:, o2:o3])
    v = _dot(h, w_ref[:, o3:IN_COLS])
    k_ref[...] = k
    v_ref[...] = v
    qb_ref[...] = (q * (HEAD_DIM ** -0.5)).astype(BF16)
    kb_ref[...] = k.astype(BF16)
    vb_ref[...] = v.astype(BF16)


def _inproj(x, g, w_b, tm):
    n = x.shape[0]
    widths = (CONV_COLS, RWKV_COLS, ATT_DIM, ATT_DIM, ATT_DIM, ATT_DIM, ATT_DIM)
    dtypes = (F32, F32, F32, F32, BF16, BF16, BF16)
    return pl.pallas_call(
        _inproj_body,
        grid=(n // tm,),
        in_specs=[_row_spec(tm, D_MODEL), _const_spec((1, D_MODEL)), _const_spec((D_MODEL, IN_COLS))],
        out_specs=[_row_spec(tm, c) for c in widths],
        out_shape=[jax.ShapeDtypeStruct((n, c), d) for c, d in zip(widths, dtypes)],
        compiler_params=_cparams(("arbitrary",)),
        name="inproj",
    )(x, g.reshape(1, -1), w_b)


def _conv_body(p_ref, buf_ref, w_ref, cb_ref, lg_ref, lb_ref, y_ref, nb_ref, ext_ref, *, tt):
    keep = CONV_WIDTH - 1
    lead = CONV_HALO - keep

    @pl.when(pl.program_id(1) == 0)
    def _():
        ext_ref[0:CONV_HALO, :] = jnp.zeros((CONV_HALO, CONV_CH), F32)
        ext_ref[lead:CONV_HALO, :] = buf_ref[0]

    p = p_ref[0]
    ext_ref[CONV_HALO:CONV_HALO + tt, :] = p[:, :CONV_CH] * _sigmoid(p[:, CONV_CH:])
    acc = jnp.zeros((tt, CONV_CH), F32)
    for j in range(CONV_WIDTH):
        acc = acc + w_ref[j:j + 1, :] * ext_ref[lead + j:lead + j + tt, :]
    y = acc + cb_ref[...]
    mu = jnp.mean(y, axis=-1, keepdims=True)
    d = y - mu
    var = jnp.mean(d * d, axis=-1, keepdims=True)
    y = d * lax.rsqrt(var + LN_EPS) * lg_ref[...] + lb_ref[...]
    y_ref[0] = (y * _sigmoid(y)).astype(y_ref.dtype)
    nb_ref[0] = ext_ref[tt + lead:tt + CONV_HALO, :]
    tail = ext_ref[tt:tt + CONV_HALO, :]
    ext_ref[0:CONV_HALO, :] = tail


def _conv_mixer(p, buf, w, cb, lg, lb, tt):
    b, t, _ = p.shape
    keep = CONV_WIDTH - 1
    return pl.pallas_call(
        functools.partial(_conv_body, tt=tt),
        grid=(b, t // tt),
        in_specs=[pl.BlockSpec((1, tt, CONV_COLS), lambda i, j: (i, j, 0)),
                  pl.BlockSpec((1, keep, CONV_CH), lambda i, j: (i, 0, 0)),
                  _const_spec((CONV_WIDTH, CONV_CH)), _const_spec((1, CONV_CH)),
                  _const_spec((1, CONV_CH)), _const_spec((1, CONV_CH))],
        out_specs=[pl.BlockSpec((1, tt, CONV_CH), lambda i, j: (i, j, 0)),
                   pl.BlockSpec((1, keep, CONV_CH), lambda i, j: (i, 0, 0))],
        out_shape=[jax.ShapeDtypeStruct((b, t, CONV_CH), BF16),
                   jax.ShapeDtypeStruct((b, keep, CONV_CH), F32)],
        scratch_shapes=[pltpu.VMEM((tt + CONV_HALO, CONV_CH), F32)],
        compiler_params=_cparams(("arbitrary", "arbitrary")),
        name="conv_mixer",
    )(p, buf, w, cb.reshape(1, -1), lg.reshape(1, -1), lb.reshape(1, -1))


def _rwkv_pre_body(p_ref, prev_ref, mu_ref, w0_ref, wup_ref, a0_ref, aup_ref, gup_ref, kk_ref, ka_ref, rk_ref,
                   *outs, heads_first):
    p = p_ref[...]
    pm = p + (prev_ref[...] - p) * mu_ref[...]
    d = RWKV_DIM
    r = pm[:, 0:d]
    k = pm[:, d:2 * d]
    v = pm[:, 2 * d:3 * d]
    xw = pm[:, 3 * d:3 * d + DECAY_LORA]
    xa = pm[:, 3 * d + DECAY_LORA:3 * d + DECAY_LORA + AAA_LORA]
    xg = pm[:, 3 * d + DECAY_LORA + AAA_LORA:RWKV_COLS]
    bd = _head_blockdiag(d)
    w_log = -_softplus(-(w0_ref[...] + _dot3(jnp.tanh(xw), wup_ref[...]))) - 0.5
    log_decay = -jnp.exp(w_log)
    a = _sigmoid(a0_ref[...] + _dot3(xa, aup_ref[...]))
    g = _dot3(_sigmoid(xg), gup_ref[...])
    kk = k * kk_ref[...]
    k = k * (1.0 + (a - 1.0) * ka_ref[...])
    kk = kk / jnp.maximum(jnp.sqrt(_dot_exact_rhs(kk * kk, bd)), 1e-12)
    steps = (r, log_decay, k, v, kk, kk * a)
    outs[-2][...] = g
    outs[-1][...] = _dot_exact_rhs(r * k * rk_ref[...], bd) * v
    if not heads_first:
        for ref, val in zip(outs[:6], steps):
            ref[...] = val
        return
    head = lambda h: slice(h * HEAD_DIM, (h + 1) * HEAD_DIM)
    for ref, val in zip(outs[:6], steps):
        for h in range(RWKV_HEADS):
            ref[h] = val[:, head(h)]
    for ref, val in zip(outs[6:9], (log_decay, kk, v)):
        val_t = val.T
        for h in range(RWKV_HEADS):
            ref[h] = val_t[head(h), :]


def _rwkv_pre(p, prev, mu, w0, w_up, a0, a_up, g_up, k_k, k_a, r_k, tm, seq_len=None):
    n = p.shape[0]
    vec = lambda a: a.reshape(1, -1)
    d = RWKV_DIM
    if seq_len is None:
        out_specs = [_row_spec(tm, d)] * 8
        out_shape = [jax.ShapeDtypeStruct((n, d), F32)] * 8
    else:
        per_seq = seq_len // tm
        chains = n // seq_len * RWKV_HEADS
        slab = pl.BlockSpec((RWKV_HEADS, tm, HEAD_DIM), lambda i: (i // per_seq, i % per_seq, 0))
        slab_t = pl.BlockSpec((RWKV_HEADS, HEAD_DIM, tm), lambda i: (i // per_seq, 0, i % per_seq))
        out_specs = [slab] * 6 + [slab_t] * 3 + [_row_spec(tm, d)] * 2
        out_shape = ([jax.ShapeDtypeStruct((chains, seq_len, HEAD_DIM), F32)] * 6
                     + [jax.ShapeDtypeStruct((chains, HEAD_DIM, seq_len), F32)] * 3
                     + [jax.ShapeDtypeStruct((n, d), F32)] * 2)
    return pl.pallas_call(
        functools.partial(_rwkv_pre_body, heads_first=seq_len is not None),
        grid=(n // tm,),
        in_specs=[_row_spec(tm, RWKV_COLS), _row_spec(tm, RWKV_COLS), _const_spec((1, RWKV_COLS)),
                  _const_spec((1, d)), _const_spec((DECAY_LORA, d)), _const_spec((1, d)),
                  _const_spec((AAA_LORA, d)), _const_spec((GATE_LORA, d)),
                  _const_spec((1, d)), _const_spec((1, d)), _const_spec((1, d))],
        out_specs=out_specs,
        out_shape=out_shape,
        compiler_params=_cparams(("arbitrary",)),
        name="rwkv_pre",
    )(p, prev, vec(mu), vec(w0), w_up, vec(a0), a_up, g_up, vec(k_k), vec(k_a), vec(r_k))


def _wkv_body(r_ref, w_ref, k_ref, v_ref, kk_ref, b_ref, s0_ref, y_ref, sT_ref, s_scr, *, nc, tc):
    ti = pl.program_id(1)

    @pl.when(ti == 0)
    def _():
        s_scr[...] = s0_ref[...]

    eye = (lax.broadcasted_iota(jnp.int32, (HEAD_DIM, HEAD_DIM), 0)
           == lax.broadcasted_iota(jnp.int32, (HEAD_DIM, HEAD_DIM), 1))

    def step(t, carry):
        for c in range(nc):
            row = lambda ref: ref[c, pl.ds(t, 1), :]
            s = s_scr[c]
            sa = jnp.sum(s * row(kk_ref), axis=1, keepdims=True)
            vcol = jnp.sum(jnp.where(eye, row(v_ref), 0.0), axis=1, keepdims=True)
            s = s * jnp.exp(row(w_ref)) - sa * row(b_ref) + vcol * row(k_ref)
            s_scr[c] = s
            ycol = jnp.sum(s * row(r_ref), axis=1, keepdims=True)
            y_ref[c, pl.ds(t, 1), :] = jnp.sum(jnp.where(eye, ycol, 0.0), axis=0, keepdims=True)
        return carry

    lax.fori_loop(0, tc, step, 0)

    @pl.when(ti == pl.num_programs(1) - 1)
    def _():
        sT_ref[...] = s_scr[...]


def _wkv(r, w, k, v, kk, b, s0, nc, tc):
    n, t, _ = r.shape
    seq = pl.BlockSpec((nc, tc, HEAD_DIM), lambda i, j: (i, j, 0))
    st = pl.BlockSpec((nc, HEAD_DIM, HEAD_DIM), lambda i, j: (i, 0, 0))
    return pl.pallas_call(
        functools.partial(_wkv_body, nc=nc, tc=tc),
        grid=(n // nc, t // tc),
        in_specs=[seq] * 6 + [st],
        out_specs=[seq, st],
        out_shape=[jax.ShapeDtypeStruct((n, t, HEAD_DIM), F32),
                   jax.ShapeDtypeStruct((n, HEAD_DIM, HEAD_DIM), F32)],
        scratch_shapes=[pltpu.VMEM((nc, HEAD_DIM, HEAD_DIM), F32)],
        compiler_params=_cparams(("arbitrary", "arbitrary")),
        name="wkv",
    )(r, w, k, v, kk, b, s0)


def _mm(a, b):
    return _dot3(a, b)


def _mm_nt(a, b):
    ah, al = _split2(a)
    bh, bl = _split2(b)
    return _dot_nt(ah, bh) + (_dot_nt(ah, bl) + _dot_nt(al, bh))


def _split3(x):
    p1 = x.astype(BF16)
    r1 = x - p1.astype(F32)
    p2 = r1.astype(BF16)
    p3 = (r1 - p2.astype(F32)).astype(BF16)
    return p1, p2, p3


def _wkv_chunk_body(r_ref, lw_ref, k_ref, v_ref, kk_ref, b_ref, lwt_ref, kkt_ref, vt_ref, s0_ref,
                    y_ref, sT_ref, s_scr, *, nc):
    ti = pl.program_id(1)
    c_len = WKV_CHUNK

    @pl.when(ti == 0)
    def _():
        s_scr[...] = s0_ref[...]

    row = lax.broadcasted_iota(jnp.int32, (c_len, c_len), 0)
    col = lax.broadcasted_iota(jnp.int32, (c_len, c_len), 1)
    strict = row > col
    incl = row >= col
    lower = incl.astype(BF16)
    upper = (row <= col).astype(BF16)
    eye = (row == col).astype(F32)

    n_sub = WKV_BLOCK // c_len
    units = [(j, c) for j in range(n_sub) for c in range(nc)]
    each = lambda f, *lists: [f(*args) for args in zip(*lists)]
    ts = lambda j: slice(j * c_len, (j + 1) * c_len)
    rows = lambda ref: [ref[c, ts(j), :] for j, c in units]
    cols = lambda ref: [ref[c, :, ts(j)] for j, c in units]
    r, lw, k, v, kk, b = (rows(ref) for ref in (r_ref, lw_ref, k_ref, v_ref, kk_ref, b_ref))
    lwt, kkt, vt = (cols(ref) for ref in (lwt_ref, kkt_ref, vt_ref))
    cum = each(lambda x: sum(_dot(lower, piece) for piece in _split3(x)), lw)
    cumt = each(lambda x: sum(_dot(piece, upper) for piece in _split3(x)), lwt)
    w_incl = each(jnp.exp, cum)
    w_last = each(lambda w: w[c_len - 1:c_len, :], w_incl)
    inv = each(lambda x: jnp.exp(-x), cum)
    qt = each(lambda x, cs, l: x * jnp.exp(cs - l), kk, cum, lw)
    qtt = each(lambda x, cs, l: x * jnp.exp(cs - l), kkt, cumt, lwt)
    rt = each(jnp.multiply, r, w_incl)
    kt = each(jnp.multiply, k, inv)
    bt = each(jnp.multiply, b, inv)
    a_qb = each(lambda x, y: jnp.where(strict, _mm_nt(x, y), 0.0), qt, bt)
    a_qk = each(lambda x, y: jnp.where(strict, _mm_nt(x, y), 0.0), qt, kt)
    a_rk = each(lambda x, y: jnp.where(incl, _mm_nt(x, y), 0.0), rt, kt)
    a_rb = each(lambda x, y: jnp.where(incl, _mm_nt(x, y), 0.0), rt, bt)
    tinv = each(lambda n: eye - n, a_qb)
    power = a_qb
    for _ in range(WKV_CHUNK_LOG2 - 1):
        power = each(_mm, power, power)
        tinv = each(lambda t, p: t + _mm(t, p), tinv, power)
    tq = each(_mm, tinv, qt)
    av = each(_mm, a_qk, v)
    av_t = each(_mm_nt, vt, a_qk)
    tq_t = each(_mm_nt, qtt, tinv)
    tav = each(_mm, tinv, av)
    tav_t = each(_mm_nt, av_t, tinv)
    ark_v = each(_mm, a_rk, v)
    rq = each(lambda x, a, y: x - _mm(a, y), rt, a_rb, tq)
    yv = each(lambda x, a, y: x - _mm(a, y), ark_v, a_rb, tav)
    gain = each(lambda x, y, w: _mm(x, y * w), vt, kt, w_last)
    bw = each(jnp.multiply, bt, w_last)
    for j in range(n_sub):
        ids = [j * nc + c for c in range(nc)]
        s = [s_scr[c] for c in range(nc)]
        sa_t = [_mm(s[c], tq_t[i]) + tav_t[i] for c, i in enumerate(ids)]
        ys = [_mm_nt(rq[i], s[c]) + yv[i] for c, i in enumerate(ids)]
        new = [s[c] * w_last[i] + gain[i] - _mm(sa_t[c], bw[i]) for c, i in enumerate(ids)]
        for c, i in enumerate(ids):
            y_ref[c, ts(j), :] = ys[c]
            s_scr[c] = new[c]

    @pl.when(ti == pl.num_programs(1) - 1)
    def _():
        sT_ref[...] = s_scr[...]


def _wkv_chunked(r, lw, k, v, kk, b, lwt, kkt, vt, s0, nc):
    n, t, _ = r.shape
    seq = pl.BlockSpec((nc, WKV_BLOCK, HEAD_DIM), lambda i, j: (i, j, 0))
    seq_t = pl.BlockSpec((nc, HEAD_DIM, WKV_BLOCK), lambda i, j: (i, 0, j))
    st = pl.BlockSpec((nc, HEAD_DIM, HEAD_DIM), lambda i, j: (i, 0, 0))
    return pl.pallas_call(
        functools.partial(_wkv_chunk_body, nc=nc),
        grid=(n // nc, t // WKV_BLOCK),
        in_specs=[seq] * 6 + [seq_t] * 3 + [st],
        out_specs=[seq, st],
        out_shape=[jax.ShapeDtypeStruct((n, t, HEAD_DIM), F32),
                   jax.ShapeDtypeStruct((n, HEAD_DIM, HEAD_DIM), F32)],
        scratch_shapes=[pltpu.VMEM((nc, HEAD_DIM, HEAD_DIM), F32)],
        compiler_params=_cparams(("arbitrary", "arbitrary")),
        name="wkv_chunked",
    )(r, lw, k, v, kk, b, lwt, kkt, vt, s0)


def _sb_block(q, kb, vb, bias, rest, umat, mask):
    z = _dot_nt(q, kb) + bias
    sp = _softplus(z)
    counted = sp if mask is None else jnp.where(mask, sp, 0.0)
    after = _dot(counted.astype(BF16), umat)
    a = jnp.exp((z - sp) + after + rest)
    if mask is not None:
        a = jnp.where(mask, a, 0.0)
    out = _dot(a.astype(BF16), vb)
    return out, after[:, 0:1] - counted[:, 0:1]


def _attn_prompt_body(q_ref, k_ref, v_ref, bias_ref, o_ref, *, tile):
    qi = pl.program_id(1)
    heads = range(ATT_HEADS)
    hs = lambda a, h: a[:, h * HEAD_DIM:(h + 1) * HEAD_DIM]
    q_all = q_ref[0]
    qs = [hs(q_all, h) for h in heads]
    biases = [bias_ref[h] for h in heads]
    umat = _suffix_matrix(tile)
    causal = (lax.broadcasted_iota(jnp.int32, (tile, tile), 1)
              < lax.broadcasted_iota(jnp.int32, (tile, tile), 0))

    def visit(j, accs, rests, mask):
        start = pl.multiple_of(j * tile, tile)
        kb = k_ref[0, pl.ds(start, tile), :]
        vb = v_ref[0, pl.ds(start, tile), :]
        zs = [_dot_nt(qs[h], hs(kb, h)) + biases[h] for h in heads]
        sps = [_softplus(z) for z in zs]
        counted = sps if mask is None else [jnp.where(mask, sp, 0.0) for sp in sps]
        afters = [_dot(sp.astype(BF16), umat) for sp in counted]
        probs = [jnp.exp((zs[h] - sps[h]) + afters[h] + rests[h]) for h in heads]
        if mask is not None:
            probs = [jnp.where(mask, a, 0.0) for a in probs]
        outs = [_dot(probs[h].astype(BF16), hs(vb, h)) for h in heads]
        new_accs = tuple(accs[h] + outs[h] for h in heads)
        new_rests = tuple(rests[h] + afters[h][:, 0:1] - counted[h][:, 0:1] for h in heads)
        return new_accs, new_rests

    accs = tuple(jnp.zeros((tile, HEAD_DIM), F32) for _ in heads)
    rests = tuple(jnp.zeros((tile, 1), F32) for _ in heads)
    accs, rests = visit(qi, accs, rests, causal)
    accs, rests = lax.fori_loop(0, qi, lambda jj, c: visit(qi - 1 - jj, c[0], c[1], None), (accs, rests))
    r = lax.broadcasted_iota(jnp.int32, (HEAD_DIM, ATT_DIM), 0)
    c = lax.broadcasted_iota(jnp.int32, (HEAD_DIM, ATT_DIM), 1)
    out = jnp.zeros((tile, ATT_DIM), F32)
    for h in heads:
        out = out + _dot(accs[h].astype(BF16), (c == r + h * HEAD_DIM).astype(BF16))
    o_ref[0] = out.astype(o_ref.dtype)


def _attn_prompt(qb, kb, vb, bias):
    b, t, d = qb.shape
    tile = ATT_TILE
    bias_b = jnp.broadcast_to(bias.astype(F32)[:, None, None], (ATT_HEADS, 1, tile))
    qspec = pl.BlockSpec((1, tile, d), lambda i, l: (i, l, 0))
    kvspec = pl.BlockSpec((1, t, d), lambda i, l: (i, 0, 0))
    return pl.pallas_call(
        functools.partial(_attn_prompt_body, tile=tile),
        grid=(b, t // tile),
        in_specs=[qspec, kvspec, kvspec, _const_spec((ATT_HEADS, 1, tile))],
        out_specs=qspec,
        out_shape=jax.ShapeDtypeStruct((b, t, d), BF16),
        compiler_params=_cparams(("arbitrary", "arbitrary")),
        name="attn_prompt",
    )(qb, kb, vb, bias_b)


def _attn_decode_body(pt_ref, q_ref, bias_ref, kn_ref, vn_ref, *refs, npg, nq):
    k_refs = refs[:npg]
    v_refs = refs[npg:2 * npg]
    o_ref, rest_ref, acc_ref = refs[2 * npg:]
    jb = pl.program_id(1)
    rows = ATT_HEADS * Q_PAD
    pages = range(npg)
    spread = lambda tot: jnp.broadcast_to(tot, (rows, PAGE_SIZE))
    q = q_ref[0]
    bias = bias_ref[...]
    umat = _suffix_matrix(PAGE_SIZE)

    @pl.when(jb == 0)
    def _():
        kidx = lax.broadcasted_iota(jnp.int32, (rows, PAGE_SIZE), 1)
        qidx = lax.broadcasted_iota(jnp.int32, (rows, PAGE_SIZE), 0) % Q_PAD
        out, tot = _sb_block(q, kn_ref[0].astype(BF16), vn_ref[0].astype(BF16), bias,
                             jnp.zeros((rows, 1), F32), umat, (kidx < qidx) & (kidx < nq))
        acc_ref[...] = out
        rest_ref[...] = spread(tot)

    zs = [_dot_nt(q, k_refs[s][0, 0].astype(BF16)) + bias for s in pages]
    sps = [_softplus(z) for z in zs]
    afters = [_dot(sp.astype(BF16), umat) for sp in sps]
    rest = rest_ref[...]
    probs = []
    for s in pages:
        probs.append(jnp.exp((zs[s] - sps[s]) + afters[s] + rest).astype(BF16))
        rest = rest + spread(afters[s][:, 0:1] - sps[s][:, 0:1])
    rest_ref[...] = rest
    out = _dot(probs[0], v_refs[0][0, 0].astype(BF16))
    for s in pages[1:]:
        out = out + _dot(probs[s], v_refs[s][0, 0].astype(BF16))
    acc_ref[...] += out

    @pl.when(jb == pl.num_programs(1) - 1)
    def _():
        head_of_row = lax.broadcasted_iota(jnp.int32, (rows, ATT_DIM), 0) // Q_PAD
        head_of_col = lax.broadcasted_iota(jnp.int32, (rows, ATT_DIM), 1) // HEAD_DIM
        kept = jnp.where(head_of_row == head_of_col, acc_ref[...], 0.0)
        out = kept[0:Q_PAD]
        for h in range(1, ATT_HEADS):
            out = out + kept[h * Q_PAD:(h + 1) * Q_PAD]
        o_ref[0] = out.astype(o_ref.dtype)


def _attn_decode(li, qb, k_new, v_new, cache_k, cache_v, page_table, bias):
    b, t, _ = qb.shape
    n_pages = page_table.shape[1]
    npg = PAGES_PER_STEP
    rows = ATT_HEADS * Q_PAD
    qpad = jnp.pad(qb, ((0, 0), (0, Q_PAD - t), (0, 0)))
    head_of_col = jnp.arange(ATT_DIM) // HEAD_DIM
    qbd = jnp.where(head_of_col[None, None, None, :] == jnp.arange(ATT_HEADS)[None, :, None, None],
                    qpad[:, None], jnp.zeros((), BF16)).reshape(b, rows, ATT_DIM)
    bias_b = jnp.broadcast_to(jnp.repeat(bias.astype(F32), Q_PAD)[:, None], (rows, PAGE_SIZE))
    kn = jnp.pad(k_new, ((0, 0), (0, PAGE_SIZE - t), (0, 0)))
    vn = jnp.pad(v_new, ((0, 0), (0, PAGE_SIZE - t), (0, 0)))

    def page_spec(s):
        return pl.BlockSpec((1, 1, PAGE_SIZE, ATT_DIM),
                            lambda i, j, pt: (li, pt[i * n_pages + n_pages - 1 - (j * npg + s)], 0, 0))

    per_b = lambda r, c: pl.BlockSpec((1, r, c), lambda i, j, pt: (i, 0, 0))
    grid_spec = pltpu.PrefetchScalarGridSpec(
        num_scalar_prefetch=1,
        grid=(b, n_pages // npg),
        in_specs=[per_b(rows, ATT_DIM), pl.BlockSpec((rows, PAGE_SIZE), lambda i, j, pt: (0, 0)),
                  per_b(PAGE_SIZE, ATT_DIM), per_b(PAGE_SIZE, ATT_DIM)]
                 + [page_spec(s) for s in range(npg)] * 2,
        out_specs=per_b(Q_PAD, ATT_DIM),
        scratch_shapes=[pltpu.VMEM((rows, PAGE_SIZE), F32), pltpu.VMEM((rows, ATT_DIM), F32)],
    )
    out = pl.pallas_call(
        functools.partial(_attn_decode_body, npg=npg, nq=t),
        grid_spec=grid_spec,
        out_shape=jax.ShapeDtypeStruct((b, Q_PAD, ATT_DIM), BF16),
        compiler_params=_cparams(("arbitrary", "arbitrary")),
        name="attn_decode",
    )(page_table.reshape(-1), qbd, bias_b, kn, vn, *([cache_k] * npg), *([cache_v] * npg))
    return out[:, :t]


def _outproj_body(x_ref, yc_ref, yw_ref, bonus_ref, g_ref, ya_ref, lng_ref, lnb_ref, w_ref, o_ref):
    bd = _head_blockdiag(RWKV_DIM)
    y = yw_ref[...]
    inv = 1.0 / HEAD_DIM
    d = y - _dot_exact_rhs(y, bd) * inv
    var = _dot_exact_rhs(d * d, bd) * inv
    yn = d * lax.rsqrt(var + GN_EPS) * lng_ref[...] + lnb_ref[...]
    yr = ((yn + bonus_ref[...]) * g_ref[...]).astype(BF16)
    c0, c1 = CONV_CH, CONV_CH + RWKV_DIM
    o_ref[...] = (x_ref[...] + _dot(yc_ref[...], w_ref[0:c0, :]) + _dot(yr, w_ref[c0:c1, :])
                  + _dot(ya_ref[...], w_ref[c1:D_MODEL, :]))


def _outproj(x, yc, yw, bonus, g, ya, ln_g, ln_b, w_b, tm):
    n = x.shape[0]
    return pl.pallas_call(
        _outproj_body,
        grid=(n // tm,),
        in_specs=[_row_spec(tm, D_MODEL), _row_spec(tm, CONV_CH), _row_spec(tm, RWKV_DIM), _row_spec(tm, RWKV_DIM),
                  _row_spec(tm, RWKV_DIM), _row_spec(tm, ATT_DIM), _const_spec((1, RWKV_DIM)),
                  _const_spec((1, RWKV_DIM)), _const_spec((D_MODEL, D_MODEL))],
        out_specs=_row_spec(tm, D_MODEL),
        out_shape=jax.ShapeDtypeStruct((n, D_MODEL), F32),
        compiler_params=_cparams(("arbitrary",)),
        name="outproj",
    )(x, yc, yw, bonus, g, ya, ln_g.reshape(1, -1), ln_b.reshape(1, -1), w_b)


def _extract_top(x, n, idx):
    width = x.shape[0]
    orig = x
    vals = []
    for _ in range(n):
        m = jnp.max(x, axis=0, keepdims=True)
        first = jnp.min(jnp.where(x == m, idx, float(width)), axis=0, keepdims=True)
        x = jnp.where(idx == first, -jnp.inf, x)
        vals.append(m)
    return vals, x != orig


def _peer_route_body(x_ref, g_ref, wq_ref, sk_ref, h_out, e2_out, th_out, f1_out):
    tm = x_ref.shape[0]
    kk = PEER_TOPK
    hb = _rmsnorm(x_ref[...], g_ref[...]).astype(BF16)
    h_out[...] = hb
    idx = lax.broadcasted_iota(jnp.int32, (N_KEYS, tm), 0).astype(F32)
    cidx = lax.broadcasted_iota(jnp.int32, (kk * kk, tm), 0).astype(F32)
    sidx = lax.broadcasted_iota(jnp.int32, (kk, tm), 0).astype(F32)
    half = PEER_DK // 2
    for h in range(PEER_HEADS):
        q = _dot(hb, wq_ref[:, h * PEER_DK:(h + 1) * PEER_DK])
        s1 = _dot_nt(sk_ref[2 * h], q[:, :half].astype(BF16))
        s2 = _dot_nt(sk_ref[2 * h + 1], q[:, half:].astype(BF16))
        v1, in1 = _extract_top(s1, kk, idx)
        v2, in2 = _extract_top(s2, kk, idx)
        v2_stack = jnp.broadcast_to(v2[kk - 1], (kk, tm))
        for i in range(kk - 1):
            v2_stack = jnp.where(sidx == float(i), v2[i], v2_stack)
        c2 = jnp.concatenate([v2_stack] * kk, axis=0)
        c1 = jnp.concatenate([jnp.broadcast_to(v, (kk, tm)) for v in v1], axis=0)
        vc, taken = _extract_top(c1 + c2, kk, cidx)
        z = jnp.ones((1, tm), F32)
        for i in range(1, kk):
            z = z + jnp.exp(vc[i] - vc[0])
        taken2 = jnp.where(taken, c2, jnp.inf)
        th = jnp.full((N_KEYS, tm), 2.0, F32)
        for a in range(kk):
            lo = jnp.min(taken2[a * kk:(a + 1) * kk], axis=0, keepdims=True)
            the = jnp.where(lo == jnp.inf, 2.0, jnp.exp(lo - v2[0]))
            th = jnp.where(in1 & (s1 == v1[a]), the, th)
        th_out[h] = th
        f1_out[h] = jnp.where(in1, jnp.exp(s1 - v1[0]), 0.0) * (1.0 / z)
        e2_out[h] = jnp.where(in2, jnp.exp(s2 - v2[0]), 0.0)


def _peer_route(x, g, wq_b, sk_b, tm):
    n = x.shape[0]
    keyed = pl.BlockSpec((PEER_HEADS, N_KEYS, tm), lambda i: (0, 0, i))
    return pl.pallas_call(
        _peer_route_body,
        grid=(n // tm,),
        in_specs=[_row_spec(tm, D_MODEL), _const_spec((1, D_MODEL)), _const_spec((D_MODEL, PEER_HEADS * PEER_DK)),
                  _const_spec((2 * PEER_HEADS, N_KEYS, PEER_DK // 2))],
        out_specs=[_row_spec(tm, D_MODEL), keyed, keyed, keyed],
        out_shape=[jax.ShapeDtypeStruct((n, D_MODEL), BF16)]
                  + [jax.ShapeDtypeStruct((PEER_HEADS, N_KEYS, n), F32)] * 3,
        compiler_params=_cparams(("arbitrary",)),
        name="peer_route",
    )(x, g.reshape(1, -1), wq_b, sk_b)


def _gelu_tanh(x):
    return 0.5 * x * (1.0 + jnp.tanh(0.7978845608028654 * (x + 0.044715 * (x * x * x))))


def _peer_expert_body(x_ref, h_ref, e2_ref, th_ref, f1_ref, u_ref, vt_ref, o_ref, acc_ref):
    ec = pl.program_id(1)

    @pl.when(ec == 0)
    def _():
        acc_ref[...] = jnp.zeros(acc_ref.shape, F32)

    tm = h_ref.shape[0]
    n_grp = 2 if tm % 512 == 0 else 1
    grp = tm // n_grp
    toks = [slice(g * grp, (g + 1) * grp) for g in range(n_grp)]
    hids = [_dot_nt(u_ref[...], h_ref[tk, :]) for tk in toks]
    for g, tk in enumerate(toks):
        weights = []
        for r in range(PEER_ROWS):
            gate = jnp.zeros((N_KEYS, grp), F32)
            for h in range(PEER_HEADS):
                e2 = e2_ref[h, :, tk]
                gate = gate + jnp.where(e2 >= th_ref[h, 0, r:r + 1, tk], e2, 0.0) * f1_ref[h, 0, r:r + 1, tk]
            weights.append((gate * _gelu_tanh(hids[g][r * N_KEYS:(r + 1) * N_KEYS])).astype(BF16))
        acc_ref[:, tk] += _dot(vt_ref[...], jnp.concatenate(weights, axis=0))

    @pl.when(ec == pl.num_programs(1) - 1)
    def _():
        o_ref[...] = x_ref[...] + acc_ref[...].T


def _peer_expert(x, hb, e2, th, f1, u_b, vt_b, tm):
    n = x.shape[0]
    te = PEER_ROWS * N_KEYS
    ne = u_b.shape[0] // te
    chunked = lambda a: a.reshape(PEER_HEADS, ne, PEER_ROWS, n)
    tok = lambda c: pl.BlockSpec((tm, c), lambda i, e: (i, 0))
    row = pl.BlockSpec((PEER_HEADS, 1, PEER_ROWS, tm), lambda i, e: (0, e, 0, i))
    return pl.pallas_call(
        _peer_expert_body,
        grid=(n // tm, ne),
        in_specs=[tok(D_MODEL), tok(D_MODEL), pl.BlockSpec((PEER_HEADS, N_KEYS, tm), lambda i, e: (0, 0, i)),
                  row, row, pl.BlockSpec((te, D_MODEL), lambda i, e: (e, 0)),
                  pl.BlockSpec((D_MODEL, te), lambda i, e: (0, e))],
        out_specs=tok(D_MODEL),
        out_shape=jax.ShapeDtypeStruct((n, D_MODEL), F32),
        scratch_shapes=[pltpu.VMEM((D_MODEL, tm), F32)],
        compiler_params=_cparams(("arbitrary", "arbitrary")),
        name="peer_expert",
    )(x, hb, e2, chunked(th), chunked(f1), u_b, vt_b)


def _final_norm_body(x_ref, g_ref, o_ref):
    o_ref[...] = _rmsnorm(x_ref[...], g_ref[...])


def _final_norm(x, g, tm):
    n = x.shape[0]
    return pl.pallas_call(
        _final_norm_body,
        grid=(n // tm,),
        in_specs=[_row_spec(tm, D_MODEL), _const_spec((1, D_MODEL))],
        out_specs=_row_spec(tm, D_MODEL),
        out_shape=jax.ShapeDtypeStruct((n, D_MODEL), F32),
        compiler_params=_cparams(("arbitrary",)),
        name="final_norm",
    )(x, g.reshape(1, -1))


def _heads_first(a, b, t):
    h = a.shape[1] // HEAD_DIM
    return a.reshape(b, t, h, HEAD_DIM).transpose(0, 2, 1, 3).reshape(b * h, t, HEAD_DIM)


def _tokens_first(a, b, t):
    h = a.shape[0] // b
    return a.reshape(b, h, t, HEAD_DIM).transpose(0, 2, 1, 3).reshape(b * t, h * HEAD_DIM)


def _layer(x, b, t, conv_buf, shift_prev, wkv0, attend, lw, tm, conv_tile, wkv_chains, wkv_tile):
    p_conv, p_rwkv, k, v, qb, kb, vb = _inproj(x, lw["norm1_g"], lw["w_in"], tm)

    y_conv, conv_new = _conv_mixer(p_conv.reshape(b, t, CONV_COLS), conv_buf, lw["conv_w"], lw["conv_b"],
                                   lw["conv_ln_g"], lw["conv_ln_b"], conv_tile)

    p3 = p_rwkv.reshape(b, t, RWKV_COLS)
    prev = jnp.concatenate([shift_prev[:, None, :], p3[:, :-1]], axis=1).reshape(b * t, RWKV_COLS)
    pre = functools.partial(_rwkv_pre, p_rwkv, prev, lw["rwkv_mu"], lw["rwkv_w0"], lw["rwkv_w_up"], lw["rwkv_a0"],
                            lw["rwkv_a_up"], lw["rwkv_g_up"], lw["rwkv_k_k"], lw["rwkv_k_a"], lw["rwkv_r_k"], tm)
    s0 = wkv0.reshape(b * RWKV_HEADS, HEAD_DIM, HEAD_DIM)
    if t % WKV_BLOCK == 0 and t % tm == 0:
        *steps, g, bonus = pre(seq_len=t)
        y_wkv, wkv_new = _wkv_chunked(*steps, s0, wkv_chains)
    else:
        *steps, g, bonus = pre()
        y_wkv, wkv_new = _wkv(*(_heads_first(a, b, t) for a in steps), s0, wkv_chains, wkv_tile)
    y_wkv = _tokens_first(y_wkv, b, t)

    y_att = attend(qb, k, v, kb, vb)

    x = _outproj(x, y_conv.reshape(b * t, CONV_CH), y_wkv, bonus, g, y_att, lw["rwkv_ln_g"], lw["rwkv_ln_b"],
                 lw["w_out"], tm)
    hb, e2, th, f1 = _peer_route(x, lw["norm2_g"], lw["peer_w_query"], lw["peer_sub_keys"], min(tm, 256))
    x = _peer_expert(x, hb, e2, th, f1, lw["peer_u"], lw["peer_vt"], tm)
    states = (k.reshape(b, t, ATT_HEADS, HEAD_DIM), v.reshape(b, t, ATT_HEADS, HEAD_DIM), conv_new,
              p3[:, -1], wkv_new.reshape(b, RWKV_HEADS, HEAD_DIM, HEAD_DIM))
    return x, states


def kernel(x_prompt, x_sample, cache_k, cache_v, state_conv, state_shift, state_wkv, page_table, norm1_g, w_in, conv_w, conv_b, conv_ln_g, conv_ln_b, rwkv_mu, rwkv_w0, rwkv_w_up, rwkv_a0, rwkv_a_up, rwkv_g_up, rwkv_k_k, rwkv_k_a, rwkv_r_k, rwkv_ln_g, rwkv_ln_b, att_bias, w_out, norm2_g, peer_w_query, peer_sub_keys, peer_u, peer_v, final_g):
    depth = w_in.shape[0]
    bp, tp, _ = x_prompt.shape
    bs, ts, _ = x_sample.shape
    n_pool = cache_k.shape[1]
    ck = cache_k.reshape(depth, n_pool, PAGE_SIZE, ATT_DIM)
    cv = cache_v.reshape(depth, n_pool, PAGE_SIZE, ATT_DIM)
    xp = x_prompt.reshape(bp * tp, D_MODEL)
    xs = x_sample.reshape(bs * ts, D_MODEL)
    st_p, st_s = [], []
    for li in range(depth):
        lw = dict(
            norm1_g=norm1_g[li], w_in=w_in[li].astype(BF16), conv_w=conv_w[li], conv_b=conv_b[li],
            conv_ln_g=conv_ln_g[li], conv_ln_b=conv_ln_b[li], rwkv_mu=rwkv_mu[li], rwkv_w0=rwkv_w0[li],
            rwkv_w_up=rwkv_w_up[li], rwkv_a0=rwkv_a0[li], rwkv_a_up=rwkv_a_up[li], rwkv_g_up=rwkv_g_up[li],
            rwkv_k_k=rwkv_k_k[li], rwkv_k_a=rwkv_k_a[li], rwkv_r_k=rwkv_r_k[li].reshape(-1),
            rwkv_ln_g=rwkv_ln_g[li], rwkv_ln_b=rwkv_ln_b[li], w_out=w_out[li].astype(BF16), norm2_g=norm2_g[li],
            peer_w_query=peer_w_query[li].astype(BF16),
            peer_sub_keys=peer_sub_keys[li].astype(BF16).reshape(2 * PEER_HEADS, N_KEYS, PEER_DK // 2),
            peer_u=peer_u[li].astype(BF16), peer_vt=peer_v[li].astype(BF16).T)
        bias = att_bias[li]

        def attend_prompt(qb, k, v, kb, vb):
            sq = lambda a: a.reshape(bp, tp, ATT_DIM)
            return _attn_prompt(sq(qb), sq(kb), sq(vb), bias).reshape(bp * tp, ATT_DIM)

        def attend_sample(qb, k, v, kb, vb):
            o = _attn_decode(li, qb.reshape(bs, ts, ATT_DIM), k.reshape(bs, ts, ATT_DIM),
                             v.reshape(bs, ts, ATT_DIM), ck, cv, page_table, bias)
            return o.reshape(bs * ts, ATT_DIM)

        xp, new_p = _layer(xp, bp, tp, jnp.zeros((bp, CONV_WIDTH - 1, CONV_CH), F32),
                           jnp.zeros((bp, RWKV_COLS), F32), jnp.zeros((bp, RWKV_HEADS, HEAD_DIM, HEAD_DIM), F32),
                           attend_prompt, lw, tm=512, conv_tile=512, wkv_chains=4, wkv_tile=256)
        xs, new_s = _layer(xs, bs, ts, state_conv[li], state_shift[li], state_wkv[li],
                           attend_sample, lw, tm=bs * ts, conv_tile=ts, wkv_chains=bp * RWKV_HEADS, wkv_tile=ts)
        st_p.append(new_p)
        st_s.append(new_s)
    y_prompt = _final_norm(xp, final_g, 512).reshape(bp, tp, D_MODEL)
    y_sample = _final_norm(xs, final_g, bs * ts).reshape(bs, ts, D_MODEL)
    stack = lambda st, j: jnp.stack([s[j] for s in st], axis=0)
    return (y_prompt, y_sample) + tuple(stack(st_p, j) for j in range(5)) + tuple(stack(st_s, j) for j in range(5))
```

```python
import functools

import jax
import jax.numpy as jnp
from jax import lax
from jax.experimental import pallas as pl
from jax.experimental.pallas import tpu as pltpu

F32 = jnp.float32
BF16 = jnp.bfloat16

D_MODEL = 1024
HEAD_DIM = 64
CONV_CH = 256
CONV_WIDTH = 31
CONV_HALO = 32
RWKV_HEADS = 6
RWKV_DIM = 384
ATT_HEADS = 6
ATT_DIM = 384
DECAY_LORA = 64
AAA_LORA = 64
GATE_LORA = 128
CONV_COLS = 512
RWKV_COLS = 1408
IN_COLS = 3072
PAGE_SIZE = 128
PEER_HEADS = 8
PEER_DK = 256
N_KEYS = 128
PEER_TOPK = 16
RMS_EPS = 1e-5
LN_EPS = 1e-5
GN_EPS = 64e-5

ATT_TILE = 256
PAGES_PER_STEP = 8
Q_PAD = 8
PEER_ROWS = 8
WKV_CHUNK_LOG2 = 6
WKV_CHUNK = 1 << WKV_CHUNK_LOG2
WKV_BLOCK = 128
VMEM_LIMIT = 56 * 1024 * 1024


def _cparams(sem):
    return pltpu.CompilerParams(dimension_semantics=sem, vmem_limit_bytes=VMEM_LIMIT)


def _sigmoid(x):
    return 1.0 / (1.0 + jnp.exp(-x))


def _softplus(x):
    return jnp.maximum(x, 0.0) + jnp.log(1.0 + jnp.exp(-jnp.abs(x)))


def _dot(a, b):
    return jnp.dot(a, b, preferred_element_type=F32)


def _dot_nt(a, b):
    return lax.dot_general(a, b, (((1,), (1,)), ((), ())), preferred_element_type=F32)


def _split2(x):
    hi = x.astype(BF16)
    lo = (x - hi.astype(F32)).astype(BF16)
    return hi, lo


def _dot3(a, b):
    ah, al = _split2(a)
    bh, bl = _split2(b)
    return _dot(ah, bh) + (_dot(ah, bl) + _dot(al, bh))


def _dot_exact_rhs(x, m):
    hi, lo = _split2(x)
    return _dot(hi, m) + _dot(lo, m)


def _rmsnorm(x, g):
    return x * lax.rsqrt(jnp.mean(x * x, axis=-1, keepdims=True) + RMS_EPS) * g


def _head_blockdiag(n):
    r = lax.broadcasted_iota(jnp.int32, (n, n), 0) // HEAD_DIM
    c = lax.broadcasted_iota(jnp.int32, (n, n), 1) // HEAD_DIM
    return (r == c).astype(BF16)


def _suffix_matrix(n):
    r = lax.broadcasted_iota(jnp.int32, (n, n), 0)
    c = lax.broadcasted_iota(jnp.int32, (n, n), 1)
    return jnp.where(r > c, -1.0, 0.0).astype(BF16)


def _row_spec(tm, cols):
    return pl.BlockSpec((tm, cols), lambda i: (i, 0))


def _const_spec(shape):
    return pl.BlockSpec(shape, lambda *_: (0,) * len(shape))


def _inproj_body(x_ref, g_ref, w_ref, conv_ref, rwkv_ref, k_ref, v_ref, qb_ref, kb_ref, vb_ref):
    h = _rmsnorm(x_ref[...], g_ref[...]).astype(BF16)
    o0, o1, o2, o3 = CONV_COLS, CONV_COLS + RWKV_COLS, CONV_COLS + RWKV_COLS + ATT_DIM, IN_COLS - ATT_DIM
    conv_ref[...] = _dot(h, w_ref[:, 0:o0])
    rwkv_ref[...] = _dot(h, w_ref[:, o0:o1])
    q = _dot(h, w_ref[:, o1:o2])
    k = _dot(h, w_ref[:, o2:o3])
    v = _dot(h, w_ref[:, o3:IN_COLS])
    k_ref[...] = k
    v_ref[...] = v
    qb_ref[...] = (q * (HEAD_DIM ** -0.5)).astype(BF16)
    kb_ref[...] = k.astype(BF16)
    vb_ref[...] = v.astype(BF16)


def _inproj(x, g, w_b, tm):
    n = x.shape[0]
    widths = (CONV_COLS, RWKV_COLS, ATT_DIM, ATT_DIM, ATT_DIM, ATT_DIM, ATT_DIM)
    dtypes = (F32, F32, F32, F32, BF16, BF16, BF16)
    return pl.pallas_call(
        _inproj_body,
        grid=(n // tm,),
        in_specs=[_row_spec(tm, D_MODEL), _const_spec((1, D_MODEL)), _const_spec((D_MODEL, IN_COLS))],
        out_specs=[_row_spec(tm, c) for c in widths],
        out_shape=[jax.ShapeDtypeStruct((n, c), d) for c, d in zip(widths, dtypes)],
        compiler_params=_cparams(("arbitrary",)),
        name="inproj",
    )(x, g.reshape(1, -1), w_b)


def _conv_body(p_ref, buf_ref, w_ref, cb_ref, lg_ref, lb_ref, y_ref, nb_ref, ext_ref, *, tt):
    keep = CONV_WIDTH - 1
    lead = CONV_HALO - keep

    @pl.when(pl.program_id(1) == 0)
    def _():
        ext_ref[0:CONV_HALO, :] = jnp.zeros((CONV_HALO, CONV_CH), F32)
        ext_ref[lead:CONV_HALO, :] = buf_ref[0]

    p = p_ref[0]
    ext_ref[CONV_HALO:CONV_HALO + tt, :] = p[:, :CONV_CH] * _sigmoid(p[:, CONV_CH:])
    acc = jnp.zeros((tt, CONV_CH), F32)
    for j in range(CONV_WIDTH):
        acc = acc + w_ref[j:j + 1, :] * ext_ref[lead + j:lead + j + tt, :]
    y = acc + cb_ref[...]
    mu = jnp.mean(y, axis=-1, keepdims=True)
    d = y - mu
    var = jnp.mean(d * d, axis=-1, keepdims=True)
    y = d * lax.rsqrt(var + LN_EPS) * lg_ref[...] + lb_ref[...]
    y_ref[0] = (y * _sigmoid(y)).astype(y_ref.dtype)
    nb_ref[0] = ext_ref[tt + lead:tt + CONV_HALO, :]
    tail = ext_ref[tt:tt + CONV_HALO, :]
    ext_ref[0:CONV_HALO, :] = tail


def _conv_mixer(p, buf, w, cb, lg, lb, tt):
    b, t, _ = p.shape
    keep = CONV_WIDTH - 1
    return pl.pallas_call(
        functools.partial(_conv_body, tt=tt),
        grid=(b, t // tt),
        in_specs=[pl.BlockSpec((1, tt, CONV_COLS), lambda i, j: (i, j, 0)),
                  pl.BlockSpec((1, keep, CONV_CH), lambda i, j: (i, 0, 0)),
                  _const_spec((CONV_WIDTH, CONV_CH)), _const_spec((1, CONV_CH)),
                  _const_spec((1, CONV_CH)), _const_spec((1, CONV_CH))],
        out_specs=[pl.BlockSpec((1, tt, CONV_CH), lambda i, j: (i, j, 0)),
                   pl.BlockSpec((1, keep, CONV_CH), lambda i, j: (i, 0, 0))],
        out_shape=[jax.ShapeDtypeStruct((b, t, CONV_CH), BF16),
                   jax.ShapeDtypeStruct((b, keep, CONV_CH), F32)],
        scratch_shapes=[pltpu.VMEM((tt + CONV_HALO, CONV_CH), F32)],
        compiler_params=_cparams(("arbitrary", "arbitrary")),
        name="conv_mixer",
    )(p, buf, w, cb.reshape(1, -1), lg.reshape(1, -1), lb.reshape(1, -1))


def _rwkv_pre_body(p_ref, prev_ref, mu_ref, w0_ref, wup_ref, a0_ref, aup_ref, gup_ref, kk_ref, ka_ref, rk_ref,
                   *outs, heads_first):
    p = p_ref[...]
    pm = p + (prev_ref[...] - p) * mu_ref[...]
    d = RWKV_DIM
    r = pm[:, 0:d]
    k = pm[:, d:2 * d]
    v = pm[:, 2 * d:3 * d]
    xw = pm[:, 3 * d:3 * d + DECAY_LORA]
    xa = pm[:, 3 * d + DECAY_LORA:3 * d + DECAY_LORA + AAA_LORA]
    xg = pm[:, 3 * d + DECAY_LORA + AAA_LORA:RWKV_COLS]
    bd = _head_blockdiag(d)
    w_log = -_softplus(-(w0_ref[...] + _dot3(jnp.tanh(xw), wup_ref[...]))) - 0.5
    log_decay = -jnp.exp(w_log)
    a = _sigmoid(a0_ref[...] + _dot3(xa, aup_ref[...]))
    g = _dot3(_sigmoid(xg), gup_ref[...])
    kk = k * kk_ref[...]
    k = k * (1.0 + (a - 1.0) * ka_ref[...])
    kk = kk / jnp.maximum(jnp.sqrt(_dot_exact_rhs(kk * kk, bd)), 1e-12)
    steps = (r, log_decay, k, v, kk, kk * a)
    outs[-2][...] = g
    outs[-1][...] = _dot_exact_rhs(r * k * rk_ref[...], bd) * v
    if not heads_first:
        for ref, val in zip(outs[:6], steps):
            ref[...] = val
        return
    head = lambda h: slice(h * HEAD_DIM, (h + 1) * HEAD_DIM)
    for ref, val in zip(outs[:6], steps):
        for h in range(RWKV_HEADS):
            ref[h] = val[:, head(h)]
    for ref, val in zip(outs[6:9], (log_decay, kk, v)):
        val_t = val.T
        for h in range(RWKV_HEADS):
            ref[h] = val_t[head(h), :]


def _rwkv_pre(p, prev, mu, w0, w_up, a0, a_up, g_up, k_k, k_a, r_k, tm, seq_len=None):
    n = p.shape[0]
    vec = lambda a: a.reshape(1, -1)
    d = RWKV_DIM
    if seq_len is None:
        out_specs = [_row_spec(tm, d)] * 8
        out_shape = [jax.ShapeDtypeStruct((n, d), F32)] * 8
    else:
        per_seq = seq_len // tm
        chains = n // seq_len * RWKV_HEADS
        slab = pl.BlockSpec((RWKV_HEADS, tm, HEAD_DIM), lambda i: (i // per_seq, i % per_seq, 0))
        slab_t = pl.BlockSpec((RWKV_HEADS, HEAD_DIM, tm), lambda i: (i // per_seq, 0, i % per_seq))
        out_specs = [slab] * 6 + [slab_t] * 3 + [_row_spec(tm, d)] * 2
        out_shape = ([jax.ShapeDtypeStruct((chains, seq_len, HEAD_DIM), F32)] * 6
                     + [jax.ShapeDtypeStruct((chains, HEAD_DIM, seq_len), F32)] * 3
                     + [jax.ShapeDtypeStruct((n, d), F32)] * 2)
    return pl.pallas_call(
        functools.partial(_rwkv_pre_body, heads_first=seq_len is not None),
        grid=(n // tm,),
        in_specs=[_row_spec(tm, RWKV_COLS), _row_spec(tm, RWKV_COLS), _const_spec((1, RWKV_COLS)),
                  _const_spec((1, d)), _const_spec((DECAY_LORA, d)), _const_spec((1, d)),
                  _const_spec((AAA_LORA, d)), _const_spec((GATE_LORA, d)),
                  _const_spec((1, d)), _const_spec((1, d)), _const_spec((1, d))],
        out_specs=out_specs,
        out_shape=out_shape,
        compiler_params=_cparams(("arbitrary",)),
        name="rwkv_pre",
    )(p, prev, vec(mu), vec(w0), w_up, vec(a0), a_up, g_up, vec(k_k), vec(k_a), vec(r_k))


def _wkv_body(r_ref, w_ref, k_ref, v_ref, kk_ref, b_ref, s0_ref, y_ref, sT_ref, s_scr, *, nc, tc):
    ti = pl.program_id(1)

    @pl.when(ti == 0)
    def _():
        s_scr[...] = s0_ref[...]

    eye = (lax.broadcasted_iota(jnp.int32, (HEAD_DIM, HEAD_DIM), 0)
           == lax.broadcasted_iota(jnp.int32, (HEAD_DIM, HEAD_DIM), 1))

    def step(t, carry):
        for c in range(nc):
            row = lambda ref: ref[c, pl.ds(t, 1), :]
            s = s_scr[c]
            sa = jnp.sum(s * row(kk_ref), axis=1, keepdims=True)
            vcol = jnp.sum(jnp.where(eye, row(v_ref), 0.0), axis=1, keepdims=True)
            s = s * jnp.exp(row(w_ref)) - sa * row(b_ref) + vcol * row(k_ref)
            s_scr[c] = s
            ycol = jnp.sum(s * row(r_ref), axis=1, keepdims=True)
            y_ref[c, pl.ds(t, 1), :] = jnp.sum(jnp.where(eye, ycol, 0.0), axis=0, keepdims=True)
        return carry

    lax.fori_loop(0, tc, step, 0)

    @pl.when(ti == pl.num_programs(1) - 1)
    def _():
        sT_ref[...] = s_scr[...]


def _wkv(r, w, k, v, kk, b, s0, nc, tc):
    n, t, _ = r.shape
    seq = pl.BlockSpec((nc, tc, HEAD_DIM), lambda i, j: (i, j, 0))
    st = pl.BlockSpec((nc, HEAD_DIM, HEAD_DIM), lambda i, j: (i, 0, 0))
    return pl.pallas_call(
        functools.partial(_wkv_body, nc=nc, tc=tc),
        grid=(n // nc, t // tc),
        in_specs=[seq] * 6 + [st],
        out_specs=[seq, st],
        out_shape=[jax.ShapeDtypeStruct((n, t, HEAD_DIM), F32),
                   jax.ShapeDtypeStruct((n, HEAD_DIM, HEAD_DIM), F32)],
        scratch_shapes=[pltpu.VMEM((nc, HEAD_DIM, HEAD_DIM), F32)],
        compiler_params=_cparams(("arbitrary", "arbitrary")),
        name="wkv",
    )(r, w, k, v, kk, b, s0)


def _mm(a, b):
    return _dot3(a, b)


def _mm_nt(a, b):
    ah, al = _split2(a)
    bh, bl = _split2(b)
    return _dot_nt(ah, bh) + (_dot_nt(ah, bl) + _dot_nt(al, bh))


def _split3(x):
    p1 = x.astype(BF16)
    r1 = x - p1.astype(F32)
    p2 = r1.astype(BF16)
    p3 = (r1 - p2.astype(F32)).astype(BF16)
    return p1, p2, p3


def _wkv_chunk_body(r_ref, lw_ref, k_ref, v_ref, kk_ref, b_ref, lwt_ref, kkt_ref, vt_ref, s0_ref,
                    y_ref, sT_ref, s_scr, *, nc):
    ti = pl.program_id(1)
    c_len = WKV_CHUNK

    @pl.when(ti == 0)
    def _():
        s_scr[...] = s0_ref[...]

    row = lax.broadcasted_iota(jnp.int32, (c_len, c_len), 0)
    col = lax.broadcasted_iota(jnp.int32, (c_len, c_len), 1)
    strict = row > col
    incl = row >= col
    lower = incl.astype(BF16)
    upper = (row <= col).astype(BF16)
    eye = (row == col).astype(F32)

    n_sub = WKV_BLOCK // c_len
    units = [(j, c) for j in range(n_sub) for c in range(nc)]
    each = lambda f, *lists: [f(*args) for args in zip(*lists)]
    ts = lambda j: slice(j * c_len, (j + 1) * c_len)
    rows = lambda ref: [ref[c, ts(j), :] for j, c in units]
    cols = lambda ref: [ref[c, :, ts(j)] for j, c in units]
    r, lw, k, v, kk, b = (rows(ref) for ref in (r_ref, lw_ref, k_ref, v_ref, kk_ref, b_ref))
    lwt, kkt, vt = (cols(ref) for ref in (lwt_ref, kkt_ref, vt_ref))
    cum = each(lambda x: sum(_dot(lower, piece) for piece in _split3(x)), lw)
    cumt = each(lambda x: sum(_dot(piece, upper) for piece in _split3(x)), lwt)
    w_incl = each(jnp.exp, cum)
    w_last = each(lambda w: w[c_len - 1:c_len, :], w_incl)
    inv = each(lambda x: jnp.exp(-x), cum)
    qt = each(lambda x, cs, l: x * jnp.exp(cs - l), kk, cum, lw)
    qtt = each(lambda x, cs, l: x * jnp.exp(cs - l), kkt, cumt, lwt)
    rt = each(jnp.multiply, r, w_incl)
    kt = each(jnp.multiply, k, inv)
    bt = each(jnp.multiply, b, inv)
    a_qb = each(lambda x, y: jnp.where(strict, _mm_nt(x, y), 0.0), qt, bt)
    a_qk = each(lambda x, y: jnp.where(strict, _mm_nt(x, y), 0.0), qt, kt)
    a_rk = each(lambda x, y: jnp.where(incl, _mm_nt(x, y), 0.0), rt, kt)
    a_rb = each(lambda x, y: jnp.where(incl, _mm_nt(x, y), 0.0), rt, bt)
    tinv = each(lambda n: eye - n, a_qb)
    power = a_qb
    for _ in range(WKV_CHUNK_LOG2 - 1):
        power = each(_mm, power, power)
        tinv = each(lambda t, p: t + _mm(t, p), tinv, power)
    tq = each(_mm, tinv, qt)
    av = each(_mm, a_qk, v)
    av_t = each(_mm_nt, vt, a_qk)
    tq_t = each(_mm_nt, qtt, tinv)
    tav = each(_mm, tinv, av)
    tav_t = each(_mm_nt, av_t, tinv)
    ark_v = each(_mm, a_rk, v)
    rq = each(lambda x, a, y: x - _mm(a, y), rt, a_rb, tq)
    yv = each(lambda x, a, y: x - _mm(a, y), ark_v, a_rb, tav)
    gain = each(lambda x, y, w: _mm(x, y * w), vt, kt, w_last)
    bw = each(jnp.multiply, bt, w_last)
    for j in range(n_sub):
        ids = [j * nc + c for c in range(nc)]
        s = [s_scr[c] for c in range(nc)]
        sa_t = [_mm(s[c], tq_t[i]) + tav_t[i] for c, i in enumerate(ids)]
        ys = [_mm_nt(rq[i], s[c]) + yv[i] for c, i in enumerate(ids)]
        new = [s[c] * w_last[i] + gain[i] - _mm(sa_t[c], bw[i]) for c, i in enumerate(ids)]
        for c, i in enumerate(ids):
            y_ref[c, ts(j), :] = ys[c]
            s_scr[c] = new[c]

    @pl.when(ti == pl.num_programs(1) - 1)
    def _():
        sT_ref[...] = s_scr[...]


def _wkv_chunked(r, lw, k, v, kk, b, lwt, kkt, vt, s0, nc):
    n, t, _ = r.shape
    seq = pl.BlockSpec((nc, WKV_BLOCK, HEAD_DIM), lambda i, j: (i, j, 0))
    seq_t = pl.BlockSpec((nc, HEAD_DIM, WKV_BLOCK), lambda i, j: (i, 0, j))
    st = pl.BlockSpec((nc, HEAD_DIM, HEAD_DIM), lambda i, j: (i, 0, 0))
    return pl.pallas_call(
        functools.partial(_wkv_chunk_body, nc=nc),
        grid=(n // nc, t // WKV_BLOCK),
        in_specs=[seq] * 6 + [seq_t] * 3 + [st],
        out_specs=[seq, st],
        out_shape=[jax.ShapeDtypeStruct((n, t, HEAD_DIM), F32),
                   jax.ShapeDtypeStruct((n, HEAD_DIM, HEAD_DIM), F32)],
        scratch_shapes=[pltpu.VMEM((nc, HEAD_DIM, HEAD_DIM), F32)],
        compiler_params=_cparams(("arbitrary", "arbitrary")),
        name="wkv_chunked",
    )(r, lw, k, v, kk, b, lwt, kkt, vt, s0)


def _sb_block(q, kb, vb, bias, rest, umat, mask):
    z = _dot_nt(q, kb) + bias
    sp = _softplus(z)
    counted = sp if mask is None else jnp.where(mask, sp, 0.0)
    after = _dot(counted.astype(BF16), umat)
    a = jnp.exp((z - sp) + after + rest)
    if mask is not None:
        a = jnp.where(mask, a, 0.0)
    out = _dot(a.astype(BF16), vb)
    return out, after[:, 0:1] - counted[:, 0:1]


def _attn_prompt_body(q_ref, k_ref, v_ref, bias_ref, o_ref, *, tile):
    qi = pl.program_id(1)
    heads = range(ATT_HEADS)
    hs = lambda a, h: a[:, h * HEAD_DIM:(h + 1) * HEAD_DIM]
    q_all = q_ref[0]
    qs = [hs(q_all, h) for h in heads]
    biases = [bias_ref[h] for h in heads]
    umat = _suffix_matrix(tile)
    causal = (lax.broadcasted_iota(jnp.int32, (tile, tile), 1)
              < lax.broadcasted_iota(jnp.int32, (tile, tile), 0))

    def visit(j, accs, rests, mask):
        start = pl.multiple_of(j * tile, tile)
        kb = k_ref[0, pl.ds(start, tile), :]
        vb = v_ref[0, pl.ds(start, tile), :]
        zs = [_dot_nt(qs[h], hs(kb, h)) + biases[h] for h in heads]
        sps = [_softplus(z) for z in zs]
        counted = sps if mask is None else [jnp.where(mask, sp, 0.0) for sp in sps]
        afters = [_dot(sp.astype(BF16), umat) for sp in counted]
        probs = [jnp.exp((zs[h] - sps[h]) + afters[h] + rests[h]) for h in heads]
        if mask is not None:
            probs = [jnp.where(mask, a, 0.0) for a in probs]
        outs = [_dot(probs[h].astype(BF16), hs(vb, h)) for h in heads]
        new_accs = tuple(accs[h] + outs[h] for h in heads)
        new_rests = tuple(rests[h] + afters[h][:, 0:1] - counted[h][:, 0:1] for h in heads)
        return new_accs, new_rests

    accs = tuple(jnp.zeros((tile, HEAD_DIM), F32) for _ in heads)
    rests = tuple(jnp.zeros((tile, 1), F32) for _ in heads)
    accs, rests = visit(qi, accs, rests, causal)
    accs, rests = lax.fori_loop(0, qi, lambda jj, c: visit(qi - 1 - jj, c[0], c[1], None), (accs, rests))
    r = lax.broadcasted_iota(jnp.int32, (HEAD_DIM, ATT_DIM), 0)
    c = lax.broadcasted_iota(jnp.int32, (HEAD_DIM, ATT_DIM), 1)
    out = jnp.zeros((tile, ATT_DIM), F32)
    for h in heads:
        out = out + _dot(accs[h].astype(BF16), (c == r + h * HEAD_DIM).astype(BF16))
    o_ref[0] = out.astype(o_ref.dtype)


def _attn_prompt(qb, kb, vb, bias):
    b, t, d = qb.shape
    tile = ATT_TILE
    bias_b = jnp.broadcast_to(bias.astype(F32)[:, None, None], (ATT_HEADS, 1, tile))
    qspec = pl.BlockSpec((1, tile, d), lambda i, l: (i, l, 0))
    kvspec = pl.BlockSpec((1, t, d), lambda i, l: (i, 0, 0))
    return pl.pallas_call(
        functools.partial(_attn_prompt_body, tile=tile),
        grid=(b, t // tile),
        in_specs=[qspec, kvspec, kvspec, _const_spec((ATT_HEADS, 1, tile))],
        out_specs=qspec,
        out_shape=jax.ShapeDtypeStruct((b, t, d), BF16),
        compiler_params=_cparams(("arbitrary", "arbitrary")),
        name="attn_prompt",
    )(qb, kb, vb, bias_b)


def _attn_decode_body(pt_ref, q_ref, bias_ref, kn_ref, vn_ref, *refs, npg, nq):
    k_refs = refs[:npg]
    v_refs = refs[npg:2 * npg]
    o_ref, rest_ref, acc_ref = refs[2 * npg:]
    jb = pl.program_id(1)
    rows = ATT_HEADS * Q_PAD
    pages = range(npg)
    spread = lambda tot: jnp.broadcast_to(tot, (rows, PAGE_SIZE))
    q = q_ref[0]
    bias = bias_ref[...]
    umat = _suffix_matrix(PAGE_SIZE)

    @pl.when(jb == 0)
    def _():
        kidx = lax.broadcasted_iota(jnp.int32, (rows, PAGE_SIZE), 1)
        qidx = lax.broadcasted_iota(jnp.int32, (rows, PAGE_SIZE), 0) % Q_PAD
        out, tot = _sb_block(q, kn_ref[0].astype(BF16), vn_ref[0].astype(BF16), bias,
                             jnp.zeros((rows, 1), F32), umat, (kidx < qidx) & (kidx < nq))
        acc_ref[...] = out
        rest_ref[...] = spread(tot)

    zs = [_dot_nt(q, k_refs[s][0, 0].astype(BF16)) + bias for s in pages]
    sps = [_softplus(z) for z in zs]
    afters = [_dot(sp.astype(BF16), umat) for sp in sps]
    rest = rest_ref[...]
    probs = []
    for s in pages:
        probs.append(jnp.exp((zs[s] - sps[s]) + afters[s] + rest).astype(BF16))
        rest = rest + spread(afters[s][:, 0:1] - sps[s][:, 0:1])
    rest_ref[...] = rest
    out = _dot(probs[0], v_refs[0][0, 0].astype(BF16))
    for s in pages[1:]:
        out = out + _dot(probs[s], v_refs[s][0, 0].astype(BF16))
    acc_ref[...] += out

    @pl.when(jb == pl.num_programs(1) - 1)
    def _():
        head_of_row = lax.broadcasted_iota(jnp.int32, (rows, ATT_DIM), 0) // Q_PAD
        head_of_col = lax.broadcasted_iota(jnp.int32, (rows, ATT_DIM), 1) // HEAD_DIM
        kept = jnp.where(head_of_row == head_of_col, acc_ref[...], 0.0)
        out = kept[0:Q_PAD]
        for h in range(1, ATT_HEADS):
            out = out + kept[h * Q_PAD:(h + 1) * Q_PAD]
        o_ref[0] = out.astype(o_ref.dtype)


def _attn_decode(li, qb, k_new, v_new, cache_k, cache_v, page_table, bias):
    b, t, _ = qb.shape
    n_pages = page_table.shape[1]
    npg = PAGES_PER_STEP
    rows = ATT_HEADS * Q_PAD
    qpad = jnp.pad(qb, ((0, 0), (0, Q_PAD - t), (0, 0)))
    head_of_col = jnp.arange(ATT_DIM) // HEAD_DIM
    qbd = jnp.where(head_of_col[None, None, None, :] == jnp.arange(ATT_HEADS)[None, :, None, None],
                    qpad[:, None], jnp.zeros((), BF16)).reshape(b, rows, ATT_DIM)
    bias_b = jnp.broadcast_to(jnp.repeat(bias.astype(F32), Q_PAD)[:, None], (rows, PAGE_SIZE))
    kn = jnp.pad(k_new, ((0, 0), (0, PAGE_SIZE - t), (0, 0)))
    vn = jnp.pad(v_new, ((0, 0), (0, PAGE_SIZE - t), (0, 0)))

    def page_spec(s):
        return pl.BlockSpec((1, 1, PAGE_SIZE, ATT_DIM),
                            lambda i, j, pt: (li, pt[i * n_pages + n_pages - 1 - (j * npg + s)], 0, 0))

    per_b = lambda r, c: pl.BlockSpec((1, r, c), lambda i, j, pt: (i, 0, 0))
    grid_spec = pltpu.PrefetchScalarGridSpec(
        num_scalar_prefetch=1,
        grid=(b, n_pages // npg),
        in_specs=[per_b(rows, ATT_DIM), pl.BlockSpec((rows, PAGE_SIZE), lambda i, j, pt: (0, 0)),
                  per_b(PAGE_SIZE, ATT_DIM), per_b(PAGE_SIZE, ATT_DIM)]
                 + [page_spec(s) for s in range(npg)] * 2,
        out_specs=per_b(Q_PAD, ATT_DIM),
        scratch_shapes=[pltpu.VMEM((rows, PAGE_SIZE), F32), pltpu.VMEM((rows, ATT_DIM), F32)],
    )
    out = pl.pallas_call(
        functools.partial(_attn_decode_body, npg=npg, nq=t),
        grid_spec=grid_spec,
        out_shape=jax.ShapeDtypeStruct((b, Q_PAD, ATT_DIM), BF16),
        compiler_params=_cparams(("arbitrary", "arbitrary")),
        name="attn_decode",
    )(page_table.reshape(-1), qbd, bias_b, kn, vn, *([cache_k] * npg), *([cache_v] * npg))
    return out[:, :t]


def _outproj_body(x_ref, yc_ref, yw_ref, bonus_ref, g_ref, ya_ref, lng_ref, lnb_ref, w_ref, o_ref):
    bd = _head_blockdiag(RWKV_DIM)
    y = yw_ref[...]
    inv = 1.0 / HEAD_DIM
    d = y - _dot_exact_rhs(y, bd) * inv
    var = _dot_exact_rhs(d * d, bd) * inv
    yn = d * lax.rsqrt(var + GN_EPS) * lng_ref[...] + lnb_ref[...]
    yr = ((yn + bonus_ref[...]) * g_ref[...]).astype(BF16)
    c0, c1 = CONV_CH, CONV_CH + RWKV_DIM
    o_ref[...] = (x_ref[...] + _dot(yc_ref[...], w_ref[0:c0, :]) + _dot(yr, w_ref[c0:c1, :])
                  + _dot(ya_ref[...], w_ref[c1:D_MODEL, :]))


def _outproj(x, yc, yw, bonus, g, ya, ln_g, ln_b, w_b, tm):
    n = x.shape[0]
    return pl.pallas_call(
        _outproj_body,
        grid=(n // tm,),
        in_specs=[_row_spec(tm, D_MODEL), _row_spec(tm, CONV_CH), _row_spec(tm, RWKV_DIM), _row_spec(tm, RWKV_DIM),
                  _row_spec(tm, RWKV_DIM), _row_spec(tm, ATT_DIM), _const_spec((1, RWKV_DIM)),
                  _const_spec((1, RWKV_DIM)), _const_spec((D_MODEL, D_MODEL))],
        out_specs=_row_spec(tm, D_MODEL),
        out_shape=jax.ShapeDtypeStruct((n, D_MODEL), F32),
        compiler_params=_cparams(("arbitrary",)),
        name="outproj",
    )(x, yc, yw, bonus, g, ya, ln_g.reshape(1, -1), ln_b.reshape(1, -1), w_b)


SUBLANES = 8
KEY_GROUPS = N_KEYS // SUBLANES


def _oddeven_merge(lo, hi, r):
    step = r * 2
    if step < hi - lo:
        yield from _oddeven_merge(lo, hi, step)
        yield from _oddeven_merge(lo + r, hi, step)
        yield from [(i, i + r) for i in range(lo + r, hi - r, step)]
    else:
        yield (lo, lo + r)


def _oddeven_sort(lo, hi):
    if hi - lo >= 1:
        mid = lo + (hi - lo) // 2
        yield from _oddeven_sort(lo, mid)
        yield from _oddeven_sort(mid + 1, hi)
        yield from _oddeven_merge(lo, hi, 1)


SORT_NET = tuple(_oddeven_sort(0, PEER_TOPK - 1))
BITONIC_NET = tuple((i, i + s) for s in (8, 4, 2, 1) for i in range(PEER_TOPK) if not i & s)
SUM_SLOTS = tuple((a, b) for a in range(PEER_TOPK) for b in range(PEER_TOPK) if (a + 1) * (b + 1) <= PEER_TOPK)


def _exchange(vals, i, j):
    vals[i], vals[j] = jnp.maximum(vals[i], vals[j]), jnp.minimum(vals[i], vals[j])


def _sublane_max(x):
    for shift in (4, 2, 1):
        x = jnp.maximum(x, pltpu.roll(x, shift, 0))
    return x


def _top16_sorted(groups):
    vals = list(groups)
    for i, j in SORT_NET:
        _exchange(vals, i, j)
    for shift in (4, 2, 1):
        moved = [pltpu.roll(v, shift, 0) for v in vals]
        vals = [jnp.maximum(vals[i], moved[PEER_TOPK - 1 - i]) for i in range(PEER_TOPK)]
        for i, j in BITONIC_NET:
            _exchange(vals, i, j)
    return vals


def _peer_route_body(x_ref, g_ref, wq_ref, sk_ref, h_out, rank_out, e2_out, cnt_out, f1_out):
    tm = x_ref.shape[0]
    kk = PEER_TOPK
    hb = _rmsnorm(x_ref[...], g_ref[...]).astype(BF16)
    h_out[...] = hb
    sub = lax.broadcasted_iota(jnp.int32, (SUBLANES, tm), 0)
    split = lambda s: [s[i * SUBLANES:(i + 1) * SUBLANES, :] for i in range(KEY_GROUPS)]
    join = lambda parts: jnp.concatenate(parts, axis=0)
    neg_inf = jnp.full((SUBLANES, tm), -jnp.inf, F32)
    half = PEER_DK // 2
    for h in range(PEER_HEADS):
        q = _dot(hb, wq_ref[:, h * PEER_DK:(h + 1) * PEER_DK])
        s1 = split(_dot_nt(sk_ref[2 * h], q[:, :half].astype(BF16)))
        s2 = split(_dot_nt(sk_ref[2 * h + 1], q[:, half:].astype(BF16)))
        v1 = _top16_sorted(s1)
        v2 = _top16_sorted(s2)
        sums = {ab: v1[ab[0]] + v2[ab[1]] for ab in SUM_SLOTS}
        packs = []
        for j in range(0, len(SUM_SLOTS), SUBLANES):
            pack = neg_inf
            for s, ab in enumerate(SUM_SLOTS[j:j + SUBLANES]):
                pack = jnp.where(sub == s, sums[ab], pack)
            packs.append(pack)
        best = []
        for _ in range(kk):
            m = packs[0]
            for p in packs[1:]:
                m = jnp.maximum(m, p)
            m = _sublane_max(m)
            best.append(m)
            packs = [jnp.where(p == m, -jnp.inf, p) for p in packs]
        z = jnp.ones((SUBLANES, tm), F32)
        for i in range(1, kk):
            z = z + jnp.exp(best[i] - best[0])
        inv_z = 1.0 / z
        cnt = [jnp.zeros((SUBLANES, tm), F32) for _ in range(kk)]
        for a, b in SUM_SLOTS:
            cnt[a] = cnt[a] + jnp.where(sums[(a, b)] >= best[kk - 1], 1.0, 0.0)
        cnt_full, f1_full, rank_full, e2_full = [], [], [], []
        for g in range(KEY_GROUPS):
            c = jnp.zeros((SUBLANES, tm), F32)
            for a in range(kk):
                c = jnp.where(s1[g] == v1[a], cnt[a], c)
            cnt_full.append(c)
            f1_full.append(jnp.where(s1[g] >= v1[kk - 1], jnp.exp(s1[g] - v1[0]), 0.0) * inv_z)
            rk = jnp.full((SUBLANES, tm), float(kk), F32)
            for b in reversed(range(kk)):
                rk = jnp.where(s2[g] == v2[b], float(b), rk)
            rank_full.append(rk)
            e2_full.append(jnp.where(s2[g] >= v2[kk - 1], jnp.exp(s2[g] - v2[0]), 0.0))
        cnt_out[h] = join(cnt_full)
        f1_out[h] = join(f1_full)
        rank_out[h] = join(rank_full).astype(BF16)
        e2_out[h] = join(e2_full).astype(BF16)


def _peer_route(x, g, wq_b, sk_b, tm):
    n = x.shape[0]
    keyed = pl.BlockSpec((PEER_HEADS, N_KEYS, tm), lambda i: (0, 0, i))
    keyed_shape = lambda dt: jax.ShapeDtypeStruct((PEER_HEADS, N_KEYS, n), dt)
    return pl.pallas_call(
        _peer_route_body,
        grid=(n // tm,),
        in_specs=[_row_spec(tm, D_MODEL), _const_spec((1, D_MODEL)), _const_spec((D_MODEL, PEER_HEADS * PEER_DK)),
                  _const_spec((2 * PEER_HEADS, N_KEYS, PEER_DK // 2))],
        out_specs=[_row_spec(tm, D_MODEL), keyed, keyed, keyed, keyed],
        out_shape=[jax.ShapeDtypeStruct((n, D_MODEL), BF16), keyed_shape(BF16), keyed_shape(BF16),
                   keyed_shape(F32), keyed_shape(F32)],
        compiler_params=_cparams(("arbitrary",)),
        name="peer_route",
    )(x, g.reshape(1, -1), wq_b, sk_b)


GATE_ROWS = 16


def _gelu_tanh(x):
    return 0.5 * x * (1.0 + jnp.tanh(0.7978845608028654 * (x + 0.044715 * (x * x * x))))


def _peer_expert_body(x_ref, h_ref, rank_ref, e2_ref, cnt_ref, f1_ref, u_ref, vt_ref, o_ref, acc_ref):
    ec = pl.program_id(1)

    @pl.when(ec == 0)
    def _():
        acc_ref[...] = jnp.zeros(acc_ref.shape, F32)

    tm = h_ref.shape[0]
    n_grp = 2 if tm % 512 == 0 else 1
    grp = tm // n_grp
    toks = [slice(g * grp, (g + 1) * grp) for g in range(n_grp)]
    hids = [_dot_nt(u_ref[...], h_ref[tk, :]) for tk in toks]
    heads = range(PEER_HEADS)
    for g, tk in enumerate(toks):
        weights = []
        for r in range(PEER_ROWS):
            bcast = lambda ref, h: jnp.broadcast_to(ref[h, 0, r:r + 1, tk], (GATE_ROWS, grp)).astype(BF16)
            cnts = [bcast(cnt_ref, h) for h in heads]
            f1s = [bcast(f1_ref, h) for h in heads]
            for j in range(N_KEYS // GATE_ROWS):
                keys = slice(j * GATE_ROWS, (j + 1) * GATE_ROWS)
                gate = jnp.zeros((GATE_ROWS, grp), BF16)
                for h in heads:
                    e2 = e2_ref[h, keys, tk]
                    gate = gate + jnp.where(rank_ref[h, keys, tk] < cnts[h], e2, jnp.zeros_like(e2)) * f1s[h]
                rows = slice(r * N_KEYS + j * GATE_ROWS, r * N_KEYS + (j + 1) * GATE_ROWS)
                weights.append(gate * _gelu_tanh(hids[g][rows]).astype(BF16))
        acc_ref[:, tk] += _dot(vt_ref[...], jnp.concatenate(weights, axis=0))

    @pl.when(ec == pl.num_programs(1) - 1)
    def _():
        o_ref[...] = x_ref[...] + acc_ref[...].T


def _peer_expert(x, hb, rank2, e2, cnt, f1, u_b, vt_b, tm):
    n = x.shape[0]
    te = PEER_ROWS * N_KEYS
    ne = u_b.shape[0] // te
    chunked = lambda a: a.reshape(PEER_HEADS, ne, PEER_ROWS, n)
    tok = lambda c: pl.BlockSpec((tm, c), lambda i, e: (i, 0))
    keyed = pl.BlockSpec((PEER_HEADS, N_KEYS, tm), lambda i, e: (0, 0, i))
    row = pl.BlockSpec((PEER_HEADS, 1, PEER_ROWS, tm), lambda i, e: (0, e, 0, i))
    return pl.pallas_call(
        _peer_expert_body,
        grid=(n // tm, ne),
        in_specs=[tok(D_MODEL), tok(D_MODEL), keyed, keyed, row, row,
                  pl.BlockSpec((te, D_MODEL), lambda i, e: (e, 0)),
                  pl.BlockSpec((D_MODEL, te), lambda i, e: (0, e))],
        out_specs=tok(D_MODEL),
        out_shape=jax.ShapeDtypeStruct((n, D_MODEL), F32),
        scratch_shapes=[pltpu.VMEM((D_MODEL, tm), F32)],
        compiler_params=_cparams(("arbitrary", "arbitrary")),
        name="peer_expert",
    )(x, hb, rank2, e2, chunked(cnt), chunked(f1), u_b, vt_b)


def _final_norm_body(x_ref, g_ref, o_ref):
    o_ref[...] = _rmsnorm(x_ref[...], g_ref[...])


def _final_norm(x, g, tm):
    n = x.shape[0]
    return pl.pallas_call(
        _final_norm_body,
        grid=(n // tm,),
        in_specs=[_row_spec(tm, D_MODEL), _const_spec((1, D_MODEL))],
        out_specs=_row_spec(tm, D_MODEL),
        out_shape=jax.ShapeDtypeStruct((n, D_MODEL), F32),
        compiler_params=_cparams(("arbitrary",)),
        name="final_norm",
    )(x, g.reshape(1, -1))


def _heads_first(a, b, t):
    h = a.shape[1] // HEAD_DIM
    return a.reshape(b, t, h, HEAD_DIM).transpose(0, 2, 1, 3).reshape(b * h, t, HEAD_DIM)


def _tokens_first(a, b, t):
    h = a.shape[0] // b
    return a.reshape(b, h, t, HEAD_DIM).transpose(0, 2, 1, 3).reshape(b * t, h * HEAD_DIM)


def _layer(x, b, t, conv_buf, shift_prev, wkv0, attend, lw, tm, conv_tile, wkv_chains, wkv_tile):
    p_conv, p_rwkv, k, v, qb, kb, vb = _inproj(x, lw["norm1_g"], lw["w_in"], tm)

    y_conv, conv_new = _conv_mixer(p_conv.reshape(b, t, CONV_COLS), conv_buf, lw["conv_w"], lw["conv_b"],
                                   lw["conv_ln_g"], lw["conv_ln_b"], conv_tile)

    p3 = p_rwkv.reshape(b, t, RWKV_COLS)
    prev = jnp.concatenate([shift_prev[:, None, :], p3[:, :-1]], axis=1).reshape(b * t, RWKV_COLS)
    pre = functools.partial(_rwkv_pre, p_rwkv, prev, lw["rwkv_mu"], lw["rwkv_w0"], lw["rwkv_w_up"], lw["rwkv_a0"],
                            lw["rwkv_a_up"], lw["rwkv_g_up"], lw["rwkv_k_k"], lw["rwkv_k_a"], lw["rwkv_r_k"], tm)
    s0 = wkv0.reshape(b * RWKV_HEADS, HEAD_DIM, HEAD_DIM)
    if t % WKV_BLOCK == 0 and t % tm == 0:
        *steps, g, bonus = pre(seq_len=t)
        y_wkv, wkv_new = _wkv_chunked(*steps, s0, wkv_chains)
    else:
        *steps, g, bonus = pre()
        y_wkv, wkv_new = _wkv(*(_heads_first(a, b, t) for a in steps), s0, wkv_chains, wkv_tile)
    y_wkv = _tokens_first(y_wkv, b, t)

    y_att = attend(qb, k, v, kb, vb)

    x = _outproj(x, y_conv.reshape(b * t, CONV_CH), y_wkv, bonus, g, y_att, lw["rwkv_ln_g"], lw["rwkv_ln_b"],
                 lw["w_out"], tm)
    routed = _peer_route(x, lw["norm2_g"], lw["peer_w_query"], lw["peer_sub_keys"], min(tm, 256))
    x = _peer_expert(x, *routed, lw["peer_u"], lw["peer_vt"], tm)
    states = (k.reshape(b, t, ATT_HEADS, HEAD_DIM), v.reshape(b, t, ATT_HEADS, HEAD_DIM), conv_new,
              p3[:, -1], wkv_new.reshape(b, RWKV_HEADS, HEAD_DIM, HEAD_DIM))
    return x, states


def kernel(x_prompt, x_sample, cache_k, cache_v, state_conv, state_shift, state_wkv, page_table, norm1_g, w_in, conv_w, conv_b, conv_ln_g, conv_ln_b, rwkv_mu, rwkv_w0, rwkv_w_up, rwkv_a0, rwkv_a_up, rwkv_g_up, rwkv_k_k, rwkv_k_a, rwkv_r_k, rwkv_ln_g, rwkv_ln_b, att_bias, w_out, norm2_g, peer_w_query, peer_sub_keys, peer_u, peer_v, final_g):
    depth = w_in.shape[0]
    bp, tp, _ = x_prompt.shape
    bs, ts, _ = x_sample.shape
    n_pool = cache_k.shape[1]
    ck = cache_k.reshape(depth, n_pool, PAGE_SIZE, ATT_DIM)
    cv = cache_v.reshape(depth, n_pool, PAGE_SIZE, ATT_DIM)
    xp = x_prompt.reshape(bp * tp, D_MODEL)
    xs = x_sample.reshape(bs * ts, D_MODEL)
    st_p, st_s = [], []
    for li in range(depth):
        lw = dict(
            norm1_g=norm1_g[li], w_in=w_in[li].astype(BF16), conv_w=conv_w[li], conv_b=conv_b[li],
            conv_ln_g=conv_ln_g[li], conv_ln_b=conv_ln_b[li], rwkv_mu=rwkv_mu[li], rwkv_w0=rwkv_w0[li],
            rwkv_w_up=rwkv_w_up[li], rwkv_a0=rwkv_a0[li], rwkv_a_up=rwkv_a_up[li], rwkv_g_up=rwkv_g_up[li],
            rwkv_k_k=rwkv_k_k[li], rwkv_k_a=rwkv_k_a[li], rwkv_r_k=rwkv_r_k[li].reshape(-1),
            rwkv_ln_g=rwkv_ln_g[li], rwkv_ln_b=rwkv_ln_b[li], w_out=w_out[li].astype(BF16), norm2_g=norm2_g[li],
            peer_w_query=peer_w_query[li].astype(BF16),
            peer_sub_keys=peer_sub_keys[li].astype(BF16).reshape(2 * PEER_HEADS, N_KEYS, PEER_DK // 2),
            peer_u=peer_u[li].astype(BF16), peer_vt=peer_v[li].astype(BF16).T)
        bias = att_bias[li]

        def attend_prompt(qb, k, v, kb, vb):
            sq = lambda a: a.reshape(bp, tp, ATT_DIM)
            return _attn_prompt(sq(qb), sq(kb), sq(vb), bias).reshape(bp * tp, ATT_DIM)

        def attend_sample(qb, k, v, kb, vb):
            o = _attn_decode(li, qb.reshape(bs, ts, ATT_DIM), k.reshape(bs, ts, ATT_DIM),
                             v.reshape(bs, ts, ATT_DIM), ck, cv, page_table, bias)
            return o.reshape(bs * ts, ATT_DIM)

        xp, new_p = _layer(xp, bp, tp, jnp.zeros((bp, CONV_WIDTH - 1, CONV_CH), F32),
                           jnp.zeros((bp, RWKV_COLS), F32), jnp.zeros((bp, RWKV_HEADS, HEAD_DIM, HEAD_DIM), F32),
                           attend_prompt, lw, tm=512, conv_tile=512, wkv_chains=4, wkv_tile=256)
        xs, new_s = _layer(xs, bs, ts, state_conv[li], state_shift[li], state_wkv[li],
                           attend_sample, lw, tm=bs * ts, conv_tile=ts, wkv_chains=bp * RWKV_HEADS, wkv_tile=ts)
        st_p.append(new_p)
        st_s.append(new_s)
    y_prompt = _final_norm(xp, final_g, 512).reshape(bp, tp, D_MODEL)
    y_sample = _final_norm(xs, final_g, bs * ts).reshape(bs, ts, D_MODEL)
    stack = lambda st, j: jnp.stack([s[j] for s in st], axis=0)
    return (y_prompt, y_sample) + tuple(stack(st_p, j) for j in range(5)) + tuple(stack(st_s, j) for j in range(5))
```

```python
import functools

import jax
import jax.numpy as jnp
from jax import lax
from jax.experimental import pallas as pl
from jax.experimental.pallas import tpu as pltpu

F32 = jnp.float32
BF16 = jnp.bfloat16

D_MODEL = 1024
HEAD_DIM = 64
CONV_CH = 256
CONV_WIDTH = 31
CONV_HALO = 32
RWKV_HEADS = 6
RWKV_DIM = 384
ATT_HEADS = 6
ATT_DIM = 384
DECAY_LORA = 64
AAA_LORA = 64
GATE_LORA = 128
CONV_COLS = 512
RWKV_COLS = 1408
IN_COLS = 3072
PAGE_SIZE = 128
PEER_HEADS = 8
PEER_DK = 256
N_KEYS = 128
PEER_TOPK = 16
RMS_EPS = 1e-5
LN_EPS = 1e-5
GN_EPS = 64e-5

ATT_TILE = 256
PAGES_PER_STEP = 8
Q_PAD = 8
PEER_ROWS = 16
WKV_CHUNK_LOG2 = 6
WKV_CHUNK = 1 << WKV_CHUNK_LOG2
WKV_BLOCK = 128
VMEM_LIMIT = 56 * 1024 * 1024


def _cparams(sem):
    return pltpu.CompilerParams(dimension_semantics=sem, vmem_limit_bytes=VMEM_LIMIT)


def _sigmoid(x):
    return 1.0 / (1.0 + jnp.exp(-x))


def _softplus(x):
    return jnp.maximum(x, 0.0) + jnp.log(1.0 + jnp.exp(-jnp.abs(x)))


def _dot(a, b):
    return jnp.dot(a, b, preferred_element_type=F32)


def _dot_nt(a, b):
    return lax.dot_general(a, b, (((1,), (1,)), ((), ())), preferred_element_type=F32)


def _split2(x):
    hi = x.astype(BF16)
    lo = (x - hi.astype(F32)).astype(BF16)
    return hi, lo


def _dot3(a, b):
    ah, al = _split2(a)
    bh, bl = _split2(b)
    return _dot(ah, bh) + (_dot(ah, bl) + _dot(al, bh))


def _dot_exact_rhs(x, m):
    hi, lo = _split2(x)
    return _dot(hi, m) + _dot(lo, m)


def _rmsnorm(x, g):
    return x * lax.rsqrt(jnp.mean(x * x, axis=-1, keepdims=True) + RMS_EPS) * g


def _head_blockdiag(n):
    r = lax.broadcasted_iota(jnp.int32, (n, n), 0) // HEAD_DIM
    c = lax.broadcasted_iota(jnp.int32, (n, n), 1) // HEAD_DIM
    return (r == c).astype(BF16)


def _suffix_matrix(n):
    r = lax.broadcasted_iota(jnp.int32, (n, n), 0)
    c = lax.broadcasted_iota(jnp.int32, (n, n), 1)
    return jnp.where(r > c, -1.0, 0.0).astype(BF16)


def _row_spec(tm, cols):
    return pl.BlockSpec((tm, cols), lambda i: (i, 0))


def _const_spec(shape):
    return pl.BlockSpec(shape, lambda *_: (0,) * len(shape))


def _inproj_body(x_ref, g_ref, w_ref, conv_ref, rwkv_ref, k_ref, v_ref, qb_ref, kb_ref, vb_ref):
    h = _rmsnorm(x_ref[...], g_ref[...]).astype(BF16)
    o0, o1, o2, o3 = CONV_COLS, CONV_COLS + RWKV_COLS, CONV_COLS + RWKV_COLS + ATT_DIM, IN_COLS - ATT_DIM
    conv_ref[...] = _dot(h, w_ref[:, 0:o0])
    rwkv_ref[...] = _dot(h, w_ref[:, o0:o1])
    q = _dot(h, w_ref[:, o1:o2])
    k = _dot(h, w_ref[:, o2:o3])
    v = _dot(h, w_ref[:, o3:IN_COLS])
    k_ref[...] = k
    v_ref[...] = v
    qb_ref[...] = (q * (HEAD_DIM ** -0.5)).astype(BF16)
    kb_ref[...] = k.astype(BF16)
    vb_ref[...] = v.astype(BF16)


def _inproj(x, g, w_b, tm):
    n = x.shape[0]
    widths = (CONV_COLS, RWKV_COLS, ATT_DIM, ATT_DIM, ATT_DIM, ATT_DIM, ATT_DIM)
    dtypes = (F32, F32, F32, F32, BF16, BF16, BF16)
    return pl.pallas_call(
        _inproj_body,
        grid=(n // tm,),
        in_specs=[_row_spec(tm, D_MODEL), _const_spec((1, D_MODEL)), _const_spec((D_MODEL, IN_COLS))],
        out_specs=[_row_spec(tm, c) for c in widths],
        out_shape=[jax.ShapeDtypeStruct((n, c), d) for c, d in zip(widths, dtypes)],
        compiler_params=_cparams(("arbitrary",)),
        name="inproj",
    )(x, g.reshape(1, -1), w_b)


def _conv_body(p_ref, buf_ref, w_ref, cb_ref, lg_ref, lb_ref, y_ref, nb_ref, ext_ref, *, tt):
    keep = CONV_WIDTH - 1
    lead = CONV_HALO - keep

    @pl.when(pl.program_id(1) == 0)
    def _():
        ext_ref[0:CONV_HALO, :] = jnp.zeros((CONV_HALO, CONV_CH), F32)
        ext_ref[lead:CONV_HALO, :] = buf_ref[0]

    p = p_ref[0]
    ext_ref[CONV_HALO:CONV_HALO + tt, :] = p[:, :CONV_CH] * _sigmoid(p[:, CONV_CH:])
    acc = jnp.zeros((tt, CONV_CH), F32)
    for j in range(CONV_WIDTH):
        acc = acc + w_ref[j:j + 1, :] * ext_ref[lead + j:lead + j + tt, :]
    y = acc + cb_ref[...]
    mu = jnp.mean(y, axis=-1, keepdims=True)
    d = y - mu
    var = jnp.mean(d * d, axis=-1, keepdims=True)
    y = d * lax.rsqrt(var + LN_EPS) * lg_ref[...] + lb_ref[...]
    y_ref[0] = (y * _sigmoid(y)).astype(y_ref.dtype)
    nb_ref[0] = ext_ref[tt + lead:tt + CONV_HALO, :]
    tail = ext_ref[tt:tt + CONV_HALO, :]
    ext_ref[0:CONV_HALO, :] = tail


def _conv_mixer(p, buf, w, cb, lg, lb, tt):
    b, t, _ = p.shape
    keep = CONV_WIDTH - 1
    return pl.pallas_call(
        functools.partial(_conv_body, tt=tt),
        grid=(b, t // tt),
        in_specs=[pl.BlockSpec((1, tt, CONV_COLS), lambda i, j: (i, j, 0)),
                  pl.BlockSpec((1, keep, CONV_CH), lambda i, j: (i, 0, 0)),
                  _const_spec((CONV_WIDTH, CONV_CH)), _const_spec((1, CONV_CH)),
                  _const_spec((1, CONV_CH)), _const_spec((1, CONV_CH))],
        out_specs=[pl.BlockSpec((1, tt, CONV_CH), lambda i, j: (i, j, 0)),
                   pl.BlockSpec((1, keep, CONV_CH), lambda i, j: (i, 0, 0))],
        out_shape=[jax.ShapeDtypeStruct((b, t, CONV_CH), BF16),
                   jax.ShapeDtypeStruct((b, keep, CONV_CH), F32)],
        scratch_shapes=[pltpu.VMEM((tt + CONV_HALO, CONV_CH), F32)],
        compiler_params=_cparams(("arbitrary", "arbitrary")),
        name="conv_mixer",
    )(p, buf, w, cb.reshape(1, -1), lg.reshape(1, -1), lb.reshape(1, -1))


def _rwkv_pre_body(p_ref, prev_ref, mu_ref, w0_ref, wup_ref, a0_ref, aup_ref, gup_ref, kk_ref, ka_ref, rk_ref,
                   *outs, tiles_per_seq):
    heads_first = tiles_per_seq > 0
    p = p_ref[...]
    if heads_first:
        *outs, carry_ref = outs

        @pl.when(pl.program_id(0) % tiles_per_seq == 0)
        def _():
            carry_ref[...] = prev_ref[0]

        first_row = lax.broadcasted_iota(jnp.int32, p.shape, 0) == 0
        prev = jnp.where(first_row, carry_ref[...], pltpu.roll(p, 1, 0))
        carry_ref[...] = p[p.shape[0] - 1:p.shape[0], :]
    else:
        prev = prev_ref[...]
    pm = p + (prev - p) * mu_ref[...]
    d = RWKV_DIM
    r = pm[:, 0:d]
    k = pm[:, d:2 * d]
    v = pm[:, 2 * d:3 * d]
    xw = pm[:, 3 * d:3 * d + DECAY_LORA]
    xa = pm[:, 3 * d + DECAY_LORA:3 * d + DECAY_LORA + AAA_LORA]
    xg = pm[:, 3 * d + DECAY_LORA + AAA_LORA:RWKV_COLS]
    bd = _head_blockdiag(d)
    w_log = -_softplus(-(w0_ref[...] + _dot3(jnp.tanh(xw), wup_ref[...]))) - 0.5
    log_decay = -jnp.exp(w_log)
    a = _sigmoid(a0_ref[...] + _dot3(xa, aup_ref[...]))
    g = _dot3(_sigmoid(xg), gup_ref[...])
    kk = k * kk_ref[...]
    k = k * (1.0 + (a - 1.0) * ka_ref[...])
    kk = kk / jnp.maximum(jnp.sqrt(_dot_exact_rhs(kk * kk, bd)), 1e-12)
    steps = (r, log_decay, k, v, kk, kk * a)
    outs[-2][...] = g
    outs[-1][...] = _dot_exact_rhs(r * k * rk_ref[...], bd) * v
    if not heads_first:
        for ref, val in zip(outs[:6], steps):
            ref[...] = val
        return
    head = lambda h: slice(h * HEAD_DIM, (h + 1) * HEAD_DIM)
    for ref, val in zip(outs[:6], steps):
        for h in range(RWKV_HEADS):
            ref[h] = val[:, head(h)]
    for ref, val in zip(outs[6:9], (log_decay, kk, v)):
        val_t = val.T
        for h in range(RWKV_HEADS):
            ref[h] = val_t[head(h), :]


def _rwkv_pre(p, prev, mu, w0, w_up, a0, a_up, g_up, k_k, k_a, r_k, tm, seq_len=None):
    n = p.shape[0]
    vec = lambda a: a.reshape(1, -1)
    d = RWKV_DIM
    if seq_len is None:
        per_seq = 0
        prev_spec = _row_spec(tm, RWKV_COLS)
        scratch = []
        out_specs = [_row_spec(tm, d)] * 8
        out_shape = [jax.ShapeDtypeStruct((n, d), F32)] * 8
    else:
        per_seq = seq_len // tm
        prev = prev.reshape(-1, 1, RWKV_COLS)
        prev_spec = pl.BlockSpec((1, 1, RWKV_COLS), lambda i: (i // per_seq, 0, 0))
        scratch = [pltpu.VMEM((1, RWKV_COLS), F32)]
        chains = n // seq_len * RWKV_HEADS
        slab = pl.BlockSpec((RWKV_HEADS, tm, HEAD_DIM), lambda i: (i // per_seq, i % per_seq, 0))
        slab_t = pl.BlockSpec((RWKV_HEADS, HEAD_DIM, tm), lambda i: (i // per_seq, 0, i % per_seq))
        out_specs = [slab] * 6 + [slab_t] * 3 + [_row_spec(tm, d)] * 2
        out_shape = ([jax.ShapeDtypeStruct((chains, seq_len, HEAD_DIM), F32)] * 6
                     + [jax.ShapeDtypeStruct((chains, HEAD_DIM, seq_len), F32)] * 3
                     + [jax.ShapeDtypeStruct((n, d), F32)] * 2)
    return pl.pallas_call(
        functools.partial(_rwkv_pre_body, tiles_per_seq=per_seq),
        grid=(n // tm,),
        in_specs=[_row_spec(tm, RWKV_COLS), prev_spec, _const_spec((1, RWKV_COLS)),
                  _const_spec((1, d)), _const_spec((DECAY_LORA, d)), _const_spec((1, d)),
                  _const_spec((AAA_LORA, d)), _const_spec((GATE_LORA, d)),
                  _const_spec((1, d)), _const_spec((1, d)), _const_spec((1, d))],
        out_specs=out_specs,
        out_shape=out_shape,
        scratch_shapes=scratch,
        compiler_params=_cparams(("arbitrary",)),
        name="rwkv_pre",
    )(p, prev, vec(mu), vec(w0), w_up, vec(a0), a_up, g_up, vec(k_k), vec(k_a), vec(r_k))


def _wkv_body(r_ref, w_ref, k_ref, v_ref, kk_ref, b_ref, s0_ref, y_ref, sT_ref, s_scr, *, nc, tc):
    ti = pl.program_id(1)

    @pl.when(ti == 0)
    def _():
        s_scr[...] = s0_ref[...]

    eye = (lax.broadcasted_iota(jnp.int32, (HEAD_DIM, HEAD_DIM), 0)
           == lax.broadcasted_iota(jnp.int32, (HEAD_DIM, HEAD_DIM), 1))

    def step(t, carry):
        for c in range(nc):
            row = lambda ref: ref[c, pl.ds(t, 1), :]
            s = s_scr[c]
            sa = jnp.sum(s * row(kk_ref), axis=1, keepdims=True)
            vcol = jnp.sum(jnp.where(eye, row(v_ref), 0.0), axis=1, keepdims=True)
            s = s * jnp.exp(row(w_ref)) - sa * row(b_ref) + vcol * row(k_ref)
            s_scr[c] = s
            ycol = jnp.sum(s * row(r_ref), axis=1, keepdims=True)
            y_ref[c, pl.ds(t, 1), :] = jnp.sum(jnp.where(eye, ycol, 0.0), axis=0, keepdims=True)
        return carry

    lax.fori_loop(0, tc, step, 0)

    @pl.when(ti == pl.num_programs(1) - 1)
    def _():
        sT_ref[...] = s_scr[...]


def _wkv(r, w, k, v, kk, b, s0, nc, tc):
    n, t, _ = r.shape
    seq = pl.BlockSpec((nc, tc, HEAD_DIM), lambda i, j: (i, j, 0))
    st = pl.BlockSpec((nc, HEAD_DIM, HEAD_DIM), lambda i, j: (i, 0, 0))
    return pl.pallas_call(
        functools.partial(_wkv_body, nc=nc, tc=tc),
        grid=(n // nc, t // tc),
        in_specs=[seq] * 6 + [st],
        out_specs=[seq, st],
        out_shape=[jax.ShapeDtypeStruct((n, t, HEAD_DIM), F32),
                   jax.ShapeDtypeStruct((n, HEAD_DIM, HEAD_DIM), F32)],
        scratch_shapes=[pltpu.VMEM((nc, HEAD_DIM, HEAD_DIM), F32)],
        compiler_params=_cparams(("arbitrary", "arbitrary")),
        name="wkv",
    )(r, w, k, v, kk, b, s0)


def _mm(a, b):
    return _dot3(a, b)


def _mm_nt(a, b):
    ah, al = _split2(a)
    bh, bl = _split2(b)
    return _dot_nt(ah, bh) + (_dot_nt(ah, bl) + _dot_nt(al, bh))


def _split3(x):
    p1 = x.astype(BF16)
    r1 = x - p1.astype(F32)
    p2 = r1.astype(BF16)
    p3 = (r1 - p2.astype(F32)).astype(BF16)
    return p1, p2, p3


def _wkv_chunk_body(r_ref, lw_ref, k_ref, v_ref, kk_ref, b_ref, lwt_ref, kkt_ref, vt_ref, s0_ref,
                    y_ref, sT_ref, s_scr, *, nc):
    ti = pl.program_id(1)
    c_len = WKV_CHUNK

    @pl.when(ti == 0)
    def _():
        s_scr[...] = s0_ref[...]

    row = lax.broadcasted_iota(jnp.int32, (c_len, c_len), 0)
    col = lax.broadcasted_iota(jnp.int32, (c_len, c_len), 1)
    strict = row > col
    incl = row >= col
    lower = incl.astype(BF16)
    upper = (row <= col).astype(BF16)
    eye = (row == col).astype(F32)

    n_sub = WKV_BLOCK // c_len
    units = [(j, c) for j in range(n_sub) for c in range(nc)]
    each = lambda f, *lists: [f(*args) for args in zip(*lists)]
    ts = lambda j: slice(j * c_len, (j + 1) * c_len)
    rows = lambda ref: [ref[c, ts(j), :] for j, c in units]
    cols = lambda ref: [ref[c, :, ts(j)] for j, c in units]
    r, lw, k, v, kk, b = (rows(ref) for ref in (r_ref, lw_ref, k_ref, v_ref, kk_ref, b_ref))
    lwt, kkt, vt = (cols(ref) for ref in (lwt_ref, kkt_ref, vt_ref))
    cum = each(lambda x: sum(_dot(lower, piece) for piece in _split3(x)), lw)
    cumt = each(lambda x: sum(_dot(piece, upper) for piece in _split3(x)), lwt)
    w_incl = each(jnp.exp, cum)
    w_last = each(lambda w: w[c_len - 1:c_len, :], w_incl)
    inv = each(lambda x: jnp.exp(-x), cum)
    qt = each(lambda x, cs, l: x * jnp.exp(cs - l), kk, cum, lw)
    qtt = each(lambda x, cs, l: x * jnp.exp(cs - l), kkt, cumt, lwt)
    rt = each(jnp.multiply, r, w_incl)
    kt = each(jnp.multiply, k, inv)
    bt = each(jnp.multiply, b, inv)
    a_qb = each(lambda x, y: jnp.where(strict, _mm_nt(x, y), 0.0), qt, bt)
    a_qk = each(lambda x, y: jnp.where(strict, _mm_nt(x, y), 0.0), qt, kt)
    a_rk = each(lambda x, y: jnp.where(incl, _mm_nt(x, y), 0.0), rt, kt)
    a_rb = each(lambda x, y: jnp.where(incl, _mm_nt(x, y), 0.0), rt, bt)
    tinv = each(lambda n: eye - n, a_qb)
    power = a_qb
    for _ in range(WKV_CHUNK_LOG2 - 1):
        power = each(_mm, power, power)
        tinv = each(lambda t, p: t + _mm(t, p), tinv, power)
    tq = each(_mm, tinv, qt)
    av = each(_mm, a_qk, v)
    av_t = each(_mm_nt, vt, a_qk)
    tq_t = each(_mm_nt, qtt, tinv)
    tav = each(_mm, tinv, av)
    tav_t = each(_mm_nt, av_t, tinv)
    ark_v = each(_mm, a_rk, v)
    rq = each(lambda x, a, y: x - _mm(a, y), rt, a_rb, tq)
    yv = each(lambda x, a, y: x - _mm(a, y), ark_v, a_rb, tav)
    gain = each(lambda x, y, w: _mm(x, y * w), vt, kt, w_last)
    bw = each(jnp.multiply, bt, w_last)
    for j in range(n_sub):
        ids = [j * nc + c for c in range(nc)]
        s = [s_scr[c] for c in range(nc)]
        sa_t = [_mm(s[c], tq_t[i]) + tav_t[i] for c, i in enumerate(ids)]
        ys = [_mm_nt(rq[i], s[c]) + yv[i] for c, i in enumerate(ids)]
        new = [s[c] * w_last[i] + gain[i] - _mm(sa_t[c], bw[i]) for c, i in enumerate(ids)]
        for c, i in enumerate(ids):
            y_ref[c, ts(j), :] = ys[c]
            s_scr[c] = new[c]

    @pl.when(ti == pl.num_programs(1) - 1)
    def _():
        sT_ref[...] = s_scr[...]


def _wkv_chunked(r, lw, k, v, kk, b, lwt, kkt, vt, s0, nc):
    n, t, _ = r.shape
    seq = pl.BlockSpec((nc, WKV_BLOCK, HEAD_DIM), lambda i, j: (i, j, 0))
    seq_t = pl.BlockSpec((nc, HEAD_DIM, WKV_BLOCK), lambda i, j: (i, 0, j))
    st = pl.BlockSpec((nc, HEAD_DIM, HEAD_DIM), lambda i, j: (i, 0, 0))
    return pl.pallas_call(
        functools.partial(_wkv_chunk_body, nc=nc),
        grid=(n // nc, t // WKV_BLOCK),
        in_specs=[seq] * 6 + [seq_t] * 3 + [st],
        out_specs=[seq, st],
        out_shape=[jax.ShapeDtypeStruct((n, t, HEAD_DIM), F32),
                   jax.ShapeDtypeStruct((n, HEAD_DIM, HEAD_DIM), F32)],
        scratch_shapes=[pltpu.VMEM((nc, HEAD_DIM, HEAD_DIM), F32)],
        compiler_params=_cparams(("arbitrary", "arbitrary")),
        name="wkv_chunked",
    )(r, lw, k, v, kk, b, lwt, kkt, vt, s0)


def _sb_block(q, kb, vb, bias, rest, umat, mask):
    z = _dot_nt(q, kb) + bias
    sp = _softplus(z)
    counted = sp if mask is None else jnp.where(mask, sp, 0.0)
    after = _dot(counted.astype(BF16), umat)
    a = jnp.exp((z - sp) + after + rest)
    if mask is not None:
        a = jnp.where(mask, a, 0.0)
    out = _dot(a.astype(BF16), vb)
    return out, after[:, 0:1] - counted[:, 0:1]


def _attn_prompt_body(q_ref, k_ref, v_ref, bias_ref, o_ref, *, tile):
    qi = pl.program_id(1)
    heads = range(ATT_HEADS)
    hs = lambda a, h: a[:, h * HEAD_DIM:(h + 1) * HEAD_DIM]
    q_all = q_ref[0]
    qs = [hs(q_all, h) for h in heads]
    biases = [bias_ref[h] for h in heads]
    umat = _suffix_matrix(tile)
    causal = (lax.broadcasted_iota(jnp.int32, (tile, tile), 1)
              < lax.broadcasted_iota(jnp.int32, (tile, tile), 0))

    def visit(j, accs, rests, mask):
        start = pl.multiple_of(j * tile, tile)
        kb = k_ref[0, pl.ds(start, tile), :]
        vb = v_ref[0, pl.ds(start, tile), :]
        zs = [_dot_nt(qs[h], hs(kb, h)) + biases[h] for h in heads]
        sps = [_softplus(z) for z in zs]
        counted = sps if mask is None else [jnp.where(mask, sp, 0.0) for sp in sps]
        afters = [_dot(sp.astype(BF16), umat) for sp in counted]
        probs = [jnp.exp((zs[h] - sps[h]) + afters[h] + rests[h]) for h in heads]
        if mask is not None:
            probs = [jnp.where(mask, a, 0.0) for a in probs]
        outs = [_dot(probs[h].astype(BF16), hs(vb, h)) for h in heads]
        new_accs = tuple(accs[h] + outs[h] for h in heads)
        new_rests = tuple(rests[h] + afters[h][:, 0:1] - counted[h][:, 0:1] for h in heads)
        return new_accs, new_rests

    accs = tuple(jnp.zeros((tile, HEAD_DIM), F32) for _ in heads)
    rests = tuple(jnp.zeros((tile, 1), F32) for _ in heads)
    accs, rests = visit(qi, accs, rests, causal)
    accs, rests = lax.fori_loop(0, qi, lambda jj, c: visit(qi - 1 - jj, c[0], c[1], None), (accs, rests))
    r = lax.broadcasted_iota(jnp.int32, (HEAD_DIM, ATT_DIM), 0)
    c = lax.broadcasted_iota(jnp.int32, (HEAD_DIM, ATT_DIM), 1)
    out = jnp.zeros((tile, ATT_DIM), F32)
    for h in heads:
        out = out + _dot(accs[h].astype(BF16), (c == r + h * HEAD_DIM).astype(BF16))
    o_ref[0] = out.astype(o_ref.dtype)


def _attn_prompt(qb, kb, vb, bias):
    b, t, d = qb.shape
    tile = ATT_TILE
    bias_b = jnp.broadcast_to(bias.astype(F32)[:, None, None], (ATT_HEADS, 1, tile))
    qspec = pl.BlockSpec((1, tile, d), lambda i, l: (i, l, 0))
    kvspec = pl.BlockSpec((1, t, d), lambda i, l: (i, 0, 0))
    return pl.pallas_call(
        functools.partial(_attn_prompt_body, tile=tile),
        grid=(b, t // tile),
        in_specs=[qspec, kvspec, kvspec, _const_spec((ATT_HEADS, 1, tile))],
        out_specs=qspec,
        out_shape=jax.ShapeDtypeStruct((b, t, d), BF16),
        compiler_params=_cparams(("arbitrary", "arbitrary")),
        name="attn_prompt",
    )(qb, kb, vb, bias_b)


def _attn_decode_body(pt_ref, q_ref, bias_ref, kn_ref, vn_ref, *refs, npg, nq):
    k_refs = refs[:npg]
    v_refs = refs[npg:2 * npg]
    o_ref, rest_ref, acc_ref = refs[2 * npg:]
    jb = pl.program_id(1)
    rows = ATT_HEADS * Q_PAD
    pages = range(npg)
    spread = lambda tot: jnp.broadcast_to(tot, (rows, PAGE_SIZE))
    q = q_ref[0]
    bias = bias_ref[...]
    umat = _suffix_matrix(PAGE_SIZE)

    @pl.when(jb == 0)
    def _():
        kidx = lax.broadcasted_iota(jnp.int32, (rows, PAGE_SIZE), 1)
        qidx = lax.broadcasted_iota(jnp.int32, (rows, PAGE_SIZE), 0) % Q_PAD
        out, tot = _sb_block(q, kn_ref[0].astype(BF16), vn_ref[0].astype(BF16), bias,
                             jnp.zeros((rows, 1), F32), umat, (kidx < qidx) & (kidx < nq))
        acc_ref[...] = out
        rest_ref[...] = spread(tot)

    zs = [_dot_nt(q, k_refs[s][0, 0].astype(BF16)) + bias for s in pages]
    sps = [_softplus(z) for z in zs]
    afters = [_dot(sp.astype(BF16), umat) for sp in sps]
    rest = rest_ref[...]
    probs = []
    for s in pages:
        probs.append(jnp.exp((zs[s] - sps[s]) + afters[s] + rest).astype(BF16))
        rest = rest + spread(afters[s][:, 0:1] - sps[s][:, 0:1])
    rest_ref[...] = rest
    out = _dot(probs[0], v_refs[0][0, 0].astype(BF16))
    for s in pages[1:]:
        out = out + _dot(probs[s], v_refs[s][0, 0].astype(BF16))
    acc_ref[...] += out

    @pl.when(jb == pl.num_programs(1) - 1)
    def _():
        head_of_row = lax.broadcasted_iota(jnp.int32, (rows, ATT_DIM), 0) // Q_PAD
        head_of_col = lax.broadcasted_iota(jnp.int32, (rows, ATT_DIM), 1) // HEAD_DIM
        kept = jnp.where(head_of_row == head_of_col, acc_ref[...], 0.0)
        out = kept[0:Q_PAD]
        for h in range(1, ATT_HEADS):
            out = out + kept[h * Q_PAD:(h + 1) * Q_PAD]
        o_ref[0] = out.astype(o_ref.dtype)


def _attn_decode(li, qb, k_new, v_new, cache_k, cache_v, page_table, bias):
    b, t, _ = qb.shape
    n_pages = page_table.shape[1]
    npg = PAGES_PER_STEP
    rows = ATT_HEADS * Q_PAD
    qpad = jnp.pad(qb, ((0, 0), (0, Q_PAD - t), (0, 0)))
    head_of_col = jnp.arange(ATT_DIM) // HEAD_DIM
    qbd = jnp.where(head_of_col[None, None, None, :] == jnp.arange(ATT_HEADS)[None, :, None, None],
                    qpad[:, None], jnp.zeros((), BF16)).reshape(b, rows, ATT_DIM)
    bias_b = jnp.broadcast_to(jnp.repeat(bias.astype(F32), Q_PAD)[:, None], (rows, PAGE_SIZE))
    kn = jnp.pad(k_new, ((0, 0), (0, PAGE_SIZE - t), (0, 0)))
    vn = jnp.pad(v_new, ((0, 0), (0, PAGE_SIZE - t), (0, 0)))

    def page_spec(s):
        return pl.BlockSpec((1, 1, PAGE_SIZE, ATT_DIM),
                            lambda i, j, pt: (li, pt[i * n_pages + n_pages - 1 - (j * npg + s)], 0, 0))

    per_b = lambda r, c: pl.BlockSpec((1, r, c), lambda i, j, pt: (i, 0, 0))
    grid_spec = pltpu.PrefetchScalarGridSpec(
        num_scalar_prefetch=1,
        grid=(b, n_pages // npg),
        in_specs=[per_b(rows, ATT_DIM), pl.BlockSpec((rows, PAGE_SIZE), lambda i, j, pt: (0, 0)),
                  per_b(PAGE_SIZE, ATT_DIM), per_b(PAGE_SIZE, ATT_DIM)]
                 + [page_spec(s) for s in range(npg)] * 2,
        out_specs=per_b(Q_PAD, ATT_DIM),
        scratch_shapes=[pltpu.VMEM((rows, PAGE_SIZE), F32), pltpu.VMEM((rows, ATT_DIM), F32)],
    )
    out = pl.pallas_call(
        functools.partial(_attn_decode_body, npg=npg, nq=t),
        grid_spec=grid_spec,
        out_shape=jax.ShapeDtypeStruct((b, Q_PAD, ATT_DIM), BF16),
        compiler_params=_cparams(("arbitrary", "arbitrary")),
        name="attn_decode",
    )(page_table.reshape(-1), qbd, bias_b, kn, vn, *([cache_k] * npg), *([cache_v] * npg))
    return out[:, :t]


def _outproj_body(x_ref, yc_ref, yw_ref, bonus_ref, g_ref, ya_ref, lng_ref, lnb_ref, w_ref, o_ref):
    bd = _head_blockdiag(RWKV_DIM)
    y = yw_ref[...]
    inv = 1.0 / HEAD_DIM
    d = y - _dot_exact_rhs(y, bd) * inv
    var = _dot_exact_rhs(d * d, bd) * inv
    yn = d * lax.rsqrt(var + GN_EPS) * lng_ref[...] + lnb_ref[...]
    yr = ((yn + bonus_ref[...]) * g_ref[...]).astype(BF16)
    c0, c1 = CONV_CH, CONV_CH + RWKV_DIM
    o_ref[...] = (x_ref[...] + _dot(yc_ref[...], w_ref[0:c0, :]) + _dot(yr, w_ref[c0:c1, :])
                  + _dot(ya_ref[...], w_ref[c1:D_MODEL, :]))


def _outproj(x, yc, yw, bonus, g, ya, ln_g, ln_b, w_b, tm):
    n = x.shape[0]
    return pl.pallas_call(
        _outproj_body,
        grid=(n // tm,),
        in_specs=[_row_spec(tm, D_MODEL), _row_spec(tm, CONV_CH), _row_spec(tm, RWKV_DIM), _row_spec(tm, RWKV_DIM),
                  _row_spec(tm, RWKV_DIM), _row_spec(tm, ATT_DIM), _const_spec((1, RWKV_DIM)),
                  _const_spec((1, RWKV_DIM)), _const_spec((D_MODEL, D_MODEL))],
        out_specs=_row_spec(tm, D_MODEL),
        out_shape=jax.ShapeDtypeStruct((n, D_MODEL), F32),
        compiler_params=_cparams(("arbitrary",)),
        name="outproj",
    )(x, yc, yw, bonus, g, ya, ln_g.reshape(1, -1), ln_b.reshape(1, -1), w_b)


SUBLANES = 8
KEY_GROUPS = N_KEYS // SUBLANES


def _oddeven_merge(lo, hi, r):
    step = r * 2
    if step < hi - lo:
        yield from _oddeven_merge(lo, hi, step)
        yield from _oddeven_merge(lo + r, hi, step)
        yield from [(i, i + r) for i in range(lo + r, hi - r, step)]
    else:
        yield (lo, lo + r)


def _oddeven_sort(lo, hi):
    if hi - lo >= 1:
        mid = lo + (hi - lo) // 2
        yield from _oddeven_sort(lo, mid)
        yield from _oddeven_sort(mid + 1, hi)
        yield from _oddeven_merge(lo, hi, 1)


SORT_NET = tuple(_oddeven_sort(0, PEER_TOPK - 1))
BITONIC_NET = tuple((i, i + s) for s in (8, 4, 2, 1) for i in range(PEER_TOPK) if not i & s)
SUM_SLOTS = tuple((a, b) for a in range(PEER_TOPK) for b in range(PEER_TOPK) if (a + 1) * (b + 1) <= PEER_TOPK)


def _exchange(vals, i, j):
    vals[i], vals[j] = jnp.maximum(vals[i], vals[j]), jnp.minimum(vals[i], vals[j])


def _sublane_max(x):
    for shift in (4, 2, 1):
        x = jnp.maximum(x, pltpu.roll(x, shift, 0))
    return x


def _top16_sorted(groups):
    vals = list(groups)
    for i, j in SORT_NET:
        _exchange(vals, i, j)
    for shift in (4, 2, 1):
        moved = [pltpu.roll(v, shift, 0) for v in vals]
        vals = [jnp.maximum(vals[i], moved[PEER_TOPK - 1 - i]) for i in range(PEER_TOPK)]
        for i, j in BITONIC_NET:
            _exchange(vals, i, j)
    return vals


def _peer_route_body(x_ref, g_ref, wq_ref, sk_ref, h_out, rank_out, e2_out, cnt_out, f1_out):
    tm = x_ref.shape[0]
    kk = PEER_TOPK
    hb = _rmsnorm(x_ref[...], g_ref[...]).astype(BF16)
    h_out[...] = hb
    sub = lax.broadcasted_iota(jnp.int32, (SUBLANES, tm), 0)
    split = lambda s: [s[i * SUBLANES:(i + 1) * SUBLANES, :] for i in range(KEY_GROUPS)]
    join = lambda parts: jnp.concatenate(parts, axis=0)
    neg_inf = jnp.full((SUBLANES, tm), -jnp.inf, F32)
    half = PEER_DK // 2
    for h in range(PEER_HEADS):
        q = _dot(hb, wq_ref[:, h * PEER_DK:(h + 1) * PEER_DK])
        s1 = split(_dot_nt(sk_ref[2 * h], q[:, :half].astype(BF16)))
        s2 = split(_dot_nt(sk_ref[2 * h + 1], q[:, half:].astype(BF16)))
        v1 = _top16_sorted(s1)
        v2 = _top16_sorted(s2)
        sums = {ab: v1[ab[0]] + v2[ab[1]] for ab in SUM_SLOTS}
        packs = []
        for j in range(0, len(SUM_SLOTS), SUBLANES):
            pack = neg_inf
            for s, ab in enumerate(SUM_SLOTS[j:j + SUBLANES]):
                pack = jnp.where(sub == s, sums[ab], pack)
            packs.append(pack)
        best = []
        for _ in range(kk):
            m = packs[0]
            for p in packs[1:]:
                m = jnp.maximum(m, p)
            m = _sublane_max(m)
            best.append(m)
            packs = [jnp.where(p == m, -jnp.inf, p) for p in packs]
        z = jnp.ones((SUBLANES, tm), F32)
        for i in range(1, kk):
            z = z + jnp.exp(best[i] - best[0])
        inv_z = 1.0 / z
        cnt = [jnp.zeros((SUBLANES, tm), F32) for _ in range(kk)]
        for a, b in SUM_SLOTS:
            cnt[a] = cnt[a] + jnp.where(sums[(a, b)] >= best[kk - 1], 1.0, 0.0)
        cnt_full, f1_full, rank_full, e2_full = [], [], [], []
        for g in range(KEY_GROUPS):
            c = jnp.zeros((SUBLANES, tm), F32)
            for a in range(kk):
                c = jnp.where(s1[g] == v1[a], cnt[a], c)
            cnt_full.append(c)
            f1_full.append(jnp.where(s1[g] >= v1[kk - 1], jnp.exp(s1[g] - v1[0]), 0.0) * inv_z)
            rk = jnp.full((SUBLANES, tm), float(kk), F32)
            for b in reversed(range(kk)):
                rk = jnp.where(s2[g] == v2[b], float(b), rk)
            rank_full.append(rk)
            e2_full.append(jnp.where(s2[g] >= v2[kk - 1], jnp.exp(s2[g] - v2[0]), 0.0))
        cnt_out[h] = join(cnt_full)
        f1_out[h] = join(f1_full)
        rank_out[h] = join(rank_full).astype(BF16)
        e2_out[h] = join(e2_full).astype(BF16)


def _peer_route(x, g, wq_b, sk_b, tm):
    n = x.shape[0]
    keyed = pl.BlockSpec((PEER_HEADS, N_KEYS, tm), lambda i: (0, 0, i))
    keyed_shape = lambda dt: jax.ShapeDtypeStruct((PEER_HEADS, N_KEYS, n), dt)
    return pl.pallas_call(
        _peer_route_body,
        grid=(n // tm,),
        in_specs=[_row_spec(tm, D_MODEL), _const_spec((1, D_MODEL)), _const_spec((D_MODEL, PEER_HEADS * PEER_DK)),
                  _const_spec((2 * PEER_HEADS, N_KEYS, PEER_DK // 2))],
        out_specs=[_row_spec(tm, D_MODEL), keyed, keyed, keyed, keyed],
        out_shape=[jax.ShapeDtypeStruct((n, D_MODEL), BF16), keyed_shape(BF16), keyed_shape(BF16),
                   keyed_shape(F32), keyed_shape(F32)],
        compiler_params=_cparams(("arbitrary",)),
        name="peer_route",
    )(x, g.reshape(1, -1), wq_b, sk_b)


GATE_ROWS = 16


def _gelu_tanh(x):
    return 0.5 * x * (1.0 + jnp.tanh(0.7978845608028654 * (x + 0.044715 * (x * x * x))))


def _peer_expert_body(x_ref, h_ref, rank_ref, e2_ref, cnt_ref, f1_ref, u_ref, vt_ref, o_ref, acc_ref):
    ec = pl.program_id(1)

    @pl.when(ec == 0)
    def _():
        acc_ref[...] = jnp.zeros(acc_ref.shape, F32)

    tm = h_ref.shape[0]
    n_grp = 2 if tm % 512 == 0 else 1
    grp = tm // n_grp
    toks = [slice(g * grp, (g + 1) * grp) for g in range(n_grp)]
    hids = [_dot_nt(u_ref[...], h_ref[tk, :]) for tk in toks]
    heads = range(PEER_HEADS)
    for g, tk in enumerate(toks):
        weights = []
        for r in range(PEER_ROWS):
            bcast = lambda ref, h: jnp.broadcast_to(ref[h, 0, r:r + 1, tk], (GATE_ROWS, grp)).astype(BF16)
            cnts = [bcast(cnt_ref, h) for h in heads]
            f1s = [bcast(f1_ref, h) for h in heads]
            for j in range(N_KEYS // GATE_ROWS):
                keys = slice(j * GATE_ROWS, (j + 1) * GATE_ROWS)
                gate = jnp.zeros((GATE_ROWS, grp), BF16)
                for h in heads:
                    e2 = e2_ref[h, keys, tk]
                    gate = gate + jnp.where(rank_ref[h, keys, tk] < cnts[h], e2, jnp.zeros_like(e2)) * f1s[h]
                rows = slice(r * N_KEYS + j * GATE_ROWS, r * N_KEYS + (j + 1) * GATE_ROWS)
                weights.append(gate * _gelu_tanh(hids[g][rows]).astype(BF16))
        acc_ref[:, tk] += _dot(vt_ref[...], jnp.concatenate(weights, axis=0))

    @pl.when(ec == pl.num_programs(1) - 1)
    def _():
        o_ref[...] = x_ref[...] + acc_ref[...].T


def _peer_expert(x, hb, rank2, e2, cnt, f1, u_b, vt_b, tm):
    n = x.shape[0]
    te = PEER_ROWS * N_KEYS
    ne = u_b.shape[0] // te
    chunked = lambda a: a.reshape(PEER_HEADS, ne, PEER_ROWS, n)
    tok = lambda c: pl.BlockSpec((tm, c), lambda i, e: (i, 0))
    keyed = pl.BlockSpec((PEER_HEADS, N_KEYS, tm), lambda i, e: (0, 0, i))
    row = pl.BlockSpec((PEER_HEADS, 1, PEER_ROWS, tm), lambda i, e: (0, e, 0, i))
    return pl.pallas_call(
        _peer_expert_body,
        grid=(n // tm, ne),
        in_specs=[tok(D_MODEL), tok(D_MODEL), keyed, keyed, row, row,
                  pl.BlockSpec((te, D_MODEL), lambda i, e: (e, 0)),
                  pl.BlockSpec((D_MODEL, te), lambda i, e: (0, e))],
        out_specs=tok(D_MODEL),
        out_shape=jax.ShapeDtypeStruct((n, D_MODEL), F32),
        scratch_shapes=[pltpu.VMEM((D_MODEL, tm), F32)],
        compiler_params=_cparams(("arbitrary", "arbitrary")),
        name="peer_expert",
    )(x, hb, rank2, e2, chunked(cnt), chunked(f1), u_b, vt_b)


def _final_norm_body(x_ref, g_ref, o_ref):
    o_ref[...] = _rmsnorm(x_ref[...], g_ref[...])


def _final_norm(x, g, tm):
    n = x.shape[0]
    return pl.pallas_call(
        _final_norm_body,
        grid=(n // tm,),
        in_specs=[_row_spec(tm, D_MODEL), _const_spec((1, D_MODEL))],
        out_specs=_row_spec(tm, D_MODEL),
        out_shape=jax.ShapeDtypeStruct((n, D_MODEL), F32),
        compiler_params=_cparams(("arbitrary",)),
        name="final_norm",
    )(x, g.reshape(1, -1))


def _heads_first(a, b, t):
    h = a.shape[1] // HEAD_DIM
    return a.reshape(b, t, h, HEAD_DIM).transpose(0, 2, 1, 3).reshape(b * h, t, HEAD_DIM)


def _tokens_first(a, b, t):
    h = a.shape[0] // b
    return a.reshape(b, h, t, HEAD_DIM).transpose(0, 2, 1, 3).reshape(b * t, h * HEAD_DIM)


def _layer(x, b, t, conv_buf, shift_prev, wkv0, attend, lw, tm, conv_tile, wkv_chains, wkv_tile):
    p_conv, p_rwkv, k, v, qb, kb, vb = _inproj(x, lw["norm1_g"], lw["w_in"], tm)

    y_conv, conv_new = _conv_mixer(p_conv.reshape(b, t, CONV_COLS), conv_buf, lw["conv_w"], lw["conv_b"],
                                   lw["conv_ln_g"], lw["conv_ln_b"], conv_tile)

    p3 = p_rwkv.reshape(b, t, RWKV_COLS)
    pre = lambda prev, **kw: _rwkv_pre(p_rwkv, prev, lw["rwkv_mu"], lw["rwkv_w0"], lw["rwkv_w_up"], lw["rwkv_a0"],
                                       lw["rwkv_a_up"], lw["rwkv_g_up"], lw["rwkv_k_k"], lw["rwkv_k_a"],
                                       lw["rwkv_r_k"], tm, **kw)
    s0 = wkv0.reshape(b * RWKV_HEADS, HEAD_DIM, HEAD_DIM)
    if t % WKV_BLOCK == 0 and t % tm == 0:
        *steps, g, bonus = pre(shift_prev, seq_len=t)
        y_wkv, wkv_new = _wkv_chunked(*steps, s0, wkv_chains)
    else:
        prev = jnp.concatenate([shift_prev[:, None, :], p3[:, :-1]], axis=1).reshape(b * t, RWKV_COLS)
        *steps, g, bonus = pre(prev)
        y_wkv, wkv_new = _wkv(*(_heads_first(a, b, t) for a in steps), s0, wkv_chains, wkv_tile)
    y_wkv = _tokens_first(y_wkv, b, t)

    y_att = attend(qb, k, v, kb, vb)

    x = _outproj(x, y_conv.reshape(b * t, CONV_CH), y_wkv, bonus, g, y_att, lw["rwkv_ln_g"], lw["rwkv_ln_b"],
                 lw["w_out"], tm)
    routed = _peer_route(x, lw["norm2_g"], lw["peer_w_query"], lw["peer_sub_keys"], min(tm, 256))
    x = _peer_expert(x, *routed, lw["peer_u"], lw["peer_vt"], tm)
    states = (k.reshape(b, t, ATT_HEADS, HEAD_DIM), v.reshape(b, t, ATT_HEADS, HEAD_DIM), conv_new,
              p3[:, -1], wkv_new.reshape(b, RWKV_HEADS, HEAD_DIM, HEAD_DIM))
    return x, states


def kernel(x_prompt, x_sample, cache_k, cache_v, state_conv, state_shift, state_wkv, page_table, norm1_g, w_in, conv_w, conv_b, conv_ln_g, conv_ln_b, rwkv_mu, rwkv_w0, rwkv_w_up, rwkv_a0, rwkv_a_up, rwkv_g_up, rwkv_k_k, rwkv_k_a, rwkv_r_k, rwkv_ln_g, rwkv_ln_b, att_bias, w_out, norm2_g, peer_w_query, peer_sub_keys, peer_u, peer_v, final_g):
    depth = w_in.shape[0]
    bp, tp, _ = x_prompt.shape
    bs, ts, _ = x_sample.shape
    n_pool = cache_k.shape[1]
    ck = cache_k.reshape(depth, n_pool, PAGE_SIZE, ATT_DIM)
    cv = cache_v.reshape(depth, n_pool, PAGE_SIZE, ATT_DIM)
    xp = x_prompt.reshape(bp * tp, D_MODEL)
    xs = x_sample.reshape(bs * ts, D_MODEL)
    st_p, st_s = [], []
    for li in range(depth):
        lw = dict(
            norm1_g=norm1_g[li], w_in=w_in[li].astype(BF16), conv_w=conv_w[li], conv_b=conv_b[li],
            conv_ln_g=conv_ln_g[li], conv_ln_b=conv_ln_b[li], rwkv_mu=rwkv_mu[li], rwkv_w0=rwkv_w0[li],
            rwkv_w_up=rwkv_w_up[li], rwkv_a0=rwkv_a0[li], rwkv_a_up=rwkv_a_up[li], rwkv_g_up=rwkv_g_up[li],
            rwkv_k_k=rwkv_k_k[li], rwkv_k_a=rwkv_k_a[li], rwkv_r_k=rwkv_r_k[li].reshape(-1),
            rwkv_ln_g=rwkv_ln_g[li], rwkv_ln_b=rwkv_ln_b[li], w_out=w_out[li].astype(BF16), norm2_g=norm2_g[li],
            peer_w_query=peer_w_query[li].astype(BF16),
            peer_sub_keys=peer_sub_keys[li].astype(BF16).reshape(2 * PEER_HEADS, N_KEYS, PEER_DK // 2),
            peer_u=peer_u[li].astype(BF16), peer_vt=peer_v[li].astype(BF16).T)
        bias = att_bias[li]

        def attend_prompt(qb, k, v, kb, vb):
            sq = lambda a: a.reshape(bp, tp, ATT_DIM)
            return _attn_prompt(sq(qb), sq(kb), sq(vb), bias).reshape(bp * tp, ATT_DIM)

        def attend_sample(qb, k, v, kb, vb):
            o = _attn_decode(li, qb.reshape(bs, ts, ATT_DIM), k.reshape(bs, ts, ATT_DIM),
                             v.reshape(bs, ts, ATT_DIM), ck, cv, page_table, bias)
            return o.reshape(bs * ts, ATT_DIM)

        xp, new_p = _layer(xp, bp, tp, jnp.zeros((bp, CONV_WIDTH - 1, CONV_CH), F32),
                           jnp.zeros((bp, RWKV_COLS), F32), jnp.zeros((bp, RWKV_HEADS, HEAD_DIM, HEAD_DIM), F32),
                           attend_prompt, lw, tm=512, conv_tile=512, wkv_chains=4, wkv_tile=256)
        xs, new_s = _layer(xs, bs, ts, state_conv[li], state_shift[li], state_wkv[li],
                           attend_sample, lw, tm=bs * ts, conv_tile=ts, wkv_chains=bp * RWKV_HEADS, wkv_tile=ts)
        st_p.append(new_p)
        st_s.append(new_s)
    y_prompt = _final_norm(xp, final_g, 512).reshape(bp, tp, D_MODEL)
    y_sample = _final_norm(xs, final_g, bs * ts).reshape(bs, ts, D_MODEL)
    stack = lambda st, j: jnp.stack([s[j] for s in st], axis=0)
    return (y_prompt, y_sample) + tuple(stack(st_p, j) for j in range(5)) + tuple(stack(st_s, j) for j in range(5))
```

```python
import functools

import jax
import jax.numpy as jnp
from jax import lax
from jax.experimental import pallas as pl
from jax.experimental.pallas import tpu as pltpu

F32 = jnp.float32
BF16 = jnp.bfloat16

D_MODEL = 1024
HEAD_DIM = 64
CONV_CH = 256
CONV_WIDTH = 31
CONV_HALO = 32
RWKV_HEADS = 6
RWKV_DIM = 384
ATT_HEADS = 6
ATT_DIM = 384
DECAY_LORA = 64
AAA_LORA = 64
GATE_LORA = 128
CONV_COLS = 512
RWKV_COLS = 1408
IN_COLS = 3072
PAGE_SIZE = 128
PEER_HEADS = 8
PEER_DK = 256
N_KEYS = 128
PEER_TOPK = 16
RMS_EPS = 1e-5
LN_EPS = 1e-5
GN_EPS = 64e-5

ATT_TILE = 256
PAGES_PER_STEP = 8
Q_PAD = 8
PEER_ROWS = 16
WKV_CHUNK_LOG2 = 6
WKV_CHUNK = 1 << WKV_CHUNK_LOG2
WKV_BLOCK = 128
VMEM_LIMIT = 56 * 1024 * 1024


def _cparams(sem):
    return pltpu.CompilerParams(dimension_semantics=sem, vmem_limit_bytes=VMEM_LIMIT)


def _sigmoid(x):
    return 1.0 / (1.0 + jnp.exp(-x))


def _softplus(x):
    return jnp.maximum(x, 0.0) + jnp.log(1.0 + jnp.exp(-jnp.abs(x)))


def _dot(a, b):
    return jnp.dot(a, b, preferred_element_type=F32)


def _dot_nt(a, b):
    return lax.dot_general(a, b, (((1,), (1,)), ((), ())), preferred_element_type=F32)


def _split2(x):
    hi = x.astype(BF16)
    lo = (x - hi.astype(F32)).astype(BF16)
    return hi, lo


def _dot3(a, b):
    ah, al = _split2(a)
    bh, bl = _split2(b)
    return _dot(ah, bh) + (_dot(ah, bl) + _dot(al, bh))


def _dot_exact_rhs(x, m):
    hi, lo = _split2(x)
    return _dot(hi, m) + _dot(lo, m)


def _rmsnorm(x, g):
    return x * lax.rsqrt(jnp.mean(x * x, axis=-1, keepdims=True) + RMS_EPS) * g


def _head_blockdiag(n):
    r = lax.broadcasted_iota(jnp.int32, (n, n), 0) // HEAD_DIM
    c = lax.broadcasted_iota(jnp.int32, (n, n), 1) // HEAD_DIM
    return (r == c).astype(BF16)


def _suffix_matrix(n):
    r = lax.broadcasted_iota(jnp.int32, (n, n), 0)
    c = lax.broadcasted_iota(jnp.int32, (n, n), 1)
    return jnp.where(r > c, -1.0, 0.0).astype(BF16)


def _row_spec(tm, cols):
    return pl.BlockSpec((tm, cols), lambda i: (i, 0))


def _const_spec(shape):
    return pl.BlockSpec(shape, lambda *_: (0,) * len(shape))


def _inproj_body(x_ref, g_ref, w_ref, conv_ref, rwkv_ref, k_ref, v_ref, qb_ref, kb_ref, vb_ref):
    h = _rmsnorm(x_ref[...], g_ref[...]).astype(BF16)
    o0, o1, o2, o3 = CONV_COLS, CONV_COLS + RWKV_COLS, CONV_COLS + RWKV_COLS + ATT_DIM, IN_COLS - ATT_DIM
    conv_ref[...] = _dot(h, w_ref[:, 0:o0])
    rwkv_ref[...] = _dot(h, w_ref[:, o0:o1])
    q = _dot(h, w_ref[:, o1:o2])
    k = _dot(h, w_ref[:, o2:o3])
    v = _dot(h, w_ref[:, o3:IN_COLS])
    k_ref[...] = k
    v_ref[...] = v
    qb_ref[...] = (q * (HEAD_DIM ** -0.5)).astype(BF16)
    kb_ref[...] = k.astype(BF16)
    vb_ref[...] = v.astype(BF16)


def _inproj(x, g, w_b, tm):
    n = x.shape[0]
    widths = (CONV_COLS, RWKV_COLS, ATT_DIM, ATT_DIM, ATT_DIM, ATT_DIM, ATT_DIM)
    dtypes = (F32, F32, F32, F32, BF16, BF16, BF16)
    return pl.pallas_call(
        _inproj_body,
        grid=(n // tm,),
        in_specs=[_row_spec(tm, D_MODEL), _const_spec((1, D_MODEL)), _const_spec((D_MODEL, IN_COLS))],
        out_specs=[_row_spec(tm, c) for c in widths],
        out_shape=[jax.ShapeDtypeStruct((n, c), d) for c, d in zip(widths, dtypes)],
        compiler_params=_cparams(("arbitrary",)),
        name="inproj",
    )(x, g.reshape(1, -1), w_b)


def _conv_body(p_ref, buf_ref, w_ref, cb_ref, lg_ref, lb_ref, y_ref, nb_ref, ext_ref, *, tt):
    keep = CONV_WIDTH - 1
    lead = CONV_HALO - keep

    @pl.when(pl.program_id(1) == 0)
    def _():
        ext_ref[0:CONV_HALO, :] = jnp.zeros((CONV_HALO, CONV_CH), F32)
        ext_ref[lead:CONV_HALO, :] = buf_ref[0]

    p = p_ref[0]
    ext_ref[CONV_HALO:CONV_HALO + tt, :] = p[:, :CONV_CH] * _sigmoid(p[:, CONV_CH:])
    acc = jnp.zeros((tt, CONV_CH), F32)
    for j in range(CONV_WIDTH):
        acc = acc + w_ref[j:j + 1, :] * ext_ref[lead + j:lead + j + tt, :]
    y = acc + cb_ref[...]
    mu = jnp.mean(y, axis=-1, keepdims=True)
    d = y - mu
    var = jnp.mean(d * d, axis=-1, keepdims=True)
    y = d * lax.rsqrt(var + LN_EPS) * lg_ref[...] + lb_ref[...]
    y_ref[0] = (y * _sigmoid(y)).astype(y_ref.dtype)
    nb_ref[0] = ext_ref[tt + lead:tt + CONV_HALO, :]
    tail = ext_ref[tt:tt + CONV_HALO, :]
    ext_ref[0:CONV_HALO, :] = tail


def _conv_mixer(p, buf, w, cb, lg, lb, tt):
    b, t, _ = p.shape
    keep = CONV_WIDTH - 1
    return pl.pallas_call(
        functools.partial(_conv_body, tt=tt),
        grid=(b, t // tt),
        in_specs=[pl.BlockSpec((1, tt, CONV_COLS), lambda i, j: (i, j, 0)),
                  pl.BlockSpec((1, keep, CONV_CH), lambda i, j: (i, 0, 0)),
                  _const_spec((CONV_WIDTH, CONV_CH)), _const_spec((1, CONV_CH)),
                  _const_spec((1, CONV_CH)), _const_spec((1, CONV_CH))],
        out_specs=[pl.BlockSpec((1, tt, CONV_CH), lambda i, j: (i, j, 0)),
                   pl.BlockSpec((1, keep, CONV_CH), lambda i, j: (i, 0, 0))],
        out_shape=[jax.ShapeDtypeStruct((b, t, CONV_CH), BF16),
                   jax.ShapeDtypeStruct((b, keep, CONV_CH), F32)],
        scratch_shapes=[pltpu.VMEM((tt + CONV_HALO, CONV_CH), F32)],
        compiler_params=_cparams(("arbitrary", "arbitrary")),
        name="conv_mixer",
    )(p, buf, w, cb.reshape(1, -1), lg.reshape(1, -1), lb.reshape(1, -1))


def _rwkv_pre_body(p_ref, prev_ref, mu_ref, w0_ref, wup_ref, a0_ref, aup_ref, gup_ref, kk_ref, ka_ref, rk_ref,
                   *outs, tiles_per_seq):
    heads_first = tiles_per_seq > 0
    p = p_ref[...]
    if heads_first:
        *outs, carry_ref = outs

        @pl.when(pl.program_id(0) % tiles_per_seq == 0)
        def _():
            carry_ref[...] = prev_ref[0]

        first_row = lax.broadcasted_iota(jnp.int32, p.shape, 0) == 0
        prev = jnp.where(first_row, carry_ref[...], pltpu.roll(p, 1, 0))
        carry_ref[...] = p[p.shape[0] - 1:p.shape[0], :]
    else:
        prev = prev_ref[...]
    pm = p + (prev - p) * mu_ref[...]
    d = RWKV_DIM
    r = pm[:, 0:d]
    k = pm[:, d:2 * d]
    v = pm[:, 2 * d:3 * d]
    xw = pm[:, 3 * d:3 * d + DECAY_LORA]
    xa = pm[:, 3 * d + DECAY_LORA:3 * d + DECAY_LORA + AAA_LORA]
    xg = pm[:, 3 * d + DECAY_LORA + AAA_LORA:RWKV_COLS]
    bd = _head_blockdiag(d)
    w_log = -_softplus(-(w0_ref[...] + _dot3(jnp.tanh(xw), wup_ref[...]))) - 0.5
    log_decay = -jnp.exp(w_log)
    a = _sigmoid(a0_ref[...] + _dot3(xa, aup_ref[...]))
    g = _dot3(_sigmoid(xg), gup_ref[...])
    kk = k * kk_ref[...]
    k = k * (1.0 + (a - 1.0) * ka_ref[...])
    kk = kk / jnp.maximum(jnp.sqrt(_dot_exact_rhs(kk * kk, bd)), 1e-12)
    steps = (r, log_decay, k, v, kk, kk * a)
    outs[-2][...] = g
    outs[-1][...] = _dot_exact_rhs(r * k * rk_ref[...], bd) * v
    if not heads_first:
        for ref, val in zip(outs[:6], steps):
            ref[...] = val
        return
    head = lambda h: slice(h * HEAD_DIM, (h + 1) * HEAD_DIM)
    for ref, val in zip(outs[:6], steps):
        for h in range(RWKV_HEADS):
            ref[h] = val[:, head(h)]
    for ref, val in zip(outs[6:9], (log_decay, kk, v)):
        val_t = val.T
        for h in range(RWKV_HEADS):
            ref[h] = val_t[head(h), :]


def _rwkv_pre(p, prev, mu, w0, w_up, a0, a_up, g_up, k_k, k_a, r_k, tm, seq_len=None):
    n = p.shape[0]
    vec = lambda a: a.reshape(1, -1)
    d = RWKV_DIM
    if seq_len is None:
        per_seq = 0
        prev_spec = _row_spec(tm, RWKV_COLS)
        scratch = []
        out_specs = [_row_spec(tm, d)] * 8
        out_shape = [jax.ShapeDtypeStruct((n, d), F32)] * 8
    else:
        per_seq = seq_len // tm
        prev = prev.reshape(-1, 1, RWKV_COLS)
        prev_spec = pl.BlockSpec((1, 1, RWKV_COLS), lambda i: (i // per_seq, 0, 0))
        scratch = [pltpu.VMEM((1, RWKV_COLS), F32)]
        chains = n // seq_len * RWKV_HEADS
        slab = pl.BlockSpec((RWKV_HEADS, tm, HEAD_DIM), lambda i: (i // per_seq, i % per_seq, 0))
        slab_t = pl.BlockSpec((RWKV_HEADS, HEAD_DIM, tm), lambda i: (i // per_seq, 0, i % per_seq))
        out_specs = [slab] * 6 + [slab_t] * 3 + [_row_spec(tm, d)] * 2
        out_shape = ([jax.ShapeDtypeStruct((chains, seq_len, HEAD_DIM), F32)] * 6
                     + [jax.ShapeDtypeStruct((chains, HEAD_DIM, seq_len), F32)] * 3
                     + [jax.ShapeDtypeStruct((n, d), F32)] * 2)
    return pl.pallas_call(
        functools.partial(_rwkv_pre_body, tiles_per_seq=per_seq),
        grid=(n // tm,),
        in_specs=[_row_spec(tm, RWKV_COLS), prev_spec, _const_spec((1, RWKV_COLS)),
                  _const_spec((1, d)), _const_spec((DECAY_LORA, d)), _const_spec((1, d)),
                  _const_spec((AAA_LORA, d)), _const_spec((GATE_LORA, d)),
                  _const_spec((1, d)), _const_spec((1, d)), _const_spec((1, d))],
        out_specs=out_specs,
        out_shape=out_shape,
        scratch_shapes=scratch,
        compiler_params=_cparams(("arbitrary",)),
        name="rwkv_pre",
    )(p, prev, vec(mu), vec(w0), w_up, vec(a0), a_up, g_up, vec(k_k), vec(k_a), vec(r_k))


def _wkv_body(r_ref, w_ref, k_ref, v_ref, kk_ref, b_ref, s0_ref, y_ref, sT_ref, s_scr, *, nc, tc):
    ti = pl.program_id(1)

    @pl.when(ti == 0)
    def _():
        s_scr[...] = s0_ref[...]

    eye = (lax.broadcasted_iota(jnp.int32, (HEAD_DIM, HEAD_DIM), 0)
           == lax.broadcasted_iota(jnp.int32, (HEAD_DIM, HEAD_DIM), 1))

    def step(t, carry):
        for c in range(nc):
            row = lambda ref: ref[c, pl.ds(t, 1), :]
            s = s_scr[c]
            sa = jnp.sum(s * row(kk_ref), axis=1, keepdims=True)
            vcol = jnp.sum(jnp.where(eye, row(v_ref), 0.0), axis=1, keepdims=True)
            s = s * jnp.exp(row(w_ref)) - sa * row(b_ref) + vcol * row(k_ref)
            s_scr[c] = s
            ycol = jnp.sum(s * row(r_ref), axis=1, keepdims=True)
            y_ref[c, pl.ds(t, 1), :] = jnp.sum(jnp.where(eye, ycol, 0.0), axis=0, keepdims=True)
        return carry

    lax.fori_loop(0, tc, step, 0)

    @pl.when(ti == pl.num_programs(1) - 1)
    def _():
        sT_ref[...] = s_scr[...]


def _wkv(r, w, k, v, kk, b, s0, nc, tc):
    n, t, _ = r.shape
    seq = pl.BlockSpec((nc, tc, HEAD_DIM), lambda i, j: (i, j, 0))
    st = pl.BlockSpec((nc, HEAD_DIM, HEAD_DIM), lambda i, j: (i, 0, 0))
    return pl.pallas_call(
        functools.partial(_wkv_body, nc=nc, tc=tc),
        grid=(n // nc, t // tc),
        in_specs=[seq] * 6 + [st],
        out_specs=[seq, st],
        out_shape=[jax.ShapeDtypeStruct((n, t, HEAD_DIM), F32),
                   jax.ShapeDtypeStruct((n, HEAD_DIM, HEAD_DIM), F32)],
        scratch_shapes=[pltpu.VMEM((nc, HEAD_DIM, HEAD_DIM), F32)],
        compiler_params=_cparams(("arbitrary", "arbitrary")),
        name="wkv",
    )(r, w, k, v, kk, b, s0)


def _mm(a, b):
    return _dot3(a, b)


def _mm_nt(a, b):
    ah, al = _split2(a)
    bh, bl = _split2(b)
    return _dot_nt(ah, bh) + (_dot_nt(ah, bl) + _dot_nt(al, bh))


def _split3(x):
    p1 = x.astype(BF16)
    r1 = x - p1.astype(F32)
    p2 = r1.astype(BF16)
    p3 = (r1 - p2.astype(F32)).astype(BF16)
    return p1, p2, p3


def _wkv_chunk_body(r_ref, lw_ref, k_ref, v_ref, kk_ref, b_ref, lwt_ref, kkt_ref, vt_ref, s0_ref,
                    y_ref, sT_ref, s_scr, *, nc):
    ti = pl.program_id(1)
    c_len = WKV_CHUNK

    @pl.when(ti == 0)
    def _():
        s_scr[...] = s0_ref[...]

    row = lax.broadcasted_iota(jnp.int32, (c_len, c_len), 0)
    col = lax.broadcasted_iota(jnp.int32, (c_len, c_len), 1)
    strict = row > col
    incl = row >= col
    lower = incl.astype(BF16)
    upper = (row <= col).astype(BF16)
    eye = (row == col).astype(F32)

    n_sub = WKV_BLOCK // c_len
    units = [(j, c) for j in range(n_sub) for c in range(nc)]
    each = lambda f, *lists: [f(*args) for args in zip(*lists)]
    ts = lambda j: slice(j * c_len, (j + 1) * c_len)
    rows = lambda ref: [ref[c, ts(j), :] for j, c in units]
    cols = lambda ref: [ref[c, :, ts(j)] for j, c in units]
    r, lw, k, v, kk, b = (rows(ref) for ref in (r_ref, lw_ref, k_ref, v_ref, kk_ref, b_ref))
    lwt, kkt, vt = (cols(ref) for ref in (lwt_ref, kkt_ref, vt_ref))
    cum = each(lambda x: sum(_dot(lower, piece) for piece in _split3(x)), lw)
    cumt = each(lambda x: sum(_dot(piece, upper) for piece in _split3(x)), lwt)
    w_incl = each(jnp.exp, cum)
    w_last = each(lambda w: w[c_len - 1:c_len, :], w_incl)
    inv = each(lambda x: jnp.exp(-x), cum)
    qt = each(lambda x, cs, l: x * jnp.exp(cs - l), kk, cum, lw)
    qtt = each(lambda x, cs, l: x * jnp.exp(cs - l), kkt, cumt, lwt)
    rt = each(jnp.multiply, r, w_incl)
    kt = each(jnp.multiply, k, inv)
    bt = each(jnp.multiply, b, inv)
    a_qb = each(lambda x, y: jnp.where(strict, _mm_nt(x, y), 0.0), qt, bt)
    a_qk = each(lambda x, y: jnp.where(strict, _mm_nt(x, y), 0.0), qt, kt)
    a_rk = each(lambda x, y: jnp.where(incl, _mm_nt(x, y), 0.0), rt, kt)
    a_rb = each(lambda x, y: jnp.where(incl, _mm_nt(x, y), 0.0), rt, bt)
    tinv = each(lambda n: eye - n, a_qb)
    power = a_qb
    for _ in range(WKV_CHUNK_LOG2 - 1):
        power = each(_mm, power, power)
        tinv = each(lambda t, p: t + _mm(t, p), tinv, power)
    tq = each(_mm, tinv, qt)
    av = each(_mm, a_qk, v)
    av_t = each(_mm_nt, vt, a_qk)
    tq_t = each(_mm_nt, qtt, tinv)
    tav = each(_mm, tinv, av)
    tav_t = each(_mm_nt, av_t, tinv)
    ark_v = each(_mm, a_rk, v)
    rq = each(lambda x, a, y: x - _mm(a, y), rt, a_rb, tq)
    yv = each(lambda x, a, y: x - _mm(a, y), ark_v, a_rb, tav)
    gain = each(lambda x, y, w: _mm(x, y * w), vt, kt, w_last)
    bw = each(jnp.multiply, bt, w_last)
    for j in range(n_sub):
        ids = [j * nc + c for c in range(nc)]
        s = [s_scr[c] for c in range(nc)]
        sa_t = [_mm(s[c], tq_t[i]) + tav_t[i] for c, i in enumerate(ids)]
        ys = [_mm_nt(rq[i], s[c]) + yv[i] for c, i in enumerate(ids)]
        new = [s[c] * w_last[i] + gain[i] - _mm(sa_t[c], bw[i]) for c, i in enumerate(ids)]
        for c, i in enumerate(ids):
            y_ref[c, ts(j), :] = ys[c]
            s_scr[c] = new[c]

    @pl.when(ti == pl.num_programs(1) - 1)
    def _():
        sT_ref[...] = s_scr[...]


def _wkv_chunked(r, lw, k, v, kk, b, lwt, kkt, vt, s0, nc):
    n, t, _ = r.shape
    seq = pl.BlockSpec((nc, WKV_BLOCK, HEAD_DIM), lambda i, j: (i, j, 0))
    seq_t = pl.BlockSpec((nc, HEAD_DIM, WKV_BLOCK), lambda i, j: (i, 0, j))
    st = pl.BlockSpec((nc, HEAD_DIM, HEAD_DIM), lambda i, j: (i, 0, 0))
    return pl.pallas_call(
        functools.partial(_wkv_chunk_body, nc=nc),
        grid=(n // nc, t // WKV_BLOCK),
        in_specs=[seq] * 6 + [seq_t] * 3 + [st],
        out_specs=[seq, st],
        out_shape=[jax.ShapeDtypeStruct((n, t, HEAD_DIM), F32),
                   jax.ShapeDtypeStruct((n, HEAD_DIM, HEAD_DIM), F32)],
        scratch_shapes=[pltpu.VMEM((nc, HEAD_DIM, HEAD_DIM), F32)],
        compiler_params=_cparams(("arbitrary", "arbitrary")),
        name="wkv_chunked",
    )(r, lw, k, v, kk, b, lwt, kkt, vt, s0)


def _sb_block(q, kb, vb, bias, rest, umat, mask):
    z = _dot_nt(q, kb) + bias
    sp = _softplus(z)
    counted = sp if mask is None else jnp.where(mask, sp, 0.0)
    after = _dot(counted.astype(BF16), umat)
    a = jnp.exp((z - sp) + after + rest)
    if mask is not None:
        a = jnp.where(mask, a, 0.0)
    out = _dot(a.astype(BF16), vb)
    return out, after[:, 0:1] - counted[:, 0:1]


def _attn_prompt_body(q_ref, k_ref, v_ref, bias_ref, o_ref, *, tile):
    qi = pl.program_id(1)
    heads = range(ATT_HEADS)
    hs = lambda a, h: a[:, h * HEAD_DIM:(h + 1) * HEAD_DIM]
    q_all = q_ref[0]
    qs = [hs(q_all, h) for h in heads]
    biases = [bias_ref[h] for h in heads]
    umat = _suffix_matrix(tile)
    causal = (lax.broadcasted_iota(jnp.int32, (tile, tile), 1)
              < lax.broadcasted_iota(jnp.int32, (tile, tile), 0))

    def visit(j, accs, rests, mask):
        start = pl.multiple_of(j * tile, tile)
        kb = k_ref[0, pl.ds(start, tile), :]
        vb = v_ref[0, pl.ds(start, tile), :]
        zs = [_dot_nt(qs[h], hs(kb, h)) + biases[h] for h in heads]
        sps = [_softplus(z) for z in zs]
        counted = sps if mask is None else [jnp.where(mask, sp, 0.0) for sp in sps]
        afters = [_dot(sp.astype(BF16), umat) for sp in counted]
        probs = [jnp.exp((zs[h] - sps[h]) + afters[h] + rests[h]) for h in heads]
        if mask is not None:
            probs = [jnp.where(mask, a, 0.0) for a in probs]
        outs = [_dot(probs[h].astype(BF16), hs(vb, h)) for h in heads]
        new_accs = tuple(accs[h] + outs[h] for h in heads)
        new_rests = tuple(rests[h] + afters[h][:, 0:1] - counted[h][:, 0:1] for h in heads)
        return new_accs, new_rests

    accs = tuple(jnp.zeros((tile, HEAD_DIM), F32) for _ in heads)
    rests = tuple(jnp.zeros((tile, 1), F32) for _ in heads)
    accs, rests = visit(qi, accs, rests, causal)
    accs, rests = lax.fori_loop(0, qi, lambda jj, c: visit(qi - 1 - jj, c[0], c[1], None), (accs, rests))
    r = lax.broadcasted_iota(jnp.int32, (HEAD_DIM, ATT_DIM), 0)
    c = lax.broadcasted_iota(jnp.int32, (HEAD_DIM, ATT_DIM), 1)
    out = jnp.zeros((tile, ATT_DIM), F32)
    for h in heads:
        out = out + _dot(accs[h].astype(BF16), (c == r + h * HEAD_DIM).astype(BF16))
    o_ref[0] = out.astype(o_ref.dtype)


def _attn_prompt(qb, kb, vb, bias):
    b, t, d = qb.shape
    tile = ATT_TILE
    bias_b = jnp.broadcast_to(bias.astype(F32)[:, None, None], (ATT_HEADS, 1, tile))
    qspec = pl.BlockSpec((1, tile, d), lambda i, l: (i, l, 0))
    kvspec = pl.BlockSpec((1, t, d), lambda i, l: (i, 0, 0))
    return pl.pallas_call(
        functools.partial(_attn_prompt_body, tile=tile),
        grid=(b, t // tile),
        in_specs=[qspec, kvspec, kvspec, _const_spec((ATT_HEADS, 1, tile))],
        out_specs=qspec,
        out_shape=jax.ShapeDtypeStruct((b, t, d), BF16),
        compiler_params=_cparams(("arbitrary", "arbitrary")),
        name="attn_prompt",
    )(qb, kb, vb, bias_b)


def _attn_decode_body(pt_ref, q_ref, bias_ref, kn_ref, vn_ref, *refs, npg, nq):
    k_refs = refs[:npg]
    v_refs = refs[npg:2 * npg]
    o_ref, rest_ref, acc_ref = refs[2 * npg:]
    jb = pl.program_id(1)
    rows = ATT_HEADS * Q_PAD
    pages = range(npg)
    spread = lambda tot: jnp.broadcast_to(tot, (rows, PAGE_SIZE))
    q = q_ref[0]
    bias = bias_ref[...]
    umat = _suffix_matrix(PAGE_SIZE)

    @pl.when(jb == 0)
    def _():
        kidx = lax.broadcasted_iota(jnp.int32, (rows, PAGE_SIZE), 1)
        qidx = lax.broadcasted_iota(jnp.int32, (rows, PAGE_SIZE), 0) % Q_PAD
        out, tot = _sb_block(q, kn_ref[0].astype(BF16), vn_ref[0].astype(BF16), bias,
                             jnp.zeros((rows, 1), F32), umat, (kidx < qidx) & (kidx < nq))
        acc_ref[...] = out
        rest_ref[...] = spread(tot)

    zs = [_dot_nt(q, k_refs[s][0, 0].astype(BF16)) + bias for s in pages]
    sps = [_softplus(z) for z in zs]
    afters = [_dot(sp.astype(BF16), umat) for sp in sps]
    rest = rest_ref[...]
    probs = []
    for s in pages:
        probs.append(jnp.exp((zs[s] - sps[s]) + afters[s] + rest).astype(BF16))
        rest = rest + spread(afters[s][:, 0:1] - sps[s][:, 0:1])
    rest_ref[...] = rest
    out = _dot(probs[0], v_refs[0][0, 0].astype(BF16))
    for s in pages[1:]:
        out = out + _dot(probs[s], v_refs[s][0, 0].astype(BF16))
    acc_ref[...] += out

    @pl.when(jb == pl.num_programs(1) - 1)
    def _():
        head_of_row = lax.broadcasted_iota(jnp.int32, (rows, ATT_DIM), 0) // Q_PAD
        head_of_col = lax.broadcasted_iota(jnp.int32, (rows, ATT_DIM), 1) // HEAD_DIM
        kept = jnp.where(head_of_row == head_of_col, acc_ref[...], 0.0)
        out = kept[0:Q_PAD]
        for h in range(1, ATT_HEADS):
            out = out + kept[h * Q_PAD:(h + 1) * Q_PAD]
        o_ref[0] = out.astype(o_ref.dtype)


def _attn_decode(li, qb, k_new, v_new, cache_k, cache_v, page_table, bias):
    b, t, _ = qb.shape
    n_pages = page_table.shape[1]
    npg = PAGES_PER_STEP
    rows = ATT_HEADS * Q_PAD
    qpad = jnp.pad(qb, ((0, 0), (0, Q_PAD - t), (0, 0)))
    head_of_col = jnp.arange(ATT_DIM) // HEAD_DIM
    qbd = jnp.where(head_of_col[None, None, None, :] == jnp.arange(ATT_HEADS)[None, :, None, None],
                    qpad[:, None], jnp.zeros((), BF16)).reshape(b, rows, ATT_DIM)
    bias_b = jnp.broadcast_to(jnp.repeat(bias.astype(F32), Q_PAD)[:, None], (rows, PAGE_SIZE))
    kn = jnp.pad(k_new, ((0, 0), (0, PAGE_SIZE - t), (0, 0)))
    vn = jnp.pad(v_new, ((0, 0), (0, PAGE_SIZE - t), (0, 0)))

    def page_spec(s):
        return pl.BlockSpec((1, 1, PAGE_SIZE, ATT_DIM),
                            lambda i, j, pt: (li, pt[i * n_pages + n_pages - 1 - (j * npg + s)], 0, 0))

    per_b = lambda r, c: pl.BlockSpec((1, r, c), lambda i, j, pt: (i, 0, 0))
    grid_spec = pltpu.PrefetchScalarGridSpec(
        num_scalar_prefetch=1,
        grid=(b, n_pages // npg),
        in_specs=[per_b(rows, ATT_DIM), pl.BlockSpec((rows, PAGE_SIZE), lambda i, j, pt: (0, 0)),
                  per_b(PAGE_SIZE, ATT_DIM), per_b(PAGE_SIZE, ATT_DIM)]
                 + [page_spec(s) for s in range(npg)] * 2,
        out_specs=per_b(Q_PAD, ATT_DIM),
        scratch_shapes=[pltpu.VMEM((rows, PAGE_SIZE), F32), pltpu.VMEM((rows, ATT_DIM), F32)],
    )
    out = pl.pallas_call(
        functools.partial(_attn_decode_body, npg=npg, nq=t),
        grid_spec=grid_spec,
        out_shape=jax.ShapeDtypeStruct((b, Q_PAD, ATT_DIM), BF16),
        compiler_params=_cparams(("arbitrary", "arbitrary")),
        name="attn_decode",
    )(page_table.reshape(-1), qbd, bias_b, kn, vn, *([cache_k] * npg), *([cache_v] * npg))
    return out[:, :t]


def _outproj_body(x_ref, yc_ref, yw_ref, bonus_ref, g_ref, ya_ref, lng_ref, lnb_ref, w_ref, o_ref):
    bd = _head_blockdiag(RWKV_DIM)
    if len(yw_ref.shape) == 3:
        r = lax.broadcasted_iota(jnp.int32, (HEAD_DIM, RWKV_DIM), 0)
        c = lax.broadcasted_iota(jnp.int32, (HEAD_DIM, RWKV_DIM), 1)
        y = sum(_dot_exact_rhs(yw_ref[h], (c == r + h * HEAD_DIM).astype(BF16)) for h in range(RWKV_HEADS))
    else:
        y = yw_ref[...]
    inv = 1.0 / HEAD_DIM
    d = y - _dot_exact_rhs(y, bd) * inv
    var = _dot_exact_rhs(d * d, bd) * inv
    yn = d * lax.rsqrt(var + GN_EPS) * lng_ref[...] + lnb_ref[...]
    yr = ((yn + bonus_ref[...]) * g_ref[...]).astype(BF16)
    c0, c1 = CONV_CH, CONV_CH + RWKV_DIM
    o_ref[...] = (x_ref[...] + _dot(yc_ref[...], w_ref[0:c0, :]) + _dot(yr, w_ref[c0:c1, :])
                  + _dot(ya_ref[...], w_ref[c1:D_MODEL, :]))


def _outproj(x, yc, yw, bonus, g, ya, ln_g, ln_b, w_b, tm):
    n = x.shape[0]
    if yw.ndim == 3:
        per_seq = yw.shape[1] // tm
        yw_spec = pl.BlockSpec((RWKV_HEADS, tm, HEAD_DIM), lambda i: (i // per_seq, i % per_seq, 0))
    else:
        yw_spec = _row_spec(tm, RWKV_DIM)
    return pl.pallas_call(
        _outproj_body,
        grid=(n // tm,),
        in_specs=[_row_spec(tm, D_MODEL), _row_spec(tm, CONV_CH), yw_spec, _row_spec(tm, RWKV_DIM),
                  _row_spec(tm, RWKV_DIM), _row_spec(tm, ATT_DIM), _const_spec((1, RWKV_DIM)),
                  _const_spec((1, RWKV_DIM)), _const_spec((D_MODEL, D_MODEL))],
        out_specs=_row_spec(tm, D_MODEL),
        out_shape=jax.ShapeDtypeStruct((n, D_MODEL), F32),
        compiler_params=_cparams(("arbitrary",)),
        name="outproj",
    )(x, yc, yw, bonus, g, ya, ln_g.reshape(1, -1), ln_b.reshape(1, -1), w_b)


SUBLANES = 8
KEY_GROUPS = N_KEYS // SUBLANES


def _oddeven_merge(lo, hi, r):
    step = r * 2
    if step < hi - lo:
        yield from _oddeven_merge(lo, hi, step)
        yield from _oddeven_merge(lo + r, hi, step)
        yield from [(i, i + r) for i in range(lo + r, hi - r, step)]
    else:
        yield (lo, lo + r)


def _oddeven_sort(lo, hi):
    if hi - lo >= 1:
        mid = lo + (hi - lo) // 2
        yield from _oddeven_sort(lo, mid)
        yield from _oddeven_sort(mid + 1, hi)
        yield from _oddeven_merge(lo, hi, 1)


SORT_NET = tuple(_oddeven_sort(0, PEER_TOPK - 1))
BITONIC_NET = tuple((i, i + s) for s in (8, 4, 2, 1) for i in range(PEER_TOPK) if not i & s)
SUM_SLOTS = tuple((a, b) for a in range(PEER_TOPK) for b in range(PEER_TOPK) if (a + 1) * (b + 1) <= PEER_TOPK)


def _exchange(vals, i, j):
    vals[i], vals[j] = jnp.maximum(vals[i], vals[j]), jnp.minimum(vals[i], vals[j])


def _sublane_max(x):
    for shift in (4, 2, 1):
        x = jnp.maximum(x, pltpu.roll(x, shift, 0))
    return x


def _top16_sorted(groups):
    vals = list(groups)
    for i, j in SORT_NET:
        _exchange(vals, i, j)
    for shift in (4, 2, 1):
        moved = [pltpu.roll(v, shift, 0) for v in vals]
        vals = [jnp.maximum(vals[i], moved[PEER_TOPK - 1 - i]) for i in range(PEER_TOPK)]
        for i, j in BITONIC_NET:
            _exchange(vals, i, j)
    return vals


def _peer_route_body(x_ref, g_ref, wq_ref, sk_ref, h_out, rank_out, e2_out, cnt_out, f1_out):
    tm = x_ref.shape[0]
    kk = PEER_TOPK
    hb = _rmsnorm(x_ref[...], g_ref[...]).astype(BF16)
    h_out[...] = hb
    sub = lax.broadcasted_iota(jnp.int32, (SUBLANES, tm), 0)
    split = lambda s: [s[i * SUBLANES:(i + 1) * SUBLANES, :] for i in range(KEY_GROUPS)]
    join = lambda parts: jnp.concatenate(parts, axis=0)
    neg_inf = jnp.full((SUBLANES, tm), -jnp.inf, F32)
    half = PEER_DK // 2
    for h in range(PEER_HEADS):
        q = _dot(hb, wq_ref[:, h * PEER_DK:(h + 1) * PEER_DK])
        s1 = split(_dot_nt(sk_ref[2 * h], q[:, :half].astype(BF16)))
        s2 = split(_dot_nt(sk_ref[2 * h + 1], q[:, half:].astype(BF16)))
        v1 = _top16_sorted(s1)
        v2 = _top16_sorted(s2)
        sums = {ab: v1[ab[0]] + v2[ab[1]] for ab in SUM_SLOTS}
        packs = []
        for j in range(0, len(SUM_SLOTS), SUBLANES):
            pack = neg_inf
            for s, ab in enumerate(SUM_SLOTS[j:j + SUBLANES]):
                pack = jnp.where(sub == s, sums[ab], pack)
            packs.append(pack)
        best = []
        for _ in range(kk):
            m = packs[0]
            for p in packs[1:]:
                m = jnp.maximum(m, p)
            m = _sublane_max(m)
            best.append(m)
            packs = [jnp.where(p == m, -jnp.inf, p) for p in packs]
        z = jnp.ones((SUBLANES, tm), F32)
        for i in range(1, kk):
            z = z + jnp.exp(best[i] - best[0])
        inv_z = 1.0 / z
        cnt = [jnp.zeros((SUBLANES, tm), F32) for _ in range(kk)]
        for a, b in SUM_SLOTS:
            cnt[a] = cnt[a] + jnp.where(sums[(a, b)] >= best[kk - 1], 1.0, 0.0)
        cnt_full, f1_full, rank_full, e2_full = [], [], [], []
        for g in range(KEY_GROUPS):
            c = jnp.zeros((SUBLANES, tm), F32)
            for a in range(kk):
                c = jnp.where(s1[g] == v1[a], cnt[a], c)
            cnt_full.append(c)
            f1_full.append(jnp.where(s1[g] >= v1[kk - 1], jnp.exp(s1[g] - v1[0]), 0.0) * inv_z)
            rk = jnp.full((SUBLANES, tm), float(kk), F32)
            for b in reversed(range(kk)):
                rk = jnp.where(s2[g] == v2[b], float(b), rk)
            rank_full.append(rk)
            e2_full.append(jnp.where(s2[g] >= v2[kk - 1], jnp.exp(s2[g] - v2[0]), 0.0))
        cnt_out[h] = join(cnt_full)
        f1_out[h] = join(f1_full)
        rank_out[h] = join(rank_full).astype(BF16)
        e2_out[h] = join(e2_full).astype(BF16)


def _peer_route(x, g, wq_b, sk_b, tm):
    n = x.shape[0]
    keyed = pl.BlockSpec((PEER_HEADS, N_KEYS, tm), lambda i: (0, 0, i))
    keyed_shape = lambda dt: jax.ShapeDtypeStruct((PEER_HEADS, N_KEYS, n), dt)
    return pl.pallas_call(
        _peer_route_body,
        grid=(n // tm,),
        in_specs=[_row_spec(tm, D_MODEL), _const_spec((1, D_MODEL)), _const_spec((D_MODEL, PEER_HEADS * PEER_DK)),
                  _const_spec((2 * PEER_HEADS, N_KEYS, PEER_DK // 2))],
        out_specs=[_row_spec(tm, D_MODEL), keyed, keyed, keyed, keyed],
        out_shape=[jax.ShapeDtypeStruct((n, D_MODEL), BF16), keyed_shape(BF16), keyed_shape(BF16),
                   keyed_shape(F32), keyed_shape(F32)],
        compiler_params=_cparams(("arbitrary",)),
        name="peer_route",
    )(x, g.reshape(1, -1), wq_b, sk_b)


GATE_ROWS = 16


def _gelu_tanh(x):
    return 0.5 * x * (1.0 + jnp.tanh(0.7978845608028654 * (x + 0.044715 * (x * x * x))))


def _peer_expert_body(x_ref, h_ref, rank_ref, e2_ref, cnt_ref, f1_ref, u_ref, vt_ref, o_ref, acc_ref):
    ec = pl.program_id(1)

    @pl.when(ec == 0)
    def _():
        acc_ref[...] = jnp.zeros(acc_ref.shape, F32)

    tm = h_ref.shape[0]
    n_grp = 2 if tm % 512 == 0 else 1
    grp = tm // n_grp
    toks = [slice(g * grp, (g + 1) * grp) for g in range(n_grp)]
    hids = [_dot_nt(u_ref[...], h_ref[tk, :]) for tk in toks]
    heads = range(PEER_HEADS)
    for g, tk in enumerate(toks):
        weights = []
        for r in range(PEER_ROWS):
            bcast = lambda ref, h: jnp.broadcast_to(ref[h, 0, r:r + 1, tk], (GATE_ROWS, grp)).astype(BF16)
            cnts = [bcast(cnt_ref, h) for h in heads]
            f1s = [bcast(f1_ref, h) for h in heads]
            for j in range(N_KEYS // GATE_ROWS):
                keys = slice(j * GATE_ROWS, (j + 1) * GATE_ROWS)
                gate = jnp.zeros((GATE_ROWS, grp), BF16)
                for h in heads:
                    e2 = e2_ref[h, keys, tk]
                    gate = gate + jnp.where(rank_ref[h, keys, tk] < cnts[h], e2, jnp.zeros_like(e2)) * f1s[h]
                rows = slice(r * N_KEYS + j * GATE_ROWS, r * N_KEYS + (j + 1) * GATE_ROWS)
                weights.append(gate * _gelu_tanh(hids[g][rows]).astype(BF16))
        acc_ref[:, tk] += _dot(vt_ref[...], jnp.concatenate(weights, axis=0))

    @pl.when(ec == pl.num_programs(1) - 1)
    def _():
        o_ref[...] = x_ref[...] + acc_ref[...].T


def _peer_expert(x, hb, rank2, e2, cnt, f1, u_b, vt_b, tm):
    n = x.shape[0]
    te = PEER_ROWS * N_KEYS
    ne = u_b.shape[0] // te
    chunked = lambda a: a.reshape(PEER_HEADS, ne, PEER_ROWS, n)
    tok = lambda c: pl.BlockSpec((tm, c), lambda i, e: (i, 0))
    keyed = pl.BlockSpec((PEER_HEADS, N_KEYS, tm), lambda i, e: (0, 0, i))
    row = pl.BlockSpec((PEER_HEADS, 1, PEER_ROWS, tm), lambda i, e: (0, e, 0, i))
    return pl.pallas_call(
        _peer_expert_body,
        grid=(n // tm, ne),
        in_specs=[tok(D_MODEL), tok(D_MODEL), keyed, keyed, row, row,
                  pl.BlockSpec((te, D_MODEL), lambda i, e: (e, 0)),
                  pl.BlockSpec((D_MODEL, te), lambda i, e: (0, e))],
        out_specs=tok(D_MODEL),
        out_shape=jax.ShapeDtypeStruct((n, D_MODEL), F32),
        scratch_shapes=[pltpu.VMEM((D_MODEL, tm), F32)],
        compiler_params=_cparams(("arbitrary", "arbitrary")),
        name="peer_expert",
    )(x, hb, rank2, e2, chunked(cnt), chunked(f1), u_b, vt_b)


def _final_norm_body(x_ref, g_ref, o_ref):
    o_ref[...] = _rmsnorm(x_ref[...], g_ref[...])


def _final_norm(x, g, tm):
    n = x.shape[0]
    return pl.pallas_call(
        _final_norm_body,
        grid=(n // tm,),
        in_specs=[_row_spec(tm, D_MODEL), _const_spec((1, D_MODEL))],
        out_specs=_row_spec(tm, D_MODEL),
        out_shape=jax.ShapeDtypeStruct((n, D_MODEL), F32),
        compiler_params=_cparams(("arbitrary",)),
        name="final_norm",
    )(x, g.reshape(1, -1))


def _heads_first(a, b, t):
    h = a.shape[1] // HEAD_DIM
    return a.reshape(b, t, h, HEAD_DIM).transpose(0, 2, 1, 3).reshape(b * h, t, HEAD_DIM)


def _tokens_first(a, b, t):
    h = a.shape[0] // b
    return a.reshape(b, h, t, HEAD_DIM).transpose(0, 2, 1, 3).reshape(b * t, h * HEAD_DIM)


def _layer(x, b, t, conv_buf, shift_prev, wkv0, attend, lw, tm, conv_tile, wkv_chains, wkv_tile):
    p_conv, p_rwkv, k, v, qb, kb, vb = _inproj(x, lw["norm1_g"], lw["w_in"], tm)

    y_conv, conv_new = _conv_mixer(p_conv.reshape(b, t, CONV_COLS), conv_buf, lw["conv_w"], lw["conv_b"],
                                   lw["conv_ln_g"], lw["conv_ln_b"], conv_tile)

    p3 = p_rwkv.reshape(b, t, RWKV_COLS)
    pre = lambda prev, **kw: _rwkv_pre(p_rwkv, prev, lw["rwkv_mu"], lw["rwkv_w0"], lw["rwkv_w_up"], lw["rwkv_a0"],
                                       lw["rwkv_a_up"], lw["rwkv_g_up"], lw["rwkv_k_k"], lw["rwkv_k_a"],
                                       lw["rwkv_r_k"], tm, **kw)
    s0 = wkv0.reshape(b * RWKV_HEADS, HEAD_DIM, HEAD_DIM)
    if t % WKV_BLOCK == 0 and t % tm == 0:
        *steps, g, bonus = pre(shift_prev, seq_len=t)
        y_wkv, wkv_new = _wkv_chunked(*steps, s0, wkv_chains)
    else:
        prev = jnp.concatenate([shift_prev[:, None, :], p3[:, :-1]], axis=1).reshape(b * t, RWKV_COLS)
        *steps, g, bonus = pre(prev)
        y_wkv, wkv_new = _wkv(*(_heads_first(a, b, t) for a in steps), s0, wkv_chains, wkv_tile)
        y_wkv = _tokens_first(y_wkv, b, t)

    y_att = attend(qb, k, v, kb, vb)

    x = _outproj(x, y_conv.reshape(b * t, CONV_CH), y_wkv, bonus, g, y_att, lw["rwkv_ln_g"], lw["rwkv_ln_b"],
                 lw["w_out"], tm)
    routed = _peer_route(x, lw["norm2_g"], lw["peer_w_query"], lw["peer_sub_keys"], min(tm, 256))
    x = _peer_expert(x, *routed, lw["peer_u"], lw["peer_vt"], tm)
    states = (k.reshape(b, t, ATT_HEADS, HEAD_DIM), v.reshape(b, t, ATT_HEADS, HEAD_DIM), conv_new,
              p3[:, -1], wkv_new.reshape(b, RWKV_HEADS, HEAD_DIM, HEAD_DIM))
    return x, states


def kernel(x_prompt, x_sample, cache_k, cache_v, state_conv, state_shift, state_wkv, page_table, norm1_g, w_in, conv_w, conv_b, conv_ln_g, conv_ln_b, rwkv_mu, rwkv_w0, rwkv_w_up, rwkv_a0, rwkv_a_up, rwkv_g_up, rwkv_k_k, rwkv_k_a, rwkv_r_k, rwkv_ln_g, rwkv_ln_b, att_bias, w_out, norm2_g, peer_w_query, peer_sub_keys, peer_u, peer_v, final_g):
    depth = w_in.shape[0]
    bp, tp, _ = x_prompt.shape
    bs, ts, _ = x_sample.shape
    n_pool = cache_k.shape[1]
    ck = cache_k.reshape(depth, n_pool, PAGE_SIZE, ATT_DIM)
    cv = cache_v.reshape(depth, n_pool, PAGE_SIZE, ATT_DIM)
    xp = x_prompt.reshape(bp * tp, D_MODEL)
    xs = x_sample.reshape(bs * ts, D_MODEL)
    st_p, st_s = [], []
    for li in range(depth):
        lw = dict(
            norm1_g=norm1_g[li], w_in=w_in[li].astype(BF16), conv_w=conv_w[li], conv_b=conv_b[li],
            conv_ln_g=conv_ln_g[li], conv_ln_b=conv_ln_b[li], rwkv_mu=rwkv_mu[li], rwkv_w0=rwkv_w0[li],
            rwkv_w_up=rwkv_w_up[li], rwkv_a0=rwkv_a0[li], rwkv_a_up=rwkv_a_up[li], rwkv_g_up=rwkv_g_up[li],
            rwkv_k_k=rwkv_k_k[li], rwkv_k_a=rwkv_k_a[li], rwkv_r_k=rwkv_r_k[li].reshape(-1),
            rwkv_ln_g=rwkv_ln_g[li], rwkv_ln_b=rwkv_ln_b[li], w_out=w_out[li].astype(BF16), norm2_g=norm2_g[li],
            peer_w_query=peer_w_query[li].astype(BF16),
            peer_sub_keys=peer_sub_keys[li].astype(BF16).reshape(2 * PEER_HEADS, N_KEYS, PEER_DK // 2),
            peer_u=peer_u[li].astype(BF16), peer_vt=peer_v[li].astype(BF16).T)
        bias = att_bias[li]

        def attend_prompt(qb, k, v, kb, vb):
            sq = lambda a: a.reshape(bp, tp, ATT_DIM)
            return _attn_prompt(sq(qb), sq(kb), sq(vb), bias).reshape(bp * tp, ATT_DIM)

        def attend_sample(qb, k, v, kb, vb):
            o = _attn_decode(li, qb.reshape(bs, ts, ATT_DIM), k.reshape(bs, ts, ATT_DIM),
                             v.reshape(bs, ts, ATT_DIM), ck, cv, page_table, bias)
            return o.reshape(bs * ts, ATT_DIM)

        xp, new_p = _layer(xp, bp, tp, jnp.zeros((bp, CONV_WIDTH - 1, CONV_CH), F32),
                           jnp.zeros((bp, RWKV_COLS), F32), jnp.zeros((bp, RWKV_HEADS, HEAD_DIM, HEAD_DIM), F32),
                           attend_prompt, lw, tm=512, conv_tile=512, wkv_chains=RWKV_HEADS, wkv_tile=256)
        xs, new_s = _layer(xs, bs, ts, state_conv[li], state_shift[li], state_wkv[li],
                           attend_sample, lw, tm=bs * ts, conv_tile=ts, wkv_chains=bp * RWKV_HEADS, wkv_tile=ts)
        st_p.append(new_p)
        st_s.append(new_s)
    y_prompt = _final_norm(xp, final_g, 512).reshape(bp, tp, D_MODEL)
    y_sample = _final_norm(xs, final_g, bs * ts).reshape(bs, ts, D_MODEL)
    stack = lambda st, j: jnp.stack([s[j] for s in st], axis=0)
    return (y_prompt, y_sample) + tuple(stack(st_p, j) for j in range(5)) + tuple(stack(st_s, j) for j in range(5))
```

```python
import functools

import jax
import jax.numpy as jnp
from jax import lax
from jax.experimental import pallas as pl
from jax.experimental.pallas import tpu as pltpu

F32 = jnp.float32
BF16 = jnp.bfloat16

D_MODEL = 1024
HEAD_DIM = 64
CONV_CH = 256
CONV_WIDTH = 31
CONV_HALO = 32
RWKV_HEADS = 6
RWKV_DIM = 384
ATT_HEADS = 6
ATT_DIM = 384
DECAY_LORA = 64
AAA_LORA = 64
GATE_LORA = 128
CONV_COLS = 512
RWKV_COLS = 1408
IN_COLS = 3072
PAGE_SIZE = 128
PEER_HEADS = 8
PEER_DK = 256
N_KEYS = 128
PEER_TOPK = 16
RMS_EPS = 1e-5
LN_EPS = 1e-5
GN_EPS = 64e-5

ATT_TILE = 256
PAGES_PER_STEP = 16
Q_PAD = 8
PEER_ROWS = 16
WKV_CHUNK_LOG2 = 6
WKV_CHUNK = 1 << WKV_CHUNK_LOG2
WKV_BLOCK = 128
VMEM_LIMIT = 56 * 1024 * 1024


def _cparams(sem):
    return pltpu.CompilerParams(dimension_semantics=sem, vmem_limit_bytes=VMEM_LIMIT)


def _sigmoid(x):
    return 1.0 / (1.0 + jnp.exp(-x))


def _softplus(x):
    return jnp.maximum(x, 0.0) + jnp.log(1.0 + jnp.exp(-jnp.abs(x)))


def _dot(a, b):
    return jnp.dot(a, b, preferred_element_type=F32)


def _dot_nt(a, b):
    return lax.dot_general(a, b, (((1,), (1,)), ((), ())), preferred_element_type=F32)


def _split2(x):
    hi = x.astype(BF16)
    lo = (x - hi.astype(F32)).astype(BF16)
    return hi, lo


def _dot3(a, b):
    ah, al = _split2(a)
    bh, bl = _split2(b)
    return _dot(ah, bh) + (_dot(ah, bl) + _dot(al, bh))


def _dot_exact_rhs(x, m):
    hi, lo = _split2(x)
    return _dot(hi, m) + _dot(lo, m)


def _rmsnorm(x, g):
    return x * lax.rsqrt(jnp.mean(x * x, axis=-1, keepdims=True) + RMS_EPS) * g


def _head_blockdiag(n):
    r = lax.broadcasted_iota(jnp.int32, (n, n), 0) // HEAD_DIM
    c = lax.broadcasted_iota(jnp.int32, (n, n), 1) // HEAD_DIM
    return (r == c).astype(BF16)


def _suffix_matrix(n):
    r = lax.broadcasted_iota(jnp.int32, (n, n), 0)
    c = lax.broadcasted_iota(jnp.int32, (n, n), 1)
    return jnp.where(r > c, -1.0, 0.0).astype(BF16)


def _row_spec(tm, cols):
    return pl.BlockSpec((tm, cols), lambda i: (i, 0))


def _const_spec(shape):
    return pl.BlockSpec(shape, lambda *_: (0,) * len(shape))


def _inproj_body(x_ref, g_ref, w_ref, conv_ref, rwkv_ref, k_ref, v_ref, qb_ref, kb_ref, vb_ref):
    h = _rmsnorm(x_ref[...], g_ref[...]).astype(BF16)
    o0, o1, o2, o3 = CONV_COLS, CONV_COLS + RWKV_COLS, CONV_COLS + RWKV_COLS + ATT_DIM, IN_COLS - ATT_DIM
    conv_ref[...] = _dot(h, w_ref[:, 0:o0])
    rwkv_ref[...] = _dot(h, w_ref[:, o0:o1])
    q = _dot(h, w_ref[:, o1:o2])
    k = _dot(h, w_ref[:, o2:o3])
    v = _dot(h, w_ref[:, o3:IN_COLS])
    k_ref[...] = k
    v_ref[...] = v
    qb_ref[...] = (q * (HEAD_DIM ** -0.5)).astype(BF16)
    kb_ref[...] = k.astype(BF16)
    vb_ref[...] = v.astype(BF16)


def _inproj(x, g, w_b, tm):
    n = x.shape[0]
    widths = (CONV_COLS, RWKV_COLS, ATT_DIM, ATT_DIM, ATT_DIM, ATT_DIM, ATT_DIM)
    dtypes = (F32, F32, F32, F32, BF16, BF16, BF16)
    return pl.pallas_call(
        _inproj_body,
        grid=(n // tm,),
        in_specs=[_row_spec(tm, D_MODEL), _const_spec((1, D_MODEL)), _const_spec((D_MODEL, IN_COLS))],
        out_specs=[_row_spec(tm, c) for c in widths],
        out_shape=[jax.ShapeDtypeStruct((n, c), d) for c, d in zip(widths, dtypes)],
        compiler_params=_cparams(("arbitrary",)),
        name="inproj",
    )(x, g.reshape(1, -1), w_b)


def _conv_body(p_ref, buf_ref, w_ref, cb_ref, lg_ref, lb_ref, y_ref, nb_ref, ext_ref, *, tt):
    keep = CONV_WIDTH - 1
    lead = CONV_HALO - keep

    @pl.when(pl.program_id(1) == 0)
    def _():
        ext_ref[0:CONV_HALO, :] = jnp.zeros((CONV_HALO, CONV_CH), F32)
        ext_ref[lead:CONV_HALO, :] = buf_ref[0]

    p = p_ref[0]
    ext_ref[CONV_HALO:CONV_HALO + tt, :] = p[:, :CONV_CH] * _sigmoid(p[:, CONV_CH:])
    acc = jnp.zeros((tt, CONV_CH), F32)
    for j in range(CONV_WIDTH):
        acc = acc + w_ref[j:j + 1, :] * ext_ref[lead + j:lead + j + tt, :]
    y = acc + cb_ref[...]
    mu = jnp.mean(y, axis=-1, keepdims=True)
    d = y - mu
    var = jnp.mean(d * d, axis=-1, keepdims=True)
    y = d * lax.rsqrt(var + LN_EPS) * lg_ref[...] + lb_ref[...]
    y_ref[0] = (y * _sigmoid(y)).astype(y_ref.dtype)
    nb_ref[0] = ext_ref[tt + lead:tt + CONV_HALO, :]
    tail = ext_ref[tt:tt + CONV_HALO, :]
    ext_ref[0:CONV_HALO, :] = tail


def _conv_mixer(p, buf, w, cb, lg, lb, tt):
    b, t, _ = p.shape
    keep = CONV_WIDTH - 1
    return pl.pallas_call(
        functools.partial(_conv_body, tt=tt),
        grid=(b, t // tt),
        in_specs=[pl.BlockSpec((1, tt, CONV_COLS), lambda i, j: (i, j, 0)),
                  pl.BlockSpec((1, keep, CONV_CH), lambda i, j: (i, 0, 0)),
                  _const_spec((CONV_WIDTH, CONV_CH)), _const_spec((1, CONV_CH)),
                  _const_spec((1, CONV_CH)), _const_spec((1, CONV_CH))],
        out_specs=[pl.BlockSpec((1, tt, CONV_CH), lambda i, j: (i, j, 0)),
                   pl.BlockSpec((1, keep, CONV_CH), lambda i, j: (i, 0, 0))],
        out_shape=[jax.ShapeDtypeStruct((b, t, CONV_CH), BF16),
                   jax.ShapeDtypeStruct((b, keep, CONV_CH), F32)],
        scratch_shapes=[pltpu.VMEM((tt + CONV_HALO, CONV_CH), F32)],
        compiler_params=_cparams(("arbitrary", "arbitrary")),
        name="conv_mixer",
    )(p, buf, w, cb.reshape(1, -1), lg.reshape(1, -1), lb.reshape(1, -1))


def _rwkv_pre_body(p_ref, prev_ref, mu_ref, w0_ref, wup_ref, a0_ref, aup_ref, gup_ref, kk_ref, ka_ref, rk_ref,
                   *outs, tiles_per_seq):
    heads_first = tiles_per_seq > 0
    p = p_ref[...]
    if heads_first:
        *outs, carry_ref = outs

        @pl.when(pl.program_id(0) % tiles_per_seq == 0)
        def _():
            carry_ref[...] = prev_ref[0]

        first_row = lax.broadcasted_iota(jnp.int32, p.shape, 0) == 0
        prev = jnp.where(first_row, carry_ref[...], pltpu.roll(p, 1, 0))
        carry_ref[...] = p[p.shape[0] - 1:p.shape[0], :]
    else:
        prev = prev_ref[...]
    pm = p + (prev - p) * mu_ref[...]
    d = RWKV_DIM
    r = pm[:, 0:d]
    k = pm[:, d:2 * d]
    v = pm[:, 2 * d:3 * d]
    xw = pm[:, 3 * d:3 * d + DECAY_LORA]
    xa = pm[:, 3 * d + DECAY_LORA:3 * d + DECAY_LORA + AAA_LORA]
    xg = pm[:, 3 * d + DECAY_LORA + AAA_LORA:RWKV_COLS]
    bd = _head_blockdiag(d)
    w_log = -_softplus(-(w0_ref[...] + _dot3(jnp.tanh(xw), wup_ref[...]))) - 0.5
    log_decay = -jnp.exp(w_log)
    a = _sigmoid(a0_ref[...] + _dot3(xa, aup_ref[...]))
    g = _dot3(_sigmoid(xg), gup_ref[...])
    kk = k * kk_ref[...]
    k = k * (1.0 + (a - 1.0) * ka_ref[...])
    kk = kk / jnp.maximum(jnp.sqrt(_dot_exact_rhs(kk * kk, bd)), 1e-12)
    steps = (r, log_decay, k, v, kk, kk * a)
    outs[-2][...] = g
    outs[-1][...] = _dot_exact_rhs(r * k * rk_ref[...], bd) * v
    if not heads_first:
        for ref, val in zip(outs[:6], steps):
            ref[...] = val
        return
    head = lambda h: slice(h * HEAD_DIM, (h + 1) * HEAD_DIM)
    for ref, val in zip(outs[:6], steps):
        for h in range(RWKV_HEADS):
            ref[h] = val[:, head(h)]
    for ref, val in zip(outs[6:9], (log_decay, kk, v)):
        val_t = val.T
        for h in range(RWKV_HEADS):
            ref[h] = val_t[head(h), :]


def _rwkv_pre(p, prev, mu, w0, w_up, a0, a_up, g_up, k_k, k_a, r_k, tm, seq_len=None):
    n = p.shape[0]
    vec = lambda a: a.reshape(1, -1)
    d = RWKV_DIM
    if seq_len is None:
        per_seq = 0
        prev_spec = _row_spec(tm, RWKV_COLS)
        scratch = []
        out_specs = [_row_spec(tm, d)] * 8
        out_shape = [jax.ShapeDtypeStruct((n, d), F32)] * 8
    else:
        per_seq = seq_len // tm
        prev = prev.reshape(-1, 1, RWKV_COLS)
        prev_spec = pl.BlockSpec((1, 1, RWKV_COLS), lambda i: (i // per_seq, 0, 0))
        scratch = [pltpu.VMEM((1, RWKV_COLS), F32)]
        chains = n // seq_len * RWKV_HEADS
        slab = pl.BlockSpec((RWKV_HEADS, tm, HEAD_DIM), lambda i: (i // per_seq, i % per_seq, 0))
        slab_t = pl.BlockSpec((RWKV_HEADS, HEAD_DIM, tm), lambda i: (i // per_seq, 0, i % per_seq))
        out_specs = [slab] * 6 + [slab_t] * 3 + [_row_spec(tm, d)] * 2
        out_shape = ([jax.ShapeDtypeStruct((chains, seq_len, HEAD_DIM), F32)] * 6
                     + [jax.ShapeDtypeStruct((chains, HEAD_DIM, seq_len), F32)] * 3
                     + [jax.ShapeDtypeStruct((n, d), F32)] * 2)
    return pl.pallas_call(
        functools.partial(_rwkv_pre_body, tiles_per_seq=per_seq),
        grid=(n // tm,),
        in_specs=[_row_spec(tm, RWKV_COLS), prev_spec, _const_spec((1, RWKV_COLS)),
                  _const_spec((1, d)), _const_spec((DECAY_LORA, d)), _const_spec((1, d)),
                  _const_spec((AAA_LORA, d)), _const_spec((GATE_LORA, d)),
                  _const_spec((1, d)), _const_spec((1, d)), _const_spec((1, d))],
        out_specs=out_specs,
        out_shape=out_shape,
        scratch_shapes=scratch,
        compiler_params=_cparams(("arbitrary",)),
        name="rwkv_pre",
    )(p, prev, vec(mu), vec(w0), w_up, vec(a0), a_up, g_up, vec(k_k), vec(k_a), vec(r_k))


def _wkv_body(r_ref, w_ref, k_ref, v_ref, kk_ref, b_ref, s0_ref, y_ref, sT_ref, s_scr, *, nc, tc):
    ti = pl.program_id(1)

    @pl.when(ti == 0)
    def _():
        s_scr[...] = s0_ref[...]

    eye = (lax.broadcasted_iota(jnp.int32, (HEAD_DIM, HEAD_DIM), 0)
           == lax.broadcasted_iota(jnp.int32, (HEAD_DIM, HEAD_DIM), 1))

    def step(t, carry):
        for c in range(nc):
            row = lambda ref: ref[c, pl.ds(t, 1), :]
            s = s_scr[c]
            sa = jnp.sum(s * row(kk_ref), axis=1, keepdims=True)
            vcol = jnp.sum(jnp.where(eye, row(v_ref), 0.0), axis=1, keepdims=True)
            s = s * jnp.exp(row(w_ref)) - sa * row(b_ref) + vcol * row(k_ref)
            s_scr[c] = s
            ycol = jnp.sum(s * row(r_ref), axis=1, keepdims=True)
            y_ref[c, pl.ds(t, 1), :] = jnp.sum(jnp.where(eye, ycol, 0.0), axis=0, keepdims=True)
        return carry

    lax.fori_loop(0, tc, step, 0)

    @pl.when(ti == pl.num_programs(1) - 1)
    def _():
        sT_ref[...] = s_scr[...]


def _wkv(r, w, k, v, kk, b, s0, nc, tc):
    n, t, _ = r.shape
    seq = pl.BlockSpec((nc, tc, HEAD_DIM), lambda i, j: (i, j, 0))
    st = pl.BlockSpec((nc, HEAD_DIM, HEAD_DIM), lambda i, j: (i, 0, 0))
    return pl.pallas_call(
        functools.partial(_wkv_body, nc=nc, tc=tc),
        grid=(n // nc, t // tc),
        in_specs=[seq] * 6 + [st],
        out_specs=[seq, st],
        out_shape=[jax.ShapeDtypeStruct((n, t, HEAD_DIM), F32),
                   jax.ShapeDtypeStruct((n, HEAD_DIM, HEAD_DIM), F32)],
        scratch_shapes=[pltpu.VMEM((nc, HEAD_DIM, HEAD_DIM), F32)],
        compiler_params=_cparams(("arbitrary", "arbitrary")),
        name="wkv",
    )(r, w, k, v, kk, b, s0)


def _mm(a, b):
    return _dot3(a, b)


def _mm_nt(a, b):
    ah, al = _split2(a)
    bh, bl = _split2(b)
    return _dot_nt(ah, bh) + (_dot_nt(ah, bl) + _dot_nt(al, bh))


def _split3(x):
    p1 = x.astype(BF16)
    r1 = x - p1.astype(F32)
    p2 = r1.astype(BF16)
    p3 = (r1 - p2.astype(F32)).astype(BF16)
    return p1, p2, p3


def _wkv_chunk_body(r_ref, lw_ref, k_ref, v_ref, kk_ref, b_ref, lwt_ref, kkt_ref, vt_ref, s0_ref,
                    y_ref, sT_ref, s_scr, *, nc):
    ti = pl.program_id(1)
    c_len = WKV_CHUNK

    @pl.when(ti == 0)
    def _():
        s_scr[...] = s0_ref[...]

    row = lax.broadcasted_iota(jnp.int32, (c_len, c_len), 0)
    col = lax.broadcasted_iota(jnp.int32, (c_len, c_len), 1)
    strict = row > col
    incl = row >= col
    lower = incl.astype(BF16)
    upper = (row <= col).astype(BF16)
    eye = (row == col).astype(F32)

    n_sub = WKV_BLOCK // c_len
    units = [(j, c) for j in range(n_sub) for c in range(nc)]
    each = lambda f, *lists: [f(*args) for args in zip(*lists)]
    ts = lambda j: slice(j * c_len, (j + 1) * c_len)
    rows = lambda ref: [ref[c, ts(j), :] for j, c in units]
    cols = lambda ref: [ref[c, :, ts(j)] for j, c in units]
    r, lw, k, v, kk, b = (rows(ref) for ref in (r_ref, lw_ref, k_ref, v_ref, kk_ref, b_ref))
    lwt, kkt, vt = (cols(ref) for ref in (lwt_ref, kkt_ref, vt_ref))
    cum = each(lambda x: sum(_dot(lower, piece) for piece in _split3(x)), lw)
    cumt = each(lambda x: sum(_dot(piece, upper) for piece in _split3(x)), lwt)
    w_incl = each(jnp.exp, cum)
    w_last = each(lambda w: w[c_len - 1:c_len, :], w_incl)
    inv = each(lambda x: jnp.exp(-x), cum)
    qt = each(lambda x, cs, l: x * jnp.exp(cs - l), kk, cum, lw)
    qtt = each(lambda x, cs, l: x * jnp.exp(cs - l), kkt, cumt, lwt)
    rt = each(jnp.multiply, r, w_incl)
    kt = each(jnp.multiply, k, inv)
    bt = each(jnp.multiply, b, inv)
    a_qb = each(lambda x, y: jnp.where(strict, _mm_nt(x, y), 0.0), qt, bt)
    a_qk = each(lambda x, y: jnp.where(strict, _mm_nt(x, y), 0.0), qt, kt)
    a_rk = each(lambda x, y: jnp.where(incl, _mm_nt(x, y), 0.0), rt, kt)
    a_rb = each(lambda x, y: jnp.where(incl, _mm_nt(x, y), 0.0), rt, bt)
    tinv = each(lambda n: eye - n, a_qb)
    power = a_qb
    for _ in range(WKV_CHUNK_LOG2 - 1):
        power = each(_mm, power, power)
        tinv = each(lambda t, p: t + _mm(t, p), tinv, power)
    tq = each(_mm, tinv, qt)
    av = each(_mm, a_qk, v)
    av_t = each(_mm_nt, vt, a_qk)
    tq_t = each(_mm_nt, qtt, tinv)
    tav = each(_mm, tinv, av)
    tav_t = each(_mm_nt, av_t, tinv)
    ark_v = each(_mm, a_rk, v)
    rq = each(lambda x, a, y: x - _mm(a, y), rt, a_rb, tq)
    yv = each(lambda x, a, y: x - _mm(a, y), ark_v, a_rb, tav)
    gain = each(lambda x, y, w: _mm(x, y * w), vt, kt, w_last)
    bw = each(jnp.multiply, bt, w_last)
    for j in range(n_sub):
        ids = [j * nc + c for c in range(nc)]
        s = [s_scr[c] for c in range(nc)]
        sa_t = [_mm(s[c], tq_t[i]) + tav_t[i] for c, i in enumerate(ids)]
        ys = [_mm_nt(rq[i], s[c]) + yv[i] for c, i in enumerate(ids)]
        new = [s[c] * w_last[i] + gain[i] - _mm(sa_t[c], bw[i]) for c, i in enumerate(ids)]
        for c, i in enumerate(ids):
            y_ref[c, ts(j), :] = ys[c]
            s_scr[c] = new[c]

    @pl.when(ti == pl.num_programs(1) - 1)
    def _():
        sT_ref[...] = s_scr[...]


def _wkv_chunked(r, lw, k, v, kk, b, lwt, kkt, vt, s0, nc):
    n, t, _ = r.shape
    seq = pl.BlockSpec((nc, WKV_BLOCK, HEAD_DIM), lambda i, j: (i, j, 0))
    seq_t = pl.BlockSpec((nc, HEAD_DIM, WKV_BLOCK), lambda i, j: (i, 0, j))
    st = pl.BlockSpec((nc, HEAD_DIM, HEAD_DIM), lambda i, j: (i, 0, 0))
    return pl.pallas_call(
        functools.partial(_wkv_chunk_body, nc=nc),
        grid=(n // nc, t // WKV_BLOCK),
        in_specs=[seq] * 6 + [seq_t] * 3 + [st],
        out_specs=[seq, st],
        out_shape=[jax.ShapeDtypeStruct((n, t, HEAD_DIM), F32),
                   jax.ShapeDtypeStruct((n, HEAD_DIM, HEAD_DIM), F32)],
        scratch_shapes=[pltpu.VMEM((nc, HEAD_DIM, HEAD_DIM), F32)],
        compiler_params=_cparams(("arbitrary", "arbitrary")),
        name="wkv_chunked",
    )(r, lw, k, v, kk, b, lwt, kkt, vt, s0)


def _sb_block(q, kb, vb, bias, rest, umat, mask):
    z = _dot_nt(q, kb) + bias
    sp = _softplus(z)
    counted = sp if mask is None else jnp.where(mask, sp, 0.0)
    after = _dot(counted.astype(BF16), umat)
    a = jnp.exp((z - sp) + after + rest)
    if mask is not None:
        a = jnp.where(mask, a, 0.0)
    out = _dot(a.astype(BF16), vb)
    return out, after[:, 0:1] - counted[:, 0:1]


def _attn_prompt_body(q_ref, k_ref, v_ref, bias_ref, o_ref, *, tile):
    qi = pl.program_id(1)
    heads = range(ATT_HEADS)
    hs = lambda a, h: a[:, h * HEAD_DIM:(h + 1) * HEAD_DIM]
    q_all = q_ref[0]
    qs = [hs(q_all, h) for h in heads]
    biases = [bias_ref[h] for h in heads]
    umat = _suffix_matrix(tile)
    causal = (lax.broadcasted_iota(jnp.int32, (tile, tile), 1)
              < lax.broadcasted_iota(jnp.int32, (tile, tile), 0))

    def visit(j, accs, rests, mask):
        start = pl.multiple_of(j * tile, tile)
        kb = k_ref[0, pl.ds(start, tile), :]
        vb = v_ref[0, pl.ds(start, tile), :]
        zs = [_dot_nt(qs[h], hs(kb, h)) + biases[h] for h in heads]
        sps = [_softplus(z) for z in zs]
        counted = sps if mask is None else [jnp.where(mask, sp, 0.0) for sp in sps]
        afters = [_dot(sp.astype(BF16), umat) for sp in counted]
        probs = [jnp.exp((zs[h] - sps[h]) + afters[h] + rests[h]) for h in heads]
        if mask is not None:
            probs = [jnp.where(mask, a, 0.0) for a in probs]
        outs = [_dot(probs[h].astype(BF16), hs(vb, h)) for h in heads]
        new_accs = tuple(accs[h] + outs[h] for h in heads)
        new_rests = tuple(rests[h] + afters[h][:, 0:1] - counted[h][:, 0:1] for h in heads)
        return new_accs, new_rests

    accs = tuple(jnp.zeros((tile, HEAD_DIM), F32) for _ in heads)
    rests = tuple(jnp.zeros((tile, 1), F32) for _ in heads)
    accs, rests = visit(qi, accs, rests, causal)
    accs, rests = lax.fori_loop(0, qi, lambda jj, c: visit(qi - 1 - jj, c[0], c[1], None), (accs, rests))
    r = lax.broadcasted_iota(jnp.int32, (HEAD_DIM, ATT_DIM), 0)
    c = lax.broadcasted_iota(jnp.int32, (HEAD_DIM, ATT_DIM), 1)
    out = jnp.zeros((tile, ATT_DIM), F32)
    for h in heads:
        out = out + _dot(accs[h].astype(BF16), (c == r + h * HEAD_DIM).astype(BF16))
    o_ref[0] = out.astype(o_ref.dtype)


def _attn_prompt(qb, kb, vb, bias):
    b, t, d = qb.shape
    tile = ATT_TILE
    bias_b = jnp.broadcast_to(bias.astype(F32)[:, None, None], (ATT_HEADS, 1, tile))
    qspec = pl.BlockSpec((1, tile, d), lambda i, l: (i, l, 0))
    kvspec = pl.BlockSpec((1, t, d), lambda i, l: (i, 0, 0))
    return pl.pallas_call(
        functools.partial(_attn_prompt_body, tile=tile),
        grid=(b, t // tile),
        in_specs=[qspec, kvspec, kvspec, _const_spec((ATT_HEADS, 1, tile))],
        out_specs=qspec,
        out_shape=jax.ShapeDtypeStruct((b, t, d), BF16),
        compiler_params=_cparams(("arbitrary", "arbitrary")),
        name="attn_prompt",
    )(qb, kb, vb, bias_b)


def _attn_decode_body(pt_ref, q_ref, bias_ref, kn_ref, vn_ref, *refs, npg, nq):
    k_refs = refs[:npg]
    v_refs = refs[npg:2 * npg]
    o_ref, rest_ref, acc_ref = refs[2 * npg:]
    jb = pl.program_id(1)
    rows = ATT_HEADS * Q_PAD
    pages = range(npg)
    spread = lambda tot: jnp.broadcast_to(tot, (rows, PAGE_SIZE))
    q = q_ref[0]
    bias = bias_ref[...]
    umat = _suffix_matrix(PAGE_SIZE)

    @pl.when(jb == 0)
    def _():
        kidx = lax.broadcasted_iota(jnp.int32, (rows, PAGE_SIZE), 1)
        qidx = lax.broadcasted_iota(jnp.int32, (rows, PAGE_SIZE), 0) % Q_PAD
        out, tot = _sb_block(q, kn_ref[0].astype(BF16), vn_ref[0].astype(BF16), bias,
                             jnp.zeros((rows, 1), F32), umat, (kidx < qidx) & (kidx < nq))
        acc_ref[...] = out
        rest_ref[...] = spread(tot)

    zs = [_dot_nt(q, k_refs[s][0, 0].astype(BF16)) + bias for s in pages]
    sps = [_softplus(z) for z in zs]
    afters = [_dot(sp.astype(BF16), umat) for sp in sps]
    rest = rest_ref[...]
    probs = []
    for s in pages:
        probs.append(jnp.exp((zs[s] - sps[s]) + afters[s] + rest).astype(BF16))
        rest = rest + spread(afters[s][:, 0:1] - sps[s][:, 0:1])
    rest_ref[...] = rest
    out = _dot(probs[0], v_refs[0][0, 0].astype(BF16))
    for s in pages[1:]:
        out = out + _dot(probs[s], v_refs[s][0, 0].astype(BF16))
    acc_ref[...] += out

    @pl.when(jb == pl.num_programs(1) - 1)
    def _():
        head_of_row = lax.broadcasted_iota(jnp.int32, (rows, ATT_DIM), 0) // Q_PAD
        head_of_col = lax.broadcasted_iota(jnp.int32, (rows, ATT_DIM), 1) // HEAD_DIM
        kept = jnp.where(head_of_row == head_of_col, acc_ref[...], 0.0)
        out = kept[0:Q_PAD]
        for h in range(1, ATT_HEADS):
            out = out + kept[h * Q_PAD:(h + 1) * Q_PAD]
        o_ref[0] = out.astype(o_ref.dtype)


def _attn_decode(li, qb, k_new, v_new, cache_k, cache_v, page_table, bias):
    b, t, _ = qb.shape
    n_pages = page_table.shape[1]
    npg = PAGES_PER_STEP
    rows = ATT_HEADS * Q_PAD
    qpad = jnp.pad(qb, ((0, 0), (0, Q_PAD - t), (0, 0)))
    head_of_col = jnp.arange(ATT_DIM) // HEAD_DIM
    qbd = jnp.where(head_of_col[None, None, None, :] == jnp.arange(ATT_HEADS)[None, :, None, None],
                    qpad[:, None], jnp.zeros((), BF16)).reshape(b, rows, ATT_DIM)
    bias_b = jnp.broadcast_to(jnp.repeat(bias.astype(F32), Q_PAD)[:, None], (rows, PAGE_SIZE))
    kn = jnp.pad(k_new, ((0, 0), (0, PAGE_SIZE - t), (0, 0)))
    vn = jnp.pad(v_new, ((0, 0), (0, PAGE_SIZE - t), (0, 0)))

    def page_spec(s):
        return pl.BlockSpec((1, 1, PAGE_SIZE, ATT_DIM),
                            lambda i, j, pt: (li, pt[i * n_pages + n_pages - 1 - (j * npg + s)], 0, 0))

    per_b = lambda r, c: pl.BlockSpec((1, r, c), lambda i, j, pt: (i, 0, 0))
    grid_spec = pltpu.PrefetchScalarGridSpec(
        num_scalar_prefetch=1,
        grid=(b, n_pages // npg),
        in_specs=[per_b(rows, ATT_DIM), pl.BlockSpec((rows, PAGE_SIZE), lambda i, j, pt: (0, 0)),
                  per_b(PAGE_SIZE, ATT_DIM), per_b(PAGE_SIZE, ATT_DIM)]
                 + [page_spec(s) for s in range(npg)] * 2,
        out_specs=per_b(Q_PAD, ATT_DIM),
        scratch_shapes=[pltpu.VMEM((rows, PAGE_SIZE), F32), pltpu.VMEM((rows, ATT_DIM), F32)],
    )
    out = pl.pallas_call(
        functools.partial(_attn_decode_body, npg=npg, nq=t),
        grid_spec=grid_spec,
        out_shape=jax.ShapeDtypeStruct((b, Q_PAD, ATT_DIM), BF16),
        compiler_params=_cparams(("arbitrary", "arbitrary")),
        name="attn_decode",
    )(page_table.reshape(-1), qbd, bias_b, kn, vn, *([cache_k] * npg), *([cache_v] * npg))
    return out[:, :t]


def _outproj_body(x_ref, yc_ref, yw_ref, bonus_ref, g_ref, ya_ref, lng_ref, lnb_ref, w_ref, o_ref):
    bd = _head_blockdiag(RWKV_DIM)
    if len(yw_ref.shape) == 3:
        r = lax.broadcasted_iota(jnp.int32, (HEAD_DIM, RWKV_DIM), 0)
        c = lax.broadcasted_iota(jnp.int32, (HEAD_DIM, RWKV_DIM), 1)
        y = sum(_dot_exact_rhs(yw_ref[h], (c == r + h * HEAD_DIM).astype(BF16)) for h in range(RWKV_HEADS))
    else:
        y = yw_ref[...]
    inv = 1.0 / HEAD_DIM
    d = y - _dot_exact_rhs(y, bd) * inv
    var = _dot_exact_rhs(d * d, bd) * inv
    yn = d * lax.rsqrt(var + GN_EPS) * lng_ref[...] + lnb_ref[...]
    yr = ((yn + bonus_ref[...]) * g_ref[...]).astype(BF16)
    c0, c1 = CONV_CH, CONV_CH + RWKV_DIM
    o_ref[...] = (x_ref[...] + _dot(yc_ref[...], w_ref[0:c0, :]) + _dot(yr, w_ref[c0:c1, :])
                  + _dot(ya_ref[...], w_ref[c1:D_MODEL, :]))


def _outproj(x, yc, yw, bonus, g, ya, ln_g, ln_b, w_b, tm):
    n = x.shape[0]
    if yw.ndim == 3:
        per_seq = yw.shape[1] // tm
        yw_spec = pl.BlockSpec((RWKV_HEADS, tm, HEAD_DIM), lambda i: (i // per_seq, i % per_seq, 0))
    else:
        yw_spec = _row_spec(tm, RWKV_DIM)
    return pl.pallas_call(
        _outproj_body,
        grid=(n // tm,),
        in_specs=[_row_spec(tm, D_MODEL), _row_spec(tm, CONV_CH), yw_spec, _row_spec(tm, RWKV_DIM),
                  _row_spec(tm, RWKV_DIM), _row_spec(tm, ATT_DIM), _const_spec((1, RWKV_DIM)),
                  _const_spec((1, RWKV_DIM)), _const_spec((D_MODEL, D_MODEL))],
        out_specs=_row_spec(tm, D_MODEL),
        out_shape=jax.ShapeDtypeStruct((n, D_MODEL), F32),
        compiler_params=_cparams(("arbitrary",)),
        name="outproj",
    )(x, yc, yw, bonus, g, ya, ln_g.reshape(1, -1), ln_b.reshape(1, -1), w_b)


SUBLANES = 8
KEY_GROUPS = N_KEYS // SUBLANES


def _oddeven_merge(lo, hi, r):
    step = r * 2
    if step < hi - lo:
        yield from _oddeven_merge(lo, hi, step)
        yield from _oddeven_merge(lo + r, hi, step)
        yield from [(i, i + r) for i in range(lo + r, hi - r, step)]
    else:
        yield (lo, lo + r)


def _oddeven_sort(lo, hi):
    if hi - lo >= 1:
        mid = lo + (hi - lo) // 2
        yield from _oddeven_sort(lo, mid)
        yield from _oddeven_sort(mid + 1, hi)
        yield from _oddeven_merge(lo, hi, 1)


SORT_NET = tuple(_oddeven_sort(0, PEER_TOPK - 1))
BITONIC_NET = tuple((i, i + s) for s in (8, 4, 2, 1) for i in range(PEER_TOPK) if not i & s)
SUM_SLOTS = tuple((a, b) for a in range(PEER_TOPK) for b in range(PEER_TOPK) if (a + 1) * (b + 1) <= PEER_TOPK)


def _exchange(vals, i, j):
    vals[i], vals[j] = jnp.maximum(vals[i], vals[j]), jnp.minimum(vals[i], vals[j])


def _sublane_max(x):
    for shift in (4, 2, 1):
        x = jnp.maximum(x, pltpu.roll(x, shift, 0))
    return x


def _top16_sorted(groups):
    vals = list(groups)
    for i, j in SORT_NET:
        _exchange(vals, i, j)
    for shift in (4, 2, 1):
        moved = [pltpu.roll(v, shift, 0) for v in vals]
        vals = [jnp.maximum(vals[i], moved[PEER_TOPK - 1 - i]) for i in range(PEER_TOPK)]
        for i, j in BITONIC_NET:
            _exchange(vals, i, j)
    return vals


def _peer_route_body(x_ref, g_ref, wq_ref, sk_ref, h_out, rank_out, e2_out, cnt_out, f1_out):
    tm = x_ref.shape[0]
    kk = PEER_TOPK
    hb = _rmsnorm(x_ref[...], g_ref[...]).astype(BF16)
    h_out[...] = hb
    sub = lax.broadcasted_iota(jnp.int32, (SUBLANES, tm), 0)
    split = lambda s: [s[i * SUBLANES:(i + 1) * SUBLANES, :] for i in range(KEY_GROUPS)]
    join = lambda parts: jnp.concatenate(parts, axis=0)
    neg_inf = jnp.full((SUBLANES, tm), -jnp.inf, F32)
    half = PEER_DK // 2
    for h in range(PEER_HEADS):
        q = _dot(hb, wq_ref[:, h * PEER_DK:(h + 1) * PEER_DK])
        s1 = split(_dot_nt(sk_ref[2 * h], q[:, :half].astype(BF16)))
        s2 = split(_dot_nt(sk_ref[2 * h + 1], q[:, half:].astype(BF16)))
        v1 = _top16_sorted(s1)
        v2 = _top16_sorted(s2)
        sums = {ab: v1[ab[0]] + v2[ab[1]] for ab in SUM_SLOTS}
        packs = []
        for j in range(0, len(SUM_SLOTS), SUBLANES):
            pack = neg_inf
            for s, ab in enumerate(SUM_SLOTS[j:j + SUBLANES]):
                pack = jnp.where(sub == s, sums[ab], pack)
            packs.append(pack)
        best = []
        for _ in range(kk):
            m = packs[0]
            for p in packs[1:]:
                m = jnp.maximum(m, p)
            m = _sublane_max(m)
            best.append(m)
            packs = [jnp.where(p == m, -jnp.inf, p) for p in packs]
        z = jnp.ones((SUBLANES, tm), F32)
        for i in range(1, kk):
            z = z + jnp.exp(best[i] - best[0])
        half_inv_z = 0.5 / z
        cnt = [jnp.zeros((SUBLANES, tm), F32) for _ in range(kk)]
        for a, b in SUM_SLOTS:
            cnt[a] = cnt[a] + jnp.where(sums[(a, b)] >= best[kk - 1], 1.0, 0.0)
        cnt_full, f1_full, rank_full, e2_full = [], [], [], []
        for g in range(KEY_GROUPS):
            c = jnp.zeros((SUBLANES, tm), F32)
            for a in range(kk):
                c = jnp.where(s1[g] == v1[a], cnt[a], c)
            cnt_full.append(c)
            f1_full.append(jnp.where(s1[g] >= v1[kk - 1], jnp.exp(s1[g] - v1[0]), 0.0) * half_inv_z)
            rk = jnp.full((SUBLANES, tm), float(kk), F32)
            for b in reversed(range(kk)):
                rk = jnp.where(s2[g] == v2[b], float(b), rk)
            rank_full.append(rk)
            e2_full.append(jnp.where(s2[g] >= v2[kk - 1], jnp.exp(s2[g] - v2[0]), 0.0))
        cnt_out[h] = join(cnt_full)
        f1_out[h] = join(f1_full)
        rank_out[h] = join(rank_full).astype(BF16)
        e2_out[h] = join(e2_full).astype(BF16)


def _peer_route(x, g, wq_b, sk_b, tm):
    n = x.shape[0]
    keyed = pl.BlockSpec((PEER_HEADS, N_KEYS, tm), lambda i: (0, 0, i))
    keyed_shape = lambda dt: jax.ShapeDtypeStruct((PEER_HEADS, N_KEYS, n), dt)
    return pl.pallas_call(
        _peer_route_body,
        grid=(n // tm,),
        in_specs=[_row_spec(tm, D_MODEL), _const_spec((1, D_MODEL)), _const_spec((D_MODEL, PEER_HEADS * PEER_DK)),
                  _const_spec((2 * PEER_HEADS, N_KEYS, PEER_DK // 2))],
        out_specs=[_row_spec(tm, D_MODEL), keyed, keyed, keyed, keyed],
        out_shape=[jax.ShapeDtypeStruct((n, D_MODEL), BF16), keyed_shape(BF16), keyed_shape(BF16),
                   keyed_shape(F32), keyed_shape(F32)],
        compiler_params=_cparams(("arbitrary",)),
        name="peer_route",
    )(x, g.reshape(1, -1), wq_b, sk_b)


GATE_ROWS = 16


GELU_A = 0.7978845608028654
GELU_B = GELU_A * 0.044715


def _gelu_tanh_doubled(x):
    return x + x * jnp.tanh(x * (GELU_A + GELU_B * (x * x)))


def _peer_expert_body(x_ref, h_ref, rank_ref, e2_ref, cnt_ref, f1_ref, u_ref, vt_ref, o_ref, acc_ref):
    ec = pl.program_id(1)

    @pl.when(ec == 0)
    def _():
        acc_ref[...] = jnp.zeros(acc_ref.shape, F32)

    tm = h_ref.shape[0]
    n_grp = 2 if tm % 512 == 0 else 1
    grp = tm // n_grp
    toks = [slice(g * grp, (g + 1) * grp) for g in range(n_grp)]
    hids = [_dot_nt(u_ref[...], h_ref[tk, :]) for tk in toks]
    heads = range(PEER_HEADS)
    for g, tk in enumerate(toks):
        weights = []
        for r in range(PEER_ROWS):
            bcast = lambda ref, h: jnp.broadcast_to(ref[h, 0, r:r + 1, tk], (GATE_ROWS, grp)).astype(BF16)
            cnts = [bcast(cnt_ref, h) for h in heads]
            f1s = [bcast(f1_ref, h) for h in heads]
            for j in range(N_KEYS // GATE_ROWS):
                keys = slice(j * GATE_ROWS, (j + 1) * GATE_ROWS)
                gate = jnp.zeros((GATE_ROWS, grp), BF16)
                for h in heads:
                    e2 = e2_ref[h, keys, tk]
                    gate = gate + jnp.where(rank_ref[h, keys, tk] < cnts[h], e2, jnp.zeros_like(e2)) * f1s[h]
                rows = slice(r * N_KEYS + j * GATE_ROWS, r * N_KEYS + (j + 1) * GATE_ROWS)
                weights.append(gate * _gelu_tanh_doubled(hids[g][rows]).astype(BF16))
        acc_ref[:, tk] += _dot(vt_ref[...], jnp.concatenate(weights, axis=0))

    @pl.when(ec == pl.num_programs(1) - 1)
    def _():
        o_ref[...] = x_ref[...] + acc_ref[...].T


def _peer_expert(x, hb, rank2, e2, cnt, f1, u_b, vt_b, tm):
    n = x.shape[0]
    te = PEER_ROWS * N_KEYS
    ne = u_b.shape[0] // te
    chunked = lambda a: a.reshape(PEER_HEADS, ne, PEER_ROWS, n)
    tok = lambda c: pl.BlockSpec((tm, c), lambda i, e: (i, 0))
    keyed = pl.BlockSpec((PEER_HEADS, N_KEYS, tm), lambda i, e: (0, 0, i))
    row = pl.BlockSpec((PEER_HEADS, 1, PEER_ROWS, tm), lambda i, e: (0, e, 0, i))
    return pl.pallas_call(
        _peer_expert_body,
        grid=(n // tm, ne),
        in_specs=[tok(D_MODEL), tok(D_MODEL), keyed, keyed, row, row,
                  pl.BlockSpec((te, D_MODEL), lambda i, e: (e, 0)),
                  pl.BlockSpec((D_MODEL, te), lambda i, e: (0, e))],
        out_specs=tok(D_MODEL),
        out_shape=jax.ShapeDtypeStruct((n, D_MODEL), F32),
        scratch_shapes=[pltpu.VMEM((D_MODEL, tm), F32)],
        compiler_params=_cparams(("arbitrary", "arbitrary")),
        name="peer_expert",
    )(x, hb, rank2, e2, chunked(cnt), chunked(f1), u_b, vt_b)


def _final_norm_body(x_ref, g_ref, o_ref):
    o_ref[...] = _rmsnorm(x_ref[...], g_ref[...])


def _final_norm(x, g, tm):
    n = x.shape[0]
    return pl.pallas_call(
        _final_norm_body,
        grid=(n // tm,),
        in_specs=[_row_spec(tm, D_MODEL), _const_spec((1, D_MODEL))],
        out_specs=_row_spec(tm, D_MODEL),
        out_shape=jax.ShapeDtypeStruct((n, D_MODEL), F32),
        compiler_params=_cparams(("arbitrary",)),
        name="final_norm",
    )(x, g.reshape(1, -1))


def _heads_first(a, b, t):
    h = a.shape[1] // HEAD_DIM
    return a.reshape(b, t, h, HEAD_DIM).transpose(0, 2, 1, 3).reshape(b * h, t, HEAD_DIM)


def _tokens_first(a, b, t):
    h = a.shape[0] // b
    return a.reshape(b, h, t, HEAD_DIM).transpose(0, 2, 1, 3).reshape(b * t, h * HEAD_DIM)


def _layer(x, b, t, conv_buf, shift_prev, wkv0, attend, lw, tm, conv_tile, wkv_chains, wkv_tile):
    p_conv, p_rwkv, k, v, qb, kb, vb = _inproj(x, lw["norm1_g"], lw["w_in"], tm)

    y_conv, conv_new = _conv_mixer(p_conv.reshape(b, t, CONV_COLS), conv_buf, lw["conv_w"], lw["conv_b"],
                                   lw["conv_ln_g"], lw["conv_ln_b"], conv_tile)

    p3 = p_rwkv.reshape(b, t, RWKV_COLS)
    pre = lambda prev, **kw: _rwkv_pre(p_rwkv, prev, lw["rwkv_mu"], lw["rwkv_w0"], lw["rwkv_w_up"], lw["rwkv_a0"],
                                       lw["rwkv_a_up"], lw["rwkv_g_up"], lw["rwkv_k_k"], lw["rwkv_k_a"],
                                       lw["rwkv_r_k"], tm, **kw)
    s0 = wkv0.reshape(b * RWKV_HEADS, HEAD_DIM, HEAD_DIM)
    if t % WKV_BLOCK == 0 and t % tm == 0:
        *steps, g, bonus = pre(shift_prev, seq_len=t)
        y_wkv, wkv_new = _wkv_chunked(*steps, s0, wkv_chains)
    else:
        prev = jnp.concatenate([shift_prev[:, None, :], p3[:, :-1]], axis=1).reshape(b * t, RWKV_COLS)
        *steps, g, bonus = pre(prev)
        y_wkv, wkv_new = _wkv(*(_heads_first(a, b, t) for a in steps), s0, wkv_chains, wkv_tile)
        y_wkv = _tokens_first(y_wkv, b, t)

    y_att = attend(qb, k, v, kb, vb)

    x = _outproj(x, y_conv.reshape(b * t, CONV_CH), y_wkv, bonus, g, y_att, lw["rwkv_ln_g"], lw["rwkv_ln_b"],
                 lw["w_out"], tm)
    routed = _peer_route(x, lw["norm2_g"], lw["peer_w_query"], lw["peer_sub_keys"], min(tm, 256))
    x = _peer_expert(x, *routed, lw["peer_u"], lw["peer_vt"], tm)
    states = (k.reshape(b, t, ATT_HEADS, HEAD_DIM), v.reshape(b, t, ATT_HEADS, HEAD_DIM), conv_new,
              p3[:, -1], wkv_new.reshape(b, RWKV_HEADS, HEAD_DIM, HEAD_DIM))
    return x, states


def kernel(x_prompt, x_sample, cache_k, cache_v, state_conv, state_shift, state_wkv, page_table, norm1_g, w_in, conv_w, conv_b, conv_ln_g, conv_ln_b, rwkv_mu, rwkv_w0, rwkv_w_up, rwkv_a0, rwkv_a_up, rwkv_g_up, rwkv_k_k, rwkv_k_a, rwkv_r_k, rwkv_ln_g, rwkv_ln_b, att_bias, w_out, norm2_g, peer_w_query, peer_sub_keys, peer_u, peer_v, final_g):
    depth = w_in.shape[0]
    bp, tp, _ = x_prompt.shape
    bs, ts, _ = x_sample.shape
    n_pool = cache_k.shape[1]
    ck = cache_k.reshape(depth, n_pool, PAGE_SIZE, ATT_DIM)
    cv = cache_v.reshape(depth, n_pool, PAGE_SIZE, ATT_DIM)
    xp = x_prompt.reshape(bp * tp, D_MODEL)
    xs = x_sample.reshape(bs * ts, D_MODEL)
    st_p, st_s = [], []
    for li in range(depth):
        lw = dict(
            norm1_g=norm1_g[li], w_in=w_in[li].astype(BF16), conv_w=conv_w[li], conv_b=conv_b[li],
            conv_ln_g=conv_ln_g[li], conv_ln_b=conv_ln_b[li], rwkv_mu=rwkv_mu[li], rwkv_w0=rwkv_w0[li],
            rwkv_w_up=rwkv_w_up[li], rwkv_a0=rwkv_a0[li], rwkv_a_up=rwkv_a_up[li], rwkv_g_up=rwkv_g_up[li],
            rwkv_k_k=rwkv_k_k[li], rwkv_k_a=rwkv_k_a[li], rwkv_r_k=rwkv_r_k[li].reshape(-1),
            rwkv_ln_g=rwkv_ln_g[li], rwkv_ln_b=rwkv_ln_b[li], w_out=w_out[li].astype(BF16), norm2_g=norm2_g[li],
            peer_w_query=peer_w_query[li].astype(BF16),
            peer_sub_keys=peer_sub_keys[li].astype(BF16).reshape(2 * PEER_HEADS, N_KEYS, PEER_DK // 2),
            peer_u=peer_u[li].astype(BF16), peer_vt=peer_v[li].astype(BF16).T)
        bias = att_bias[li]

        def attend_prompt(qb, k, v, kb, vb):
            sq = lambda a: a.reshape(bp, tp, ATT_DIM)
            return _attn_prompt(sq(qb), sq(kb), sq(vb), bias).reshape(bp * tp, ATT_DIM)

        def attend_sample(qb, k, v, kb, vb):
            o = _attn_decode(li, qb.reshape(bs, ts, ATT_DIM), k.reshape(bs, ts, ATT_DIM),
                             v.reshape(bs, ts, ATT_DIM), ck, cv, page_table, bias)
            return o.reshape(bs * ts, ATT_DIM)

        xp, new_p = _layer(xp, bp, tp, jnp.zeros((bp, CONV_WIDTH - 1, CONV_CH), F32),
                           jnp.zeros((bp, RWKV_COLS), F32), jnp.zeros((bp, RWKV_HEADS, HEAD_DIM, HEAD_DIM), F32),
                           attend_prompt, lw, tm=512, conv_tile=512, wkv_chains=RWKV_HEADS, wkv_tile=256)
        xs, new_s = _layer(xs, bs, ts, state_conv[li], state_shift[li], state_wkv[li],
                           attend_sample, lw, tm=bs * ts, conv_tile=ts, wkv_chains=bp * RWKV_HEADS, wkv_tile=ts)
        st_p.append(new_p)
        st_s.append(new_s)
    y_prompt = _final_norm(xp, final_g, 512).reshape(bp, tp, D_MODEL)
    y_sample = _final_norm(xs, final_g, bs * ts).reshape(bs, ts, D_MODEL)
    stack = lambda st, j: jnp.stack([s[j] for s in st], axis=0)
    return (y_prompt, y_sample) + tuple(stack(st_p, j) for j in range(5)) + tuple(stack(st_s, j) for j in range(5))
```

```python
import functools

import jax
import jax.numpy as jnp
from jax import lax
from jax.experimental import pallas as pl
from jax.experimental.pallas import tpu as pltpu

F32 = jnp.float32
BF16 = jnp.bfloat16

D_MODEL = 1024
HEAD_DIM = 64
CONV_CH = 256
CONV_WIDTH = 31
CONV_HALO = 32
RWKV_HEADS = 6
RWKV_DIM = 384
ATT_HEADS = 6
ATT_DIM = 384
DECAY_LORA = 64
AAA_LORA = 64
GATE_LORA = 128
CONV_COLS = 512
RWKV_COLS = 1408
IN_COLS = 3072
PAGE_SIZE = 128
PEER_HEADS = 8
PEER_DK = 256
N_KEYS = 128
PEER_TOPK = 16
RMS_EPS = 1e-5
LN_EPS = 1e-5
GN_EPS = 64e-5

ATT_TILE = 256
PAGES_PER_STEP = 32
Q_PAD = 8
PEER_ROWS = 16
WKV_CHUNK_LOG2 = 6
WKV_CHUNK = 1 << WKV_CHUNK_LOG2
WKV_BLOCK = 128
VMEM_LIMIT = 56 * 1024 * 1024


def _cparams(sem):
    return pltpu.CompilerParams(dimension_semantics=sem, vmem_limit_bytes=VMEM_LIMIT)


def _sigmoid(x):
    return 1.0 / (1.0 + jnp.exp(-x))


def _softplus(x):
    return jnp.maximum(x, 0.0) + jnp.log(1.0 + jnp.exp(-jnp.abs(x)))


def _dot(a, b):
    return jnp.dot(a, b, preferred_element_type=F32)


def _dot_nt(a, b):
    return lax.dot_general(a, b, (((1,), (1,)), ((), ())), preferred_element_type=F32)


def _split2(x):
    hi = x.astype(BF16)
    lo = (x - hi.astype(F32)).astype(BF16)
    return hi, lo


def _dot3(a, b):
    ah, al = _split2(a)
    bh, bl = _split2(b)
    return _dot(ah, bh) + (_dot(ah, bl) + _dot(al, bh))


def _dot_exact_rhs(x, m):
    hi, lo = _split2(x)
    return _dot(hi, m) + _dot(lo, m)


def _rmsnorm(x, g):
    return x * lax.rsqrt(jnp.mean(x * x, axis=-1, keepdims=True) + RMS_EPS) * g


def _head_blockdiag(n):
    r = lax.broadcasted_iota(jnp.int32, (n, n), 0) // HEAD_DIM
    c = lax.broadcasted_iota(jnp.int32, (n, n), 1) // HEAD_DIM
    return (r == c).astype(BF16)


def _suffix_matrix(n):
    r = lax.broadcasted_iota(jnp.int32, (n, n), 0)
    c = lax.broadcasted_iota(jnp.int32, (n, n), 1)
    return jnp.where(r > c, -1.0, 0.0).astype(BF16)


def _row_spec(tm, cols):
    return pl.BlockSpec((tm, cols), lambda i: (i, 0))


def _const_spec(shape):
    return pl.BlockSpec(shape, lambda *_: (0,) * len(shape))


def _inproj_body(x_ref, g_ref, w_ref, conv_ref, rwkv_ref, k_ref, v_ref, qb_ref, kb_ref, vb_ref):
    h = _rmsnorm(x_ref[...], g_ref[...]).astype(BF16)
    o0, o1, o2, o3 = CONV_COLS, CONV_COLS + RWKV_COLS, CONV_COLS + RWKV_COLS + ATT_DIM, IN_COLS - ATT_DIM
    conv_ref[...] = _dot(h, w_ref[:, 0:o0])
    rwkv_ref[...] = _dot(h, w_ref[:, o0:o1])
    q = _dot(h, w_ref[:, o1:o2])
    k = _dot(h, w_ref[:, o2:o3])
    v = _dot(h, w_ref[:, o3:IN_COLS])
    k_ref[...] = k
    v_ref[...] = v
    qb_ref[...] = (q * (HEAD_DIM ** -0.5)).astype(BF16)
    kb_ref[...] = k.astype(BF16)
    vb_ref[...] = v.astype(BF16)


def _inproj(x, g, w_b, tm):
    n = x.shape[0]
    widths = (CONV_COLS, RWKV_COLS, ATT_DIM, ATT_DIM, ATT_DIM, ATT_DIM, ATT_DIM)
    dtypes = (F32, F32, F32, F32, BF16, BF16, BF16)
    return pl.pallas_call(
        _inproj_body,
        grid=(n // tm,),
        in_specs=[_row_spec(tm, D_MODEL), _const_spec((1, D_MODEL)), _const_spec((D_MODEL, IN_COLS))],
        out_specs=[_row_spec(tm, c) for c in widths],
        out_shape=[jax.ShapeDtypeStruct((n, c), d) for c, d in zip(widths, dtypes)],
        compiler_params=_cparams(("arbitrary",)),
        name="inproj",
    )(x, g.reshape(1, -1), w_b)


def _conv_body(p_ref, buf_ref, w_ref, cb_ref, lg_ref, lb_ref, y_ref, nb_ref, ext_ref, *, tt):
    keep = CONV_WIDTH - 1
    lead = CONV_HALO - keep

    @pl.when(pl.program_id(1) == 0)
    def _():
        ext_ref[0:CONV_HALO, :] = jnp.zeros((CONV_HALO, CONV_CH), F32)
        ext_ref[lead:CONV_HALO, :] = buf_ref[0]

    p = p_ref[0]
    ext_ref[CONV_HALO:CONV_HALO + tt, :] = p[:, :CONV_CH] * _sigmoid(p[:, CONV_CH:])
    acc = jnp.zeros((tt, CONV_CH), F32)
    for j in range(CONV_WIDTH):
        acc = acc + w_ref[j:j + 1, :] * ext_ref[lead + j:lead + j + tt, :]
    y = acc + cb_ref[...]
    mu = jnp.mean(y, axis=-1, keepdims=True)
    d = y - mu
    var = jnp.mean(d * d, axis=-1, keepdims=True)
    y = d * lax.rsqrt(var + LN_EPS) * lg_ref[...] + lb_ref[...]
    y_ref[0] = (y * _sigmoid(y)).astype(y_ref.dtype)
    nb_ref[0] = ext_ref[tt + lead:tt + CONV_HALO, :]
    tail = ext_ref[tt:tt + CONV_HALO, :]
    ext_ref[0:CONV_HALO, :] = tail


def _conv_mixer(p, buf, w, cb, lg, lb, tt):
    b, t, _ = p.shape
    keep = CONV_WIDTH - 1
    return pl.pallas_call(
        functools.partial(_conv_body, tt=tt),
        grid=(b, t // tt),
        in_specs=[pl.BlockSpec((1, tt, CONV_COLS), lambda i, j: (i, j, 0)),
                  pl.BlockSpec((1, keep, CONV_CH), lambda i, j: (i, 0, 0)),
                  _const_spec((CONV_WIDTH, CONV_CH)), _const_spec((1, CONV_CH)),
                  _const_spec((1, CONV_CH)), _const_spec((1, CONV_CH))],
        out_specs=[pl.BlockSpec((1, tt, CONV_CH), lambda i, j: (i, j, 0)),
                   pl.BlockSpec((1, keep, CONV_CH), lambda i, j: (i, 0, 0))],
        out_shape=[jax.ShapeDtypeStruct((b, t, CONV_CH), BF16),
                   jax.ShapeDtypeStruct((b, keep, CONV_CH), F32)],
        scratch_shapes=[pltpu.VMEM((tt + CONV_HALO, CONV_CH), F32)],
        compiler_params=_cparams(("arbitrary", "arbitrary")),
        name="conv_mixer",
    )(p, buf, w, cb.reshape(1, -1), lg.reshape(1, -1), lb.reshape(1, -1))


def _rwkv_pre_body(p_ref, prev_ref, mu_ref, w0_ref, wup_ref, a0_ref, aup_ref, gup_ref, kk_ref, ka_ref, rk_ref,
                   *outs, tiles_per_seq):
    heads_first = tiles_per_seq > 0
    p = p_ref[...]
    if heads_first:
        *outs, carry_ref = outs

        @pl.when(pl.program_id(0) % tiles_per_seq == 0)
        def _():
            carry_ref[...] = prev_ref[0]

        first_row = lax.broadcasted_iota(jnp.int32, p.shape, 0) == 0
        prev = jnp.where(first_row, carry_ref[...], pltpu.roll(p, 1, 0))
        carry_ref[...] = p[p.shape[0] - 1:p.shape[0], :]
    else:
        prev = prev_ref[...]
    pm = p + (prev - p) * mu_ref[...]
    d = RWKV_DIM
    r = pm[:, 0:d]
    k = pm[:, d:2 * d]
    v = pm[:, 2 * d:3 * d]
    xw = pm[:, 3 * d:3 * d + DECAY_LORA]
    xa = pm[:, 3 * d + DECAY_LORA:3 * d + DECAY_LORA + AAA_LORA]
    xg = pm[:, 3 * d + DECAY_LORA + AAA_LORA:RWKV_COLS]
    bd = _head_blockdiag(d)
    w_log = -_softplus(-(w0_ref[...] + _dot3(jnp.tanh(xw), wup_ref[...]))) - 0.5
    log_decay = -jnp.exp(w_log)
    a = _sigmoid(a0_ref[...] + _dot3(xa, aup_ref[...]))
    g = _dot3(_sigmoid(xg), gup_ref[...])
    kk = k * kk_ref[...]
    k = k * (1.0 + (a - 1.0) * ka_ref[...])
    kk = kk / jnp.maximum(jnp.sqrt(_dot_exact_rhs(kk * kk, bd)), 1e-12)
    steps = (r, log_decay, k, v, kk, kk * a)
    outs[-2][...] = g
    outs[-1][...] = _dot_exact_rhs(r * k * rk_ref[...], bd) * v
    if not heads_first:
        for ref, val in zip(outs[:6], steps):
            ref[...] = val
        return
    head = lambda h: slice(h * HEAD_DIM, (h + 1) * HEAD_DIM)
    for ref, val in zip(outs[:6], steps):
        for h in range(RWKV_HEADS):
            ref[h] = val[:, head(h)]
    for ref, val in zip(outs[6:9], (log_decay, kk, v)):
        val_t = val.T
        for h in range(RWKV_HEADS):
            ref[h] = val_t[head(h), :]


def _rwkv_pre(p, prev, mu, w0, w_up, a0, a_up, g_up, k_k, k_a, r_k, tm, seq_len=None):
    n = p.shape[0]
    vec = lambda a: a.reshape(1, -1)
    d = RWKV_DIM
    if seq_len is None:
        per_seq = 0
        prev_spec = _row_spec(tm, RWKV_COLS)
        scratch = []
        out_specs = [_row_spec(tm, d)] * 8
        out_shape = [jax.ShapeDtypeStruct((n, d), F32)] * 8
    else:
        per_seq = seq_len // tm
        prev = prev.reshape(-1, 1, RWKV_COLS)
        prev_spec = pl.BlockSpec((1, 1, RWKV_COLS), lambda i: (i // per_seq, 0, 0))
        scratch = [pltpu.VMEM((1, RWKV_COLS), F32)]
        chains = n // seq_len * RWKV_HEADS
        slab = pl.BlockSpec((RWKV_HEADS, tm, HEAD_DIM), lambda i: (i // per_seq, i % per_seq, 0))
        slab_t = pl.BlockSpec((RWKV_HEADS, HEAD_DIM, tm), lambda i: (i // per_seq, 0, i % per_seq))
        out_specs = [slab] * 6 + [slab_t] * 3 + [_row_spec(tm, d)] * 2
        out_shape = ([jax.ShapeDtypeStruct((chains, seq_len, HEAD_DIM), F32)] * 6
                     + [jax.ShapeDtypeStruct((chains, HEAD_DIM, seq_len), F32)] * 3
                     + [jax.ShapeDtypeStruct((n, d), F32)] * 2)
    return pl.pallas_call(
        functools.partial(_rwkv_pre_body, tiles_per_seq=per_seq),
        grid=(n // tm,),
        in_specs=[_row_spec(tm, RWKV_COLS), prev_spec, _const_spec((1, RWKV_COLS)),
                  _const_spec((1, d)), _const_spec((DECAY_LORA, d)), _const_spec((1, d)),
                  _const_spec((AAA_LORA, d)), _const_spec((GATE_LORA, d)),
                  _const_spec((1, d)), _const_spec((1, d)), _const_spec((1, d))],
        out_specs=out_specs,
        out_shape=out_shape,
        scratch_shapes=scratch,
        compiler_params=_cparams(("arbitrary",)),
        name="rwkv_pre",
    )(p, prev, vec(mu), vec(w0), w_up, vec(a0), a_up, g_up, vec(k_k), vec(k_a), vec(r_k))


def _wkv_body(r_ref, w_ref, k_ref, v_ref, kk_ref, b_ref, s0_ref, y_ref, sT_ref, s_scr, *, nc, tc):
    ti = pl.program_id(1)

    @pl.when(ti == 0)
    def _():
        s_scr[...] = s0_ref[...]

    eye = (lax.broadcasted_iota(jnp.int32, (HEAD_DIM, HEAD_DIM), 0)
           == lax.broadcasted_iota(jnp.int32, (HEAD_DIM, HEAD_DIM), 1))

    def step(t, carry):
        for c in range(nc):
            row = lambda ref: ref[c, pl.ds(t, 1), :]
            s = s_scr[c]
            sa = jnp.sum(s * row(kk_ref), axis=1, keepdims=True)
            vcol = jnp.sum(jnp.where(eye, row(v_ref), 0.0), axis=1, keepdims=True)
            s = s * jnp.exp(row(w_ref)) - sa * row(b_ref) + vcol * row(k_ref)
            s_scr[c] = s
            ycol = jnp.sum(s * row(r_ref), axis=1, keepdims=True)
            y_ref[c, pl.ds(t, 1), :] = jnp.sum(jnp.where(eye, ycol, 0.0), axis=0, keepdims=True)
        return carry

    lax.fori_loop(0, tc, step, 0)

    @pl.when(ti == pl.num_programs(1) - 1)
    def _():
        sT_ref[...] = s_scr[...]


def _wkv(r, w, k, v, kk, b, s0, nc, tc):
    n, t, _ = r.shape
    seq = pl.BlockSpec((nc, tc, HEAD_DIM), lambda i, j: (i, j, 0))
    st = pl.BlockSpec((nc, HEAD_DIM, HEAD_DIM), lambda i, j: (i, 0, 0))
    return pl.pallas_call(
        functools.partial(_wkv_body, nc=nc, tc=tc),
        grid=(n // nc, t // tc),
        in_specs=[seq] * 6 + [st],
        out_specs=[seq, st],
        out_shape=[jax.ShapeDtypeStruct((n, t, HEAD_DIM), F32),
                   jax.ShapeDtypeStruct((n, HEAD_DIM, HEAD_DIM), F32)],
        scratch_shapes=[pltpu.VMEM((nc, HEAD_DIM, HEAD_DIM), F32)],
        compiler_params=_cparams(("arbitrary", "arbitrary")),
        name="wkv",
    )(r, w, k, v, kk, b, s0)


def _mm(a, b):
    return _dot3(a, b)


def _mm_nt(a, b):
    ah, al = _split2(a)
    bh, bl = _split2(b)
    return _dot_nt(ah, bh) + (_dot_nt(ah, bl) + _dot_nt(al, bh))


def _split3(x):
    p1 = x.astype(BF16)
    r1 = x - p1.astype(F32)
    p2 = r1.astype(BF16)
    p3 = (r1 - p2.astype(F32)).astype(BF16)
    return p1, p2, p3


def _wkv_chunk_body(r_ref, lw_ref, k_ref, v_ref, kk_ref, b_ref, lwt_ref, kkt_ref, vt_ref, s0_ref,
                    y_ref, sT_ref, s_scr, *, nc):
    ti = pl.program_id(1)
    c_len = WKV_CHUNK

    @pl.when(ti == 0)
    def _():
        s_scr[...] = s0_ref[...]

    row = lax.broadcasted_iota(jnp.int32, (c_len, c_len), 0)
    col = lax.broadcasted_iota(jnp.int32, (c_len, c_len), 1)
    strict = row > col
    incl = row >= col
    lower = incl.astype(BF16)
    upper = (row <= col).astype(BF16)
    eye = (row == col).astype(F32)

    n_sub = WKV_BLOCK // c_len
    units = [(j, c) for j in range(n_sub) for c in range(nc)]
    each = lambda f, *lists: [f(*args) for args in zip(*lists)]
    ts = lambda j: slice(j * c_len, (j + 1) * c_len)
    rows = lambda ref: [ref[c, ts(j), :] for j, c in units]
    cols = lambda ref: [ref[c, :, ts(j)] for j, c in units]
    r, lw, k, v, kk, b = (rows(ref) for ref in (r_ref, lw_ref, k_ref, v_ref, kk_ref, b_ref))
    lwt, kkt, vt = (cols(ref) for ref in (lwt_ref, kkt_ref, vt_ref))
    cum = each(lambda x: sum(_dot(lower, piece) for piece in _split3(x)), lw)
    cumt = each(lambda x: sum(_dot(piece, upper) for piece in _split3(x)), lwt)
    w_incl = each(jnp.exp, cum)
    w_last = each(lambda w: w[c_len - 1:c_len, :], w_incl)
    inv = each(lambda x: jnp.exp(-x), cum)
    qt = each(lambda x, cs, l: x * jnp.exp(cs - l), kk, cum, lw)
    qtt = each(lambda x, cs, l: x * jnp.exp(cs - l), kkt, cumt, lwt)
    rt = each(jnp.multiply, r, w_incl)
    kt = each(jnp.multiply, k, inv)
    bt = each(jnp.multiply, b, inv)
    a_qb = each(lambda x, y: jnp.where(strict, _mm_nt(x, y), 0.0), qt, bt)
    a_qk = each(lambda x, y: jnp.where(strict, _mm_nt(x, y), 0.0), qt, kt)
    a_rk = each(lambda x, y: jnp.where(incl, _mm_nt(x, y), 0.0), rt, kt)
    a_rb = each(lambda x, y: jnp.where(incl, _mm_nt(x, y), 0.0), rt, bt)
    tinv = each(lambda n: eye - n, a_qb)
    power = a_qb
    for _ in range(WKV_CHUNK_LOG2 - 1):
        power = each(_mm, power, power)
        tinv = each(lambda t, p: t + _mm(t, p), tinv, power)
    tq = each(_mm, tinv, qt)
    av = each(_mm, a_qk, v)
    av_t = each(_mm_nt, vt, a_qk)
    tq_t = each(_mm_nt, qtt, tinv)
    tav = each(_mm, tinv, av)
    tav_t = each(_mm_nt, av_t, tinv)
    ark_v = each(_mm, a_rk, v)
    rq = each(lambda x, a, y: x - _mm(a, y), rt, a_rb, tq)
    yv = each(lambda x, a, y: x - _mm(a, y), ark_v, a_rb, tav)
    gain = each(lambda x, y, w: _mm(x, y * w), vt, kt, w_last)
    bw = each(jnp.multiply, bt, w_last)
    for j in range(n_sub):
        ids = [j * nc + c for c in range(nc)]
        s = [s_scr[c] for c in range(nc)]
        sa_t = [_mm(s[c], tq_t[i]) + tav_t[i] for c, i in enumerate(ids)]
        ys = [_mm_nt(rq[i], s[c]) + yv[i] for c, i in enumerate(ids)]
        new = [s[c] * w_last[i] + gain[i] - _mm(sa_t[c], bw[i]) for c, i in enumerate(ids)]
        for c, i in enumerate(ids):
            y_ref[c, ts(j), :] = ys[c]
            s_scr[c] = new[c]

    @pl.when(ti == pl.num_programs(1) - 1)
    def _():
        sT_ref[...] = s_scr[...]


def _wkv_chunked(r, lw, k, v, kk, b, lwt, kkt, vt, s0, nc):
    n, t, _ = r.shape
    seq = pl.BlockSpec((nc, WKV_BLOCK, HEAD_DIM), lambda i, j: (i, j, 0))
    seq_t = pl.BlockSpec((nc, HEAD_DIM, WKV_BLOCK), lambda i, j: (i, 0, j))
    st = pl.BlockSpec((nc, HEAD_DIM, HEAD_DIM), lambda i, j: (i, 0, 0))
    return pl.pallas_call(
        functools.partial(_wkv_chunk_body, nc=nc),
        grid=(n // nc, t // WKV_BLOCK),
        in_specs=[seq] * 6 + [seq_t] * 3 + [st],
        out_specs=[seq, st],
        out_shape=[jax.ShapeDtypeStruct((n, t, HEAD_DIM), F32),
                   jax.ShapeDtypeStruct((n, HEAD_DIM, HEAD_DIM), F32)],
        scratch_shapes=[pltpu.VMEM((nc, HEAD_DIM, HEAD_DIM), F32)],
        compiler_params=_cparams(("arbitrary", "arbitrary")),
        name="wkv_chunked",
    )(r, lw, k, v, kk, b, lwt, kkt, vt, s0)


def _sb_block(q, kb, vb, bias, rest, umat, mask):
    z = _dot_nt(q, kb) + bias
    sp = _softplus(z)
    counted = sp if mask is None else jnp.where(mask, sp, 0.0)
    after = _dot(counted.astype(BF16), umat)
    a = jnp.exp((z - sp) + after + rest)
    if mask is not None:
        a = jnp.where(mask, a, 0.0)
    out = _dot(a.astype(BF16), vb)
    return out, after[:, 0:1] - counted[:, 0:1]


def _attn_prompt_body(q_ref, k_ref, v_ref, bias_ref, o_ref, *, tile):
    qi = pl.program_id(1)
    heads = range(ATT_HEADS)
    hs = lambda a, h: a[:, h * HEAD_DIM:(h + 1) * HEAD_DIM]
    q_all = q_ref[0]
    qs = [hs(q_all, h) for h in heads]
    biases = [bias_ref[h] for h in heads]
    umat = _suffix_matrix(tile)
    causal = (lax.broadcasted_iota(jnp.int32, (tile, tile), 1)
              < lax.broadcasted_iota(jnp.int32, (tile, tile), 0))

    def visit(j, accs, rests, mask):
        start = pl.multiple_of(j * tile, tile)
        kb = k_ref[0, pl.ds(start, tile), :]
        vb = v_ref[0, pl.ds(start, tile), :]
        zs = [_dot_nt(qs[h], hs(kb, h)) + biases[h] for h in heads]
        sps = [_softplus(z) for z in zs]
        counted = sps if mask is None else [jnp.where(mask, sp, 0.0) for sp in sps]
        afters = [_dot(sp.astype(BF16), umat) for sp in counted]
        probs = [jnp.exp((zs[h] - sps[h]) + afters[h] + rests[h]) for h in heads]
        if mask is not None:
            probs = [jnp.where(mask, a, 0.0) for a in probs]
        outs = [_dot(probs[h].astype(BF16), hs(vb, h)) for h in heads]
        new_accs = tuple(accs[h] + outs[h] for h in heads)
        new_rests = tuple(rests[h] + afters[h][:, 0:1] - counted[h][:, 0:1] for h in heads)
        return new_accs, new_rests

    accs = tuple(jnp.zeros((tile, HEAD_DIM), F32) for _ in heads)
    rests = tuple(jnp.zeros((tile, 1), F32) for _ in heads)
    accs, rests = visit(qi, accs, rests, causal)
    accs, rests = lax.fori_loop(0, qi, lambda jj, c: visit(qi - 1 - jj, c[0], c[1], None), (accs, rests))
    r = lax.broadcasted_iota(jnp.int32, (HEAD_DIM, ATT_DIM), 0)
    c = lax.broadcasted_iota(jnp.int32, (HEAD_DIM, ATT_DIM), 1)
    out = jnp.zeros((tile, ATT_DIM), F32)
    for h in heads:
        out = out + _dot(accs[h].astype(BF16), (c == r + h * HEAD_DIM).astype(BF16))
    o_ref[0] = out.astype(o_ref.dtype)


def _attn_prompt(qb, kb, vb, bias):
    b, t, d = qb.shape
    tile = ATT_TILE
    bias_b = jnp.broadcast_to(bias.astype(F32)[:, None, None], (ATT_HEADS, 1, tile))
    qspec = pl.BlockSpec((1, tile, d), lambda i, l: (i, l, 0))
    kvspec = pl.BlockSpec((1, t, d), lambda i, l: (i, 0, 0))
    return pl.pallas_call(
        functools.partial(_attn_prompt_body, tile=tile),
        grid=(b, t // tile),
        in_specs=[qspec, kvspec, kvspec, _const_spec((ATT_HEADS, 1, tile))],
        out_specs=qspec,
        out_shape=jax.ShapeDtypeStruct((b, t, d), BF16),
        compiler_params=_cparams(("arbitrary", "arbitrary")),
        name="attn_prompt",
    )(qb, kb, vb, bias_b)


def _attn_decode_body(pt_ref, q_ref, bias_ref, kn_ref, vn_ref, *refs, npg, nq):
    k_refs = refs[:npg]
    v_refs = refs[npg:2 * npg]
    o_ref, rest_ref, acc_ref = refs[2 * npg:]
    jb = pl.program_id(1)
    rows = ATT_HEADS * Q_PAD
    pages = range(npg)
    spread = lambda tot: jnp.broadcast_to(tot, (rows, PAGE_SIZE))
    q = q_ref[0]
    bias = bias_ref[...]
    umat = _suffix_matrix(PAGE_SIZE)

    @pl.when(jb == 0)
    def _():
        kidx = lax.broadcasted_iota(jnp.int32, (rows, PAGE_SIZE), 1)
        qidx = lax.broadcasted_iota(jnp.int32, (rows, PAGE_SIZE), 0) % Q_PAD
        out, tot = _sb_block(q, kn_ref[0].astype(BF16), vn_ref[0].astype(BF16), bias,
                             jnp.zeros((rows, 1), F32), umat, (kidx < qidx) & (kidx < nq))
        acc_ref[...] = out
        rest_ref[...] = spread(tot)

    zs = [_dot_nt(q, k_refs[s][0, 0].astype(BF16)) + bias for s in pages]
    sps = [_softplus(z) for z in zs]
    afters = [_dot(sp.astype(BF16), umat) for sp in sps]
    rest = rest_ref[...]
    probs = []
    for s in pages:
        probs.append(jnp.exp((zs[s] - sps[s]) + afters[s] + rest).astype(BF16))
        rest = rest + spread(afters[s][:, 0:1] - sps[s][:, 0:1])
    rest_ref[...] = rest
    out = _dot(probs[0], v_refs[0][0, 0].astype(BF16))
    for s in pages[1:]:
        out = out + _dot(probs[s], v_refs[s][0, 0].astype(BF16))
    acc_ref[...] += out

    @pl.when(jb == pl.num_programs(1) - 1)
    def _():
        head_of_row = lax.broadcasted_iota(jnp.int32, (rows, ATT_DIM), 0) // Q_PAD
        head_of_col = lax.broadcasted_iota(jnp.int32, (rows, ATT_DIM), 1) // HEAD_DIM
        kept = jnp.where(head_of_row == head_of_col, acc_ref[...], 0.0)
        out = kept[0:Q_PAD]
        for h in range(1, ATT_HEADS):
            out = out + kept[h * Q_PAD:(h + 1) * Q_PAD]
        o_ref[0] = out.astype(o_ref.dtype)


def _attn_decode(li, qb, k_new, v_new, cache_k, cache_v, page_table, bias):
    b, t, _ = qb.shape
    n_pages = page_table.shape[1]
    npg = PAGES_PER_STEP
    rows = ATT_HEADS * Q_PAD
    qpad = jnp.pad(qb, ((0, 0), (0, Q_PAD - t), (0, 0)))
    head_of_col = jnp.arange(ATT_DIM) // HEAD_DIM
    qbd = jnp.where(head_of_col[None, None, None, :] == jnp.arange(ATT_HEADS)[None, :, None, None],
                    qpad[:, None], jnp.zeros((), BF16)).reshape(b, rows, ATT_DIM)
    bias_b = jnp.broadcast_to(jnp.repeat(bias.astype(F32), Q_PAD)[:, None], (rows, PAGE_SIZE))
    kn = jnp.pad(k_new, ((0, 0), (0, PAGE_SIZE - t), (0, 0)))
    vn = jnp.pad(v_new, ((0, 0), (0, PAGE_SIZE - t), (0, 0)))

    def page_spec(s):
        return pl.BlockSpec((1, 1, PAGE_SIZE, ATT_DIM),
                            lambda i, j, pt: (li, pt[i * n_pages + n_pages - 1 - (j * npg + s)], 0, 0))

    per_b = lambda r, c: pl.BlockSpec((1, r, c), lambda i, j, pt: (i, 0, 0))
    grid_spec = pltpu.PrefetchScalarGridSpec(
        num_scalar_prefetch=1,
        grid=(b, n_pages // npg),
        in_specs=[per_b(rows, ATT_DIM), pl.BlockSpec((rows, PAGE_SIZE), lambda i, j, pt: (0, 0)),
                  per_b(PAGE_SIZE, ATT_DIM), per_b(PAGE_SIZE, ATT_DIM)]
                 + [page_spec(s) for s in range(npg)] * 2,
        out_specs=per_b(Q_PAD, ATT_DIM),
        scratch_shapes=[pltpu.VMEM((rows, PAGE_SIZE), F32), pltpu.VMEM((rows, ATT_DIM), F32)],
    )
    out = pl.pallas_call(
        functools.partial(_attn_decode_body, npg=npg, nq=t),
        grid_spec=grid_spec,
        out_shape=jax.ShapeDtypeStruct((b, Q_PAD, ATT_DIM), BF16),
        compiler_params=_cparams(("arbitrary", "arbitrary")),
        name="attn_decode",
    )(page_table.reshape(-1), qbd, bias_b, kn, vn, *([cache_k] * npg), *([cache_v] * npg))
    return out[:, :t]


def _outproj_body(x_ref, yc_ref, yw_ref, bonus_ref, g_ref, ya_ref, lng_ref, lnb_ref, w_ref, o_ref):
    bd = _head_blockdiag(RWKV_DIM)
    if len(yw_ref.shape) == 3:
        r = lax.broadcasted_iota(jnp.int32, (HEAD_DIM, RWKV_DIM), 0)
        c = lax.broadcasted_iota(jnp.int32, (HEAD_DIM, RWKV_DIM), 1)
        y = sum(_dot_exact_rhs(yw_ref[h], (c == r + h * HEAD_DIM).astype(BF16)) for h in range(RWKV_HEADS))
    else:
        y = yw_ref[...]
    inv = 1.0 / HEAD_DIM
    d = y - _dot_exact_rhs(y, bd) * inv
    var = _dot_exact_rhs(d * d, bd) * inv
    yn = d * lax.rsqrt(var + GN_EPS) * lng_ref[...] + lnb_ref[...]
    yr = ((yn + bonus_ref[...]) * g_ref[...]).astype(BF16)
    c0, c1 = CONV_CH, CONV_CH + RWKV_DIM
    o_ref[...] = (x_ref[...] + _dot(yc_ref[...], w_ref[0:c0, :]) + _dot(yr, w_ref[c0:c1, :])
                  + _dot(ya_ref[...], w_ref[c1:D_MODEL, :]))


def _outproj(x, yc, yw, bonus, g, ya, ln_g, ln_b, w_b, tm):
    n = x.shape[0]
    if yw.ndim == 3:
        per_seq = yw.shape[1] // tm
        yw_spec = pl.BlockSpec((RWKV_HEADS, tm, HEAD_DIM), lambda i: (i // per_seq, i % per_seq, 0))
    else:
        yw_spec = _row_spec(tm, RWKV_DIM)
    return pl.pallas_call(
        _outproj_body,
        grid=(n // tm,),
        in_specs=[_row_spec(tm, D_MODEL), _row_spec(tm, CONV_CH), yw_spec, _row_spec(tm, RWKV_DIM),
                  _row_spec(tm, RWKV_DIM), _row_spec(tm, ATT_DIM), _const_spec((1, RWKV_DIM)),
                  _const_spec((1, RWKV_DIM)), _const_spec((D_MODEL, D_MODEL))],
        out_specs=_row_spec(tm, D_MODEL),
        out_shape=jax.ShapeDtypeStruct((n, D_MODEL), F32),
        compiler_params=_cparams(("arbitrary",)),
        name="outproj",
    )(x, yc, yw, bonus, g, ya, ln_g.reshape(1, -1), ln_b.reshape(1, -1), w_b)


SUBLANES = 8
KEY_GROUPS = N_KEYS // SUBLANES


def _oddeven_merge(lo, hi, r):
    step = r * 2
    if step < hi - lo:
        yield from _oddeven_merge(lo, hi, step)
        yield from _oddeven_merge(lo + r, hi, step)
        yield from [(i, i + r) for i in range(lo + r, hi - r, step)]
    else:
        yield (lo, lo + r)


def _oddeven_sort(lo, hi):
    if hi - lo >= 1:
        mid = lo + (hi - lo) // 2
        yield from _oddeven_sort(lo, mid)
        yield from _oddeven_sort(mid + 1, hi)
        yield from _oddeven_merge(lo, hi, 1)


SORT_NET = tuple(_oddeven_sort(0, PEER_TOPK - 1))
BITONIC_NET = tuple((i, i + s) for s in (8, 4, 2, 1) for i in range(PEER_TOPK) if not i & s)
SUM_SLOTS = tuple((a, b) for a in range(PEER_TOPK) for b in range(PEER_TOPK) if (a + 1) * (b + 1) <= PEER_TOPK)


def _exchange(vals, i, j):
    vals[i], vals[j] = jnp.maximum(vals[i], vals[j]), jnp.minimum(vals[i], vals[j])


def _sublane_max(x):
    for shift in (4, 2, 1):
        x = jnp.maximum(x, pltpu.roll(x, shift, 0))
    return x


def _top16_sorted(groups):
    vals = list(groups)
    for i, j in SORT_NET:
        _exchange(vals, i, j)
    for shift in (4, 2, 1):
        moved = [pltpu.roll(v, shift, 0) for v in vals]
        vals = [jnp.maximum(vals[i], moved[PEER_TOPK - 1 - i]) for i in range(PEER_TOPK)]
        for i, j in BITONIC_NET:
            _exchange(vals, i, j)
    return vals


def _peer_route_body(x_ref, g_ref, wq_ref, sk_ref, h_out, rank_out, e2_out, cnt_out, f1_out):
    tm = x_ref.shape[0]
    kk = PEER_TOPK
    hb = _rmsnorm(x_ref[...], g_ref[...]).astype(BF16)
    h_out[...] = hb
    sub = lax.broadcasted_iota(jnp.int32, (SUBLANES, tm), 0)
    split = lambda s: [s[i * SUBLANES:(i + 1) * SUBLANES, :] for i in range(KEY_GROUPS)]
    join = lambda parts: jnp.concatenate(parts, axis=0)
    neg_inf = jnp.full((SUBLANES, tm), -jnp.inf, F32)
    half = PEER_DK // 2
    for h in range(PEER_HEADS):
        q = _dot(hb, wq_ref[:, h * PEER_DK:(h + 1) * PEER_DK])
        s1 = split(_dot_nt(sk_ref[2 * h], q[:, :half].astype(BF16)))
        s2 = split(_dot_nt(sk_ref[2 * h + 1], q[:, half:].astype(BF16)))
        v1 = _top16_sorted(s1)
        v2 = _top16_sorted(s2)
        sums = {ab: v1[ab[0]] + v2[ab[1]] for ab in SUM_SLOTS}
        packs = []
        for j in range(0, len(SUM_SLOTS), SUBLANES):
            pack = neg_inf
            for s, ab in enumerate(SUM_SLOTS[j:j + SUBLANES]):
                pack = jnp.where(sub == s, sums[ab], pack)
            packs.append(pack)
        best = []
        for _ in range(kk):
            m = packs[0]
            for p in packs[1:]:
                m = jnp.maximum(m, p)
            m = _sublane_max(m)
            best.append(m)
            packs = [jnp.where(p == m, -jnp.inf, p) for p in packs]
        z = jnp.ones((SUBLANES, tm), F32)
        for i in range(1, kk):
            z = z + jnp.exp(best[i] - best[0])
        half_inv_z = 0.5 / z
        cnt = [jnp.zeros((SUBLANES, tm), F32) for _ in range(kk)]
        for a, b in SUM_SLOTS:
            cnt[a] = cnt[a] + jnp.where(sums[(a, b)] >= best[kk - 1], 1.0, 0.0)
        cnt_full, f1_full, rank_full, e2_full = [], [], [], []
        for g in range(KEY_GROUPS):
            c = jnp.zeros((SUBLANES, tm), F32)
            for a in range(kk):
                c = jnp.where(s1[g] == v1[a], cnt[a], c)
            cnt_full.append(c)
            f1_full.append(jnp.where(s1[g] >= v1[kk - 1], jnp.exp(s1[g] - v1[0]), 0.0) * half_inv_z)
            rk = jnp.full((SUBLANES, tm), float(kk), F32)
            for b in reversed(range(kk)):
                rk = jnp.where(s2[g] == v2[b], float(b), rk)
            rank_full.append(rk)
            e2_full.append(jnp.where(s2[g] >= v2[kk - 1], jnp.exp(s2[g] - v2[0]), 0.0))
        cnt_out[h] = join(cnt_full)
        f1_out[h] = join(f1_full)
        rank_out[h] = join(rank_full).astype(BF16)
        e2_out[h] = join(e2_full).astype(BF16)


def _peer_route(x, g, wq_b, sk_b, tm):
    n = x.shape[0]
    keyed = pl.BlockSpec((PEER_HEADS, N_KEYS, tm), lambda i: (0, 0, i))
    keyed_shape = lambda dt: jax.ShapeDtypeStruct((PEER_HEADS, N_KEYS, n), dt)
    return pl.pallas_call(
        _peer_route_body,
        grid=(n // tm,),
        in_specs=[_row_spec(tm, D_MODEL), _const_spec((1, D_MODEL)), _const_spec((D_MODEL, PEER_HEADS * PEER_DK)),
                  _const_spec((2 * PEER_HEADS, N_KEYS, PEER_DK // 2))],
        out_specs=[_row_spec(tm, D_MODEL), keyed, keyed, keyed, keyed],
        out_shape=[jax.ShapeDtypeStruct((n, D_MODEL), BF16), keyed_shape(BF16), keyed_shape(BF16),
                   keyed_shape(F32), keyed_shape(F32)],
        compiler_params=_cparams(("arbitrary",)),
        name="peer_route",
    )(x, g.reshape(1, -1), wq_b, sk_b)


GATE_ROWS = 16


GELU_A = 0.7978845608028654
GELU_B = GELU_A * 0.044715


def _gelu_tanh_doubled(x):
    return x + x * jnp.tanh(x * (GELU_A + GELU_B * (x * x)))


def _peer_expert_body(x_ref, h_ref, rank_ref, e2_ref, cnt_ref, f1_ref, u_ref, vt_ref, o_ref, acc_ref):
    ec = pl.program_id(1)

    @pl.when(ec == 0)
    def _():
        acc_ref[...] = jnp.zeros(acc_ref.shape, F32)

    tm = h_ref.shape[0]
    n_grp = 2 if tm % 512 == 0 else 1
    grp = tm // n_grp
    toks = [slice(g * grp, (g + 1) * grp) for g in range(n_grp)]
    hids = [_dot_nt(u_ref[...], h_ref[tk, :]) for tk in toks]
    heads = range(PEER_HEADS)
    for g, tk in enumerate(toks):
        weights = []
        for r in range(PEER_ROWS):
            bcast = lambda ref, h: jnp.broadcast_to(ref[h, 0, r:r + 1, tk], (GATE_ROWS, grp)).astype(BF16)
            cnts = [bcast(cnt_ref, h) for h in heads]
            f1s = [bcast(f1_ref, h) for h in heads]
            for j in range(N_KEYS // GATE_ROWS):
                keys = slice(j * GATE_ROWS, (j + 1) * GATE_ROWS)
                gate = jnp.zeros((GATE_ROWS, grp), BF16)
                for h in heads:
                    e2 = e2_ref[h, keys, tk]
                    gate = gate + jnp.where(rank_ref[h, keys, tk] < cnts[h], e2, jnp.zeros_like(e2)) * f1s[h]
                rows = slice(r * N_KEYS + j * GATE_ROWS, r * N_KEYS + (j + 1) * GATE_ROWS)
                weights.append(gate * _gelu_tanh_doubled(hids[g][rows]).astype(BF16))
        acc_ref[:, tk] += _dot(vt_ref[...], jnp.concatenate(weights, axis=0))

    @pl.when(ec == pl.num_programs(1) - 1)
    def _():
        o_ref[...] = x_ref[...] + acc_ref[...].T


def _peer_expert(x, hb, rank2, e2, cnt, f1, u_b, vt_b, tm):
    n = x.shape[0]
    te = PEER_ROWS * N_KEYS
    ne = u_b.shape[0] // te
    chunked = lambda a: a.reshape(PEER_HEADS, ne, PEER_ROWS, n)
    tok = lambda c: pl.BlockSpec((tm, c), lambda i, e: (i, 0))
    keyed = pl.BlockSpec((PEER_HEADS, N_KEYS, tm), lambda i, e: (0, 0, i))
    row = pl.BlockSpec((PEER_HEADS, 1, PEER_ROWS, tm), lambda i, e: (0, e, 0, i))
    return pl.pallas_call(
        _peer_expert_body,
        grid=(n // tm, ne),
        in_specs=[tok(D_MODEL), tok(D_MODEL), keyed, keyed, row, row,
                  pl.BlockSpec((te, D_MODEL), lambda i, e: (e, 0)),
                  pl.BlockSpec((D_MODEL, te), lambda i, e: (0, e))],
        out_specs=tok(D_MODEL),
        out_shape=jax.ShapeDtypeStruct((n, D_MODEL), F32),
        scratch_shapes=[pltpu.VMEM((D_MODEL, tm), F32)],
        compiler_params=_cparams(("arbitrary", "arbitrary")),
        name="peer_expert",
    )(x, hb, rank2, e2, chunked(cnt), chunked(f1), u_b, vt_b)


def _final_norm_body(x_ref, g_ref, o_ref):
    o_ref[...] = _rmsnorm(x_ref[...], g_ref[...])


def _final_norm(x, g, tm):
    n = x.shape[0]
    return pl.pallas_call(
        _final_norm_body,
        grid=(n // tm,),
        in_specs=[_row_spec(tm, D_MODEL), _const_spec((1, D_MODEL))],
        out_specs=_row_spec(tm, D_MODEL),
        out_shape=jax.ShapeDtypeStruct((n, D_MODEL), F32),
        compiler_params=_cparams(("arbitrary",)),
        name="final_norm",
    )(x, g.reshape(1, -1))


def _heads_first(a, b, t):
    h = a.shape[1] // HEAD_DIM
    return a.reshape(b, t, h, HEAD_DIM).transpose(0, 2, 1, 3).reshape(b * h, t, HEAD_DIM)


def _tokens_first(a, b, t):
    h = a.shape[0] // b
    return a.reshape(b, h, t, HEAD_DIM).transpose(0, 2, 1, 3).reshape(b * t, h * HEAD_DIM)


def _layer(x, b, t, conv_buf, shift_prev, wkv0, attend, lw, tm, conv_tile, wkv_chains, wkv_tile):
    p_conv, p_rwkv, k, v, qb, kb, vb = _inproj(x, lw["norm1_g"], lw["w_in"], tm)

    y_conv, conv_new = _conv_mixer(p_conv.reshape(b, t, CONV_COLS), conv_buf, lw["conv_w"], lw["conv_b"],
                                   lw["conv_ln_g"], lw["conv_ln_b"], conv_tile)

    p3 = p_rwkv.reshape(b, t, RWKV_COLS)
    pre = lambda prev, **kw: _rwkv_pre(p_rwkv, prev, lw["rwkv_mu"], lw["rwkv_w0"], lw["rwkv_w_up"], lw["rwkv_a0"],
                                       lw["rwkv_a_up"], lw["rwkv_g_up"], lw["rwkv_k_k"], lw["rwkv_k_a"],
                                       lw["rwkv_r_k"], tm, **kw)
    s0 = wkv0.reshape(b * RWKV_HEADS, HEAD_DIM, HEAD_DIM)
    if t % WKV_BLOCK == 0 and t % tm == 0:
        *steps, g, bonus = pre(shift_prev, seq_len=t)
        y_wkv, wkv_new = _wkv_chunked(*steps, s0, wkv_chains)
    else:
        prev = jnp.concatenate([shift_prev[:, None, :], p3[:, :-1]], axis=1).reshape(b * t, RWKV_COLS)
        *steps, g, bonus = pre(prev)
        y_wkv, wkv_new = _wkv(*(_heads_first(a, b, t) for a in steps), s0, wkv_chains, wkv_tile)
        y_wkv = _tokens_first(y_wkv, b, t)

    y_att = attend(qb, k, v, kb, vb)

    x = _outproj(x, y_conv.reshape(b * t, CONV_CH), y_wkv, bonus, g, y_att, lw["rwkv_ln_g"], lw["rwkv_ln_b"],
                 lw["w_out"], tm)
    routed = _peer_route(x, lw["norm2_g"], lw["peer_w_query"], lw["peer_sub_keys"], min(tm, 256))
    x = _peer_expert(x, *routed, lw["peer_u"], lw["peer_vt"], tm)
    states = (k.reshape(b, t, ATT_HEADS, HEAD_DIM), v.reshape(b, t, ATT_HEADS, HEAD_DIM), conv_new,
              p3[:, -1], wkv_new.reshape(b, RWKV_HEADS, HEAD_DIM, HEAD_DIM))
    return x, states


def kernel(x_prompt, x_sample, cache_k, cache_v, state_conv, state_shift, state_wkv, page_table, norm1_g, w_in, conv_w, conv_b, conv_ln_g, conv_ln_b, rwkv_mu, rwkv_w0, rwkv_w_up, rwkv_a0, rwkv_a_up, rwkv_g_up, rwkv_k_k, rwkv_k_a, rwkv_r_k, rwkv_ln_g, rwkv_ln_b, att_bias, w_out, norm2_g, peer_w_query, peer_sub_keys, peer_u, peer_v, final_g):
    depth = w_in.shape[0]
    bp, tp, _ = x_prompt.shape
    bs, ts, _ = x_sample.shape
    n_pool = cache_k.shape[1]
    ck = cache_k.reshape(depth, n_pool, PAGE_SIZE, ATT_DIM)
    cv = cache_v.reshape(depth, n_pool, PAGE_SIZE, ATT_DIM)
    xp = x_prompt.reshape(bp * tp, D_MODEL)
    xs = x_sample.reshape(bs * ts, D_MODEL)
    st_p, st_s = [], []
    for li in range(depth):
        lw = dict(
            norm1_g=norm1_g[li], w_in=w_in[li].astype(BF16), conv_w=conv_w[li], conv_b=conv_b[li],
            conv_ln_g=conv_ln_g[li], conv_ln_b=conv_ln_b[li], rwkv_mu=rwkv_mu[li], rwkv_w0=rwkv_w0[li],
            rwkv_w_up=rwkv_w_up[li], rwkv_a0=rwkv_a0[li], rwkv_a_up=rwkv_a_up[li], rwkv_g_up=rwkv_g_up[li],
            rwkv_k_k=rwkv_k_k[li], rwkv_k_a=rwkv_k_a[li], rwkv_r_k=rwkv_r_k[li].reshape(-1),
            rwkv_ln_g=rwkv_ln_g[li], rwkv_ln_b=rwkv_ln_b[li], w_out=w_out[li].astype(BF16), norm2_g=norm2_g[li],
            peer_w_query=peer_w_query[li].astype(BF16),
            peer_sub_keys=peer_sub_keys[li].astype(BF16).reshape(2 * PEER_HEADS, N_KEYS, PEER_DK // 2),
            peer_u=peer_u[li].astype(BF16), peer_vt=peer_v[li].astype(BF16).T)
        bias = att_bias[li]

        def attend_prompt(qb, k, v, kb, vb):
            sq = lambda a: a.reshape(bp, tp, ATT_DIM)
            return _attn_prompt(sq(qb), sq(kb), sq(vb), bias).reshape(bp * tp, ATT_DIM)

        def attend_sample(qb, k, v, kb, vb):
            o = _attn_decode(li, qb.reshape(bs, ts, ATT_DIM), k.reshape(bs, ts, ATT_DIM),
                             v.reshape(bs, ts, ATT_DIM), ck, cv, page_table, bias)
            return o.reshape(bs * ts, ATT_DIM)

        xp, new_p = _layer(xp, bp, tp, jnp.zeros((bp, CONV_WIDTH - 1, CONV_CH), F32),
                           jnp.zeros((bp, RWKV_COLS), F32), jnp.zeros((bp, RWKV_HEADS, HEAD_DIM, HEAD_DIM), F32),
                           attend_prompt, lw, tm=512, conv_tile=512, wkv_chains=RWKV_HEADS, wkv_tile=256)
        xs, new_s = _layer(xs, bs, ts, state_conv[li], state_shift[li], state_wkv[li],
                           attend_sample, lw, tm=bs * ts, conv_tile=ts, wkv_chains=bp * RWKV_HEADS, wkv_tile=ts)
        st_p.append(new_p)
        st_s.append(new_s)
    y_prompt = _final_norm(xp, final_g, 512).reshape(bp, tp, D_MODEL)
    y_sample = _final_norm(xs, final_g, bs * ts).reshape(bs, ts, D_MODEL)
    stack = lambda st, j: jnp.stack([s[j] for s in st], axis=0)
    return (y_prompt, y_sample) + tuple(stack(st_p, j) for j in range(5)) + tuple(stack(st_s, j) for j in range(5))
```

```python
import functools

import jax
import jax.numpy as jnp
from jax import lax
from jax.experimental import pallas as pl
from jax.experimental.pallas import tpu as pltpu

F32 = jnp.float32
BF16 = jnp.bfloat16

D_MODEL = 1024
HEAD_DIM = 64
CONV_CH = 256
CONV_WIDTH = 31
CONV_HALO = 32
RWKV_HEADS = 6
RWKV_DIM = 384
ATT_HEADS = 6
ATT_DIM = 384
DECAY_LORA = 64
AAA_LORA = 64
GATE_LORA = 128
CONV_COLS = 512
RWKV_COLS = 1408
IN_COLS = 3072
PAGE_SIZE = 128
PEER_HEADS = 8
PEER_DK = 256
N_KEYS = 128
PEER_TOPK = 16
RMS_EPS = 1e-5
LN_EPS = 1e-5
GN_EPS = 64e-5

ATT_TILE = 256
PAGES_PER_STEP = 32
Q_PAD = 8
PEER_ROWS = 16
WKV_CHUNK_LOG2 = 6
WKV_CHUNK = 1 << WKV_CHUNK_LOG2
WKV_BLOCK = 128
VMEM_LIMIT = 56 * 1024 * 1024


def _cparams(sem):
    return pltpu.CompilerParams(dimension_semantics=sem, vmem_limit_bytes=VMEM_LIMIT)


def _sigmoid(x):
    return 1.0 / (1.0 + jnp.exp(-x))


def _softplus(x):
    return jnp.maximum(x, 0.0) + jnp.log(1.0 + jnp.exp(-jnp.abs(x)))


def _dot(a, b):
    return jnp.dot(a, b, preferred_element_type=F32)


def _dot_nt(a, b):
    return lax.dot_general(a, b, (((1,), (1,)), ((), ())), preferred_element_type=F32)


def _split2(x):
    hi = x.astype(BF16)
    lo = (x - hi.astype(F32)).astype(BF16)
    return hi, lo


def _dot3(a, b):
    ah, al = _split2(a)
    bh, bl = _split2(b)
    return _dot(ah, bh) + (_dot(ah, bl) + _dot(al, bh))


def _dot_exact_rhs(x, m):
    hi, lo = _split2(x)
    return _dot(hi, m) + _dot(lo, m)


def _rmsnorm(x, g):
    return x * lax.rsqrt(jnp.mean(x * x, axis=-1, keepdims=True) + RMS_EPS) * g


def _head_blockdiag(n):
    r = lax.broadcasted_iota(jnp.int32, (n, n), 0) // HEAD_DIM
    c = lax.broadcasted_iota(jnp.int32, (n, n), 1) // HEAD_DIM
    return (r == c).astype(BF16)


def _suffix_matrix(n):
    r = lax.broadcasted_iota(jnp.int32, (n, n), 0)
    c = lax.broadcasted_iota(jnp.int32, (n, n), 1)
    return jnp.where(r > c, -1.0, 0.0).astype(BF16)


def _row_spec(tm, cols):
    return pl.BlockSpec((tm, cols), lambda i: (i, 0))


def _const_spec(shape):
    return pl.BlockSpec(shape, lambda *_: (0,) * len(shape))


def _inproj_body(x_ref, g_ref, w_ref, conv_ref, rwkv_ref, k_ref, v_ref, qb_ref, kb_ref, vb_ref):
    h = _rmsnorm(x_ref[...], g_ref[...]).astype(BF16)
    o0, o1, o2, o3 = CONV_COLS, CONV_COLS + RWKV_COLS, CONV_COLS + RWKV_COLS + ATT_DIM, IN_COLS - ATT_DIM
    conv_ref[...] = _dot(h, w_ref[:, 0:o0])
    rwkv_ref[...] = _dot(h, w_ref[:, o0:o1])
    q = _dot(h, w_ref[:, o1:o2])
    k = _dot(h, w_ref[:, o2:o3])
    v = _dot(h, w_ref[:, o3:IN_COLS])
    k_ref[...] = k
    v_ref[...] = v
    qb_ref[...] = (q * (HEAD_DIM ** -0.5)).astype(BF16)
    kb_ref[...] = k.astype(BF16)
    vb_ref[...] = v.astype(BF16)


def _inproj(x, g, w_b, tm):
    n = x.shape[0]
    widths = (CONV_COLS, RWKV_COLS, ATT_DIM, ATT_DIM, ATT_DIM, ATT_DIM, ATT_DIM)
    dtypes = (F32, F32, F32, F32, BF16, BF16, BF16)
    return pl.pallas_call(
        _inproj_body,
        grid=(n // tm,),
        in_specs=[_row_spec(tm, D_MODEL), _const_spec((1, D_MODEL)), _const_spec((D_MODEL, IN_COLS))],
        out_specs=[_row_spec(tm, c) for c in widths],
        out_shape=[jax.ShapeDtypeStruct((n, c), d) for c, d in zip(widths, dtypes)],
        compiler_params=_cparams(("arbitrary",)),
        name="inproj",
    )(x, g.reshape(1, -1), w_b)


def _conv_body(p_ref, buf_ref, w_ref, cb_ref, lg_ref, lb_ref, y_ref, nb_ref, ext_ref, *, tt):
    keep = CONV_WIDTH - 1
    lead = CONV_HALO - keep

    @pl.when(pl.program_id(1) == 0)
    def _():
        ext_ref[0:CONV_HALO, :] = jnp.zeros((CONV_HALO, CONV_CH), F32)
        ext_ref[lead:CONV_HALO, :] = buf_ref[0]

    p = p_ref[0]
    ext_ref[CONV_HALO:CONV_HALO + tt, :] = p[:, :CONV_CH] * _sigmoid(p[:, CONV_CH:])
    acc = jnp.zeros((tt, CONV_CH), F32)
    for j in range(CONV_WIDTH):
        acc = acc + w_ref[j:j + 1, :] * ext_ref[lead + j:lead + j + tt, :]
    y = acc + cb_ref[...]
    mu = jnp.mean(y, axis=-1, keepdims=True)
    d = y - mu
    var = jnp.mean(d * d, axis=-1, keepdims=True)
    y = d * lax.rsqrt(var + LN_EPS) * lg_ref[...] + lb_ref[...]
    y_ref[0] = (y * _sigmoid(y)).astype(y_ref.dtype)
    nb_ref[0] = ext_ref[tt + lead:tt + CONV_HALO, :]
    tail = ext_ref[tt:tt + CONV_HALO, :]
    ext_ref[0:CONV_HALO, :] = tail


def _conv_mixer(p, buf, w, cb, lg, lb, tt):
    b, t, _ = p.shape
    keep = CONV_WIDTH - 1
    return pl.pallas_call(
        functools.partial(_conv_body, tt=tt),
        grid=(b, t // tt),
        in_specs=[pl.BlockSpec((1, tt, CONV_COLS), lambda i, j: (i, j, 0)),
                  pl.BlockSpec((1, keep, CONV_CH), lambda i, j: (i, 0, 0)),
                  _const_spec((CONV_WIDTH, CONV_CH)), _const_spec((1, CONV_CH)),
                  _const_spec((1, CONV_CH)), _const_spec((1, CONV_CH))],
        out_specs=[pl.BlockSpec((1, tt, CONV_CH), lambda i, j: (i, j, 0)),
                   pl.BlockSpec((1, keep, CONV_CH), lambda i, j: (i, 0, 0))],
        out_shape=[jax.ShapeDtypeStruct((b, t, CONV_CH), BF16),
                   jax.ShapeDtypeStruct((b, keep, CONV_CH), F32)],
        scratch_shapes=[pltpu.VMEM((tt + CONV_HALO, CONV_CH), F32)],
        compiler_params=_cparams(("arbitrary", "arbitrary")),
        name="conv_mixer",
    )(p, buf, w, cb.reshape(1, -1), lg.reshape(1, -1), lb.reshape(1, -1))


def _rwkv_pre_body(p_ref, prev_ref, mu_ref, w0_ref, wup_ref, a0_ref, aup_ref, gup_ref, kk_ref, ka_ref, rk_ref,
                   *outs, tiles_per_seq):
    heads_first = tiles_per_seq > 0
    p = p_ref[...]
    if heads_first:
        *outs, carry_ref = outs

        @pl.when(pl.program_id(0) % tiles_per_seq == 0)
        def _():
            carry_ref[...] = prev_ref[0]

        first_row = lax.broadcasted_iota(jnp.int32, p.shape, 0) == 0
        prev = jnp.where(first_row, carry_ref[...], pltpu.roll(p, 1, 0))
        carry_ref[...] = p[p.shape[0] - 1:p.shape[0], :]
    else:
        prev = prev_ref[...]
    pm = p + (prev - p) * mu_ref[...]
    d = RWKV_DIM
    r = pm[:, 0:d]
    k = pm[:, d:2 * d]
    v = pm[:, 2 * d:3 * d]
    xw = pm[:, 3 * d:3 * d + DECAY_LORA]
    xa = pm[:, 3 * d + DECAY_LORA:3 * d + DECAY_LORA + AAA_LORA]
    xg = pm[:, 3 * d + DECAY_LORA + AAA_LORA:RWKV_COLS]
    bd = _head_blockdiag(d)
    w_log = -_softplus(-(w0_ref[...] + _dot3(jnp.tanh(xw), wup_ref[...]))) - 0.5
    log_decay = -jnp.exp(w_log)
    a = _sigmoid(a0_ref[...] + _dot3(xa, aup_ref[...]))
    g = _dot3(_sigmoid(xg), gup_ref[...])
    kk = k * kk_ref[...]
    k = k * (1.0 + (a - 1.0) * ka_ref[...])
    kk = kk / jnp.maximum(jnp.sqrt(_dot_exact_rhs(kk * kk, bd)), 1e-12)
    steps = (r, log_decay, k, v, kk, kk * a)
    outs[-2][...] = g
    outs[-1][...] = _dot_exact_rhs(r * k * rk_ref[...], bd) * v
    if not heads_first:
        for ref, val in zip(outs[:6], steps):
            ref[...] = val
        return
    head = lambda h: slice(h * HEAD_DIM, (h + 1) * HEAD_DIM)
    for ref, val in zip(outs[:6], steps):
        for h in range(RWKV_HEADS):
            ref[h] = val[:, head(h)]
    for ref, val in zip(outs[6:9], (log_decay, kk, v)):
        val_t = val.T
        for h in range(RWKV_HEADS):
            ref[h] = val_t[head(h), :]


def _rwkv_pre(p, prev, mu, w0, w_up, a0, a_up, g_up, k_k, k_a, r_k, tm, seq_len=None):
    n = p.shape[0]
    vec = lambda a: a.reshape(1, -1)
    d = RWKV_DIM
    if seq_len is None:
        per_seq = 0
        prev_spec = _row_spec(tm, RWKV_COLS)
        scratch = []
        out_specs = [_row_spec(tm, d)] * 8
        out_shape = [jax.ShapeDtypeStruct((n, d), F32)] * 8
    else:
        per_seq = seq_len // tm
        prev = prev.reshape(-1, 1, RWKV_COLS)
        prev_spec = pl.BlockSpec((1, 1, RWKV_COLS), lambda i: (i // per_seq, 0, 0))
        scratch = [pltpu.VMEM((1, RWKV_COLS), F32)]
        chains = n // seq_len * RWKV_HEADS
        slab = pl.BlockSpec((RWKV_HEADS, tm, HEAD_DIM), lambda i: (i // per_seq, i % per_seq, 0))
        slab_t = pl.BlockSpec((RWKV_HEADS, HEAD_DIM, tm), lambda i: (i // per_seq, 0, i % per_seq))
        out_specs = [slab] * 6 + [slab_t] * 3 + [_row_spec(tm, d)] * 2
        out_shape = ([jax.ShapeDtypeStruct((chains, seq_len, HEAD_DIM), F32)] * 6
                     + [jax.ShapeDtypeStruct((chains, HEAD_DIM, seq_len), F32)] * 3
                     + [jax.ShapeDtypeStruct((n, d), F32)] * 2)
    return pl.pallas_call(
        functools.partial(_rwkv_pre_body, tiles_per_seq=per_seq),
        grid=(n // tm,),
        in_specs=[_row_spec(tm, RWKV_COLS), prev_spec, _const_spec((1, RWKV_COLS)),
                  _const_spec((1, d)), _const_spec((DECAY_LORA, d)), _const_spec((1, d)),
                  _const_spec((AAA_LORA, d)), _const_spec((GATE_LORA, d)),
                  _const_spec((1, d)), _const_spec((1, d)), _const_spec((1, d))],
        out_specs=out_specs,
        out_shape=out_shape,
        scratch_shapes=scratch,
        compiler_params=_cparams(("arbitrary",)),
        name="rwkv_pre",
    )(p, prev, vec(mu), vec(w0), w_up, vec(a0), a_up, g_up, vec(k_k), vec(k_a), vec(r_k))


def _wkv_body(r_ref, w_ref, k_ref, v_ref, kk_ref, b_ref, s0_ref, y_ref, sT_ref, s_scr, *, nc, tc):
    ti = pl.program_id(1)

    @pl.when(ti == 0)
    def _():
        s_scr[...] = s0_ref[...]

    eye = (lax.broadcasted_iota(jnp.int32, (HEAD_DIM, HEAD_DIM), 0)
           == lax.broadcasted_iota(jnp.int32, (HEAD_DIM, HEAD_DIM), 1))

    def step(t, carry):
        for c in range(nc):
            row = lambda ref: ref[c, pl.ds(t, 1), :]
            s = s_scr[c]
            sa = jnp.sum(s * row(kk_ref), axis=1, keepdims=True)
            vcol = jnp.sum(jnp.where(eye, row(v_ref), 0.0), axis=1, keepdims=True)
            s = s * jnp.exp(row(w_ref)) - sa * row(b_ref) + vcol * row(k_ref)
            s_scr[c] = s
            ycol = jnp.sum(s * row(r_ref), axis=1, keepdims=True)
            y_ref[c, pl.ds(t, 1), :] = jnp.sum(jnp.where(eye, ycol, 0.0), axis=0, keepdims=True)
        return carry

    lax.fori_loop(0, tc, step, 0)

    @pl.when(ti == pl.num_programs(1) - 1)
    def _():
        sT_ref[...] = s_scr[...]


def _wkv(r, w, k, v, kk, b, s0, nc, tc):
    n, t, _ = r.shape
    seq = pl.BlockSpec((nc, tc, HEAD_DIM), lambda i, j: (i, j, 0))
    st = pl.BlockSpec((nc, HEAD_DIM, HEAD_DIM), lambda i, j: (i, 0, 0))
    return pl.pallas_call(
        functools.partial(_wkv_body, nc=nc, tc=tc),
        grid=(n // nc, t // tc),
        in_specs=[seq] * 6 + [st],
        out_specs=[seq, st],
        out_shape=[jax.ShapeDtypeStruct((n, t, HEAD_DIM), F32),
                   jax.ShapeDtypeStruct((n, HEAD_DIM, HEAD_DIM), F32)],
        scratch_shapes=[pltpu.VMEM((nc, HEAD_DIM, HEAD_DIM), F32)],
        compiler_params=_cparams(("arbitrary", "arbitrary")),
        name="wkv",
    )(r, w, k, v, kk, b, s0)


def _mm(a, b):
    return _dot3(a, b)


def _mm_nt(a, b):
    ah, al = _split2(a)
    bh, bl = _split2(b)
    return _dot_nt(ah, bh) + (_dot_nt(ah, bl) + _dot_nt(al, bh))


def _split3(x):
    p1 = x.astype(BF16)
    r1 = x - p1.astype(F32)
    p2 = r1.astype(BF16)
    p3 = (r1 - p2.astype(F32)).astype(BF16)
    return p1, p2, p3


def _wkv_chunk_body(r_ref, lw_ref, k_ref, v_ref, kk_ref, b_ref, lwt_ref, kkt_ref, vt_ref, s0_ref,
                    y_ref, sT_ref, s_scr, *, nc):
    ti = pl.program_id(1)
    c_len = WKV_CHUNK

    @pl.when(ti == 0)
    def _():
        s_scr[...] = s0_ref[...]

    row = lax.broadcasted_iota(jnp.int32, (c_len, c_len), 0)
    col = lax.broadcasted_iota(jnp.int32, (c_len, c_len), 1)
    strict = row > col
    incl = row >= col
    lower = incl.astype(BF16)
    upper = (row <= col).astype(BF16)
    eye = (row == col).astype(F32)

    n_sub = WKV_BLOCK // c_len
    units = [(j, c) for j in range(n_sub) for c in range(nc)]
    each = lambda f, *lists: [f(*args) for args in zip(*lists)]
    ts = lambda j: slice(j * c_len, (j + 1) * c_len)
    rows = lambda ref: [ref[c, ts(j), :] for j, c in units]
    cols = lambda ref: [ref[c, :, ts(j)] for j, c in units]
    r, lw, k, v, kk, b = (rows(ref) for ref in (r_ref, lw_ref, k_ref, v_ref, kk_ref, b_ref))
    lwt, kkt, vt = (cols(ref) for ref in (lwt_ref, kkt_ref, vt_ref))
    cum = each(lambda x: sum(_dot(lower, piece) for piece in _split3(x)), lw)
    cumt = each(lambda x: sum(_dot(piece, upper) for piece in _split3(x)), lwt)
    w_incl = each(jnp.exp, cum)
    w_last = each(lambda w: w[c_len - 1:c_len, :], w_incl)
    inv = each(lambda x: jnp.exp(-x), cum)
    qt = each(lambda x, cs, l: x * jnp.exp(cs - l), kk, cum, lw)
    qtt = each(lambda x, cs, l: x * jnp.exp(cs - l), kkt, cumt, lwt)
    rt = each(jnp.multiply, r, w_incl)
    kt = each(jnp.multiply, k, inv)
    bt = each(jnp.multiply, b, inv)
    a_qb = each(lambda x, y: jnp.where(strict, _mm_nt(x, y), 0.0), qt, bt)
    a_qk = each(lambda x, y: jnp.where(strict, _mm_nt(x, y), 0.0), qt, kt)
    a_rk = each(lambda x, y: jnp.where(incl, _mm_nt(x, y), 0.0), rt, kt)
    a_rb = each(lambda x, y: jnp.where(incl, _mm_nt(x, y), 0.0), rt, bt)
    tinv = each(lambda n: eye - n, a_qb)
    power = a_qb
    for _ in range(WKV_CHUNK_LOG2 - 1):
        power = each(_mm, power, power)
        tinv = each(lambda t, p: t + _mm(t, p), tinv, power)
    tq = each(_mm, tinv, qt)
    av = each(_mm, a_qk, v)
    av_t = each(_mm_nt, vt, a_qk)
    tq_t = each(_mm_nt, qtt, tinv)
    tav = each(_mm, tinv, av)
    tav_t = each(_mm_nt, av_t, tinv)
    ark_v = each(_mm, a_rk, v)
    rq = each(lambda x, a, y: x - _mm(a, y), rt, a_rb, tq)
    yv = each(lambda x, a, y: x - _mm(a, y), ark_v, a_rb, tav)
    gain = each(lambda x, y, w: _mm(x, y * w), vt, kt, w_last)
    bw = each(jnp.multiply, bt, w_last)
    for j in range(n_sub):
        ids = [j * nc + c for c in range(nc)]
        s = [s_scr[c] for c in range(nc)]
        sa_t = [_mm(s[c], tq_t[i]) + tav_t[i] for c, i in enumerate(ids)]
        ys = [_mm_nt(rq[i], s[c]) + yv[i] for c, i in enumerate(ids)]
        new = [s[c] * w_last[i] + gain[i] - _mm(sa_t[c], bw[i]) for c, i in enumerate(ids)]
        for c, i in enumerate(ids):
            y_ref[c, ts(j), :] = ys[c]
            s_scr[c] = new[c]

    @pl.when(ti == pl.num_programs(1) - 1)
    def _():
        sT_ref[...] = s_scr[...]


def _wkv_chunked(r, lw, k, v, kk, b, lwt, kkt, vt, s0, nc):
    n, t, _ = r.shape
    seq = pl.BlockSpec((nc, WKV_BLOCK, HEAD_DIM), lambda i, j: (i, j, 0))
    seq_t = pl.BlockSpec((nc, HEAD_DIM, WKV_BLOCK), lambda i, j: (i, 0, j))
    st = pl.BlockSpec((nc, HEAD_DIM, HEAD_DIM), lambda i, j: (i, 0, 0))
    return pl.pallas_call(
        functools.partial(_wkv_chunk_body, nc=nc),
        grid=(n // nc, t // WKV_BLOCK),
        in_specs=[seq] * 6 + [seq_t] * 3 + [st],
        out_specs=[seq, st],
        out_shape=[jax.ShapeDtypeStruct((n, t, HEAD_DIM), F32),
                   jax.ShapeDtypeStruct((n, HEAD_DIM, HEAD_DIM), F32)],
        scratch_shapes=[pltpu.VMEM((nc, HEAD_DIM, HEAD_DIM), F32)],
        compiler_params=_cparams(("arbitrary", "arbitrary")),
        name="wkv_chunked",
    )(r, lw, k, v, kk, b, lwt, kkt, vt, s0)


def _sb_block(q, kb, vb, bias, rest, umat, mask):
    z = _dot_nt(q, kb) + bias
    sp = _softplus(z)
    counted = sp if mask is None else jnp.where(mask, sp, 0.0)
    after = _dot(counted.astype(BF16), umat)
    a = jnp.exp((z - sp) + after + rest)
    if mask is not None:
        a = jnp.where(mask, a, 0.0)
    out = _dot(a.astype(BF16), vb)
    return out, after[:, 0:1] - counted[:, 0:1]


def _attn_prompt_body(q_ref, k_ref, v_ref, bias_ref, o_ref, *, tile):
    qi = pl.program_id(1)
    heads = range(ATT_HEADS)
    hs = lambda a, h: a[:, h * HEAD_DIM:(h + 1) * HEAD_DIM]
    q_all = q_ref[0]
    qs = [hs(q_all, h) for h in heads]
    biases = [bias_ref[h] for h in heads]
    umat = _suffix_matrix(tile)
    causal = (lax.broadcasted_iota(jnp.int32, (tile, tile), 1)
              < lax.broadcasted_iota(jnp.int32, (tile, tile), 0))

    def visit(j, accs, rests, mask):
        start = pl.multiple_of(j * tile, tile)
        kb = k_ref[0, pl.ds(start, tile), :]
        vb = v_ref[0, pl.ds(start, tile), :]
        zs = [_dot_nt(qs[h], hs(kb, h)) + biases[h] for h in heads]
        sps = [_softplus(z) for z in zs]
        counted = sps if mask is None else [jnp.where(mask, sp, 0.0) for sp in sps]
        afters = [_dot(sp.astype(BF16), umat) for sp in counted]
        probs = [jnp.exp((zs[h] - sps[h]) + afters[h] + rests[h]) for h in heads]
        if mask is not None:
            probs = [jnp.where(mask, a, 0.0) for a in probs]
        outs = [_dot(probs[h].astype(BF16), hs(vb, h)) for h in heads]
        new_accs = tuple(accs[h] + outs[h] for h in heads)
        new_rests = tuple(rests[h] + afters[h][:, 0:1] - counted[h][:, 0:1] for h in heads)
        return new_accs, new_rests

    accs = tuple(jnp.zeros((tile, HEAD_DIM), F32) for _ in heads)
    rests = tuple(jnp.zeros((tile, 1), F32) for _ in heads)
    accs, rests = visit(qi, accs, rests, causal)
    accs, rests = lax.fori_loop(0, qi, lambda jj, c: visit(qi - 1 - jj, c[0], c[1], None), (accs, rests))
    r = lax.broadcasted_iota(jnp.int32, (HEAD_DIM, ATT_DIM), 0)
    c = lax.broadcasted_iota(jnp.int32, (HEAD_DIM, ATT_DIM), 1)
    out = jnp.zeros((tile, ATT_DIM), F32)
    for h in heads:
        out = out + _dot(accs[h].astype(BF16), (c == r + h * HEAD_DIM).astype(BF16))
    o_ref[0] = out.astype(o_ref.dtype)


def _attn_prompt(qb, kb, vb, bias):
    b, t, d = qb.shape
    tile = ATT_TILE
    bias_b = jnp.broadcast_to(bias.astype(F32)[:, None, None], (ATT_HEADS, 1, tile))
    qspec = pl.BlockSpec((1, tile, d), lambda i, l: (i, l, 0))
    kvspec = pl.BlockSpec((1, t, d), lambda i, l: (i, 0, 0))
    return pl.pallas_call(
        functools.partial(_attn_prompt_body, tile=tile),
        grid=(b, t // tile),
        in_specs=[qspec, kvspec, kvspec, _const_spec((ATT_HEADS, 1, tile))],
        out_specs=qspec,
        out_shape=jax.ShapeDtypeStruct((b, t, d), BF16),
        compiler_params=_cparams(("arbitrary", "arbitrary")),
        name="attn_prompt",
    )(qb, kb, vb, bias_b)


def _attn_decode_body(pt_ref, q_ref, bias_ref, kn_ref, vn_ref, *refs, npg, nq):
    k_refs = refs[:npg]
    v_refs = refs[npg:2 * npg]
    o_ref, rest_ref, acc_ref = refs[2 * npg:]
    jb = pl.program_id(1)
    rows = ATT_HEADS * Q_PAD
    pages = range(npg)
    spread = lambda tot: jnp.broadcast_to(tot, (rows, PAGE_SIZE))
    q = q_ref[0]
    bias = bias_ref[...]
    umat = _suffix_matrix(PAGE_SIZE)

    @pl.when(jb == 0)
    def _():
        kidx = lax.broadcasted_iota(jnp.int32, (rows, PAGE_SIZE), 1)
        qidx = lax.broadcasted_iota(jnp.int32, (rows, PAGE_SIZE), 0) % Q_PAD
        out, tot = _sb_block(q, kn_ref[0].astype(BF16), vn_ref[0].astype(BF16), bias,
                             jnp.zeros((rows, 1), F32), umat, (kidx < qidx) & (kidx < nq))
        acc_ref[...] = out
        rest_ref[...] = spread(tot)

    zs = [_dot_nt(q, k_refs[s][0, 0].astype(BF16)) + bias for s in pages]
    sps = [_softplus(z) for z in zs]
    afters = [_dot(sp.astype(BF16), umat) for sp in sps]
    rest = rest_ref[...]
    probs = []
    for s in pages:
        probs.append(jnp.exp((zs[s] - sps[s]) + afters[s] + rest).astype(BF16))
        rest = rest + spread(afters[s][:, 0:1] - sps[s][:, 0:1])
    rest_ref[...] = rest
    out = _dot(probs[0], v_refs[0][0, 0].astype(BF16))
    for s in pages[1:]:
        out = out + _dot(probs[s], v_refs[s][0, 0].astype(BF16))
    acc_ref[...] += out

    @pl.when(jb == pl.num_programs(1) - 1)
    def _():
        head_of_row = lax.broadcasted_iota(jnp.int32, (rows, ATT_DIM), 0) // Q_PAD
        head_of_col = lax.broadcasted_iota(jnp.int32, (rows, ATT_DIM), 1) // HEAD_DIM
        kept = jnp.where(head_of_row == head_of_col, acc_ref[...], 0.0)
        out = kept[0:Q_PAD]
        for h in range(1, ATT_HEADS):
            out = out + kept[h * Q_PAD:(h + 1) * Q_PAD]
        o_ref[0] = out.astype(o_ref.dtype)


def _attn_decode(li, qb, k_new, v_new, cache_k, cache_v, page_table, bias):
    b, t, _ = qb.shape
    n_pages = page_table.shape[1]
    npg = PAGES_PER_STEP
    rows = ATT_HEADS * Q_PAD
    qpad = jnp.pad(qb, ((0, 0), (0, Q_PAD - t), (0, 0)))
    head_of_col = jnp.arange(ATT_DIM) // HEAD_DIM
    qbd = jnp.where(head_of_col[None, None, None, :] == jnp.arange(ATT_HEADS)[None, :, None, None],
                    qpad[:, None], jnp.zeros((), BF16)).reshape(b, rows, ATT_DIM)
    bias_b = jnp.broadcast_to(jnp.repeat(bias.astype(F32), Q_PAD)[:, None], (rows, PAGE_SIZE))
    kn = jnp.pad(k_new, ((0, 0), (0, PAGE_SIZE - t), (0, 0)))
    vn = jnp.pad(v_new, ((0, 0), (0, PAGE_SIZE - t), (0, 0)))

    def page_spec(s):
        return pl.BlockSpec((1, 1, PAGE_SIZE, ATT_DIM),
                            lambda i, j, pt: (li, pt[i * n_pages + n_pages - 1 - (j * npg + s)], 0, 0))

    per_b = lambda r, c: pl.BlockSpec((1, r, c), lambda i, j, pt: (i, 0, 0))
    grid_spec = pltpu.PrefetchScalarGridSpec(
        num_scalar_prefetch=1,
        grid=(b, n_pages // npg),
        in_specs=[per_b(rows, ATT_DIM), pl.BlockSpec((rows, PAGE_SIZE), lambda i, j, pt: (0, 0)),
                  per_b(PAGE_SIZE, ATT_DIM), per_b(PAGE_SIZE, ATT_DIM)]
                 + [page_spec(s) for s in range(npg)] * 2,
        out_specs=per_b(Q_PAD, ATT_DIM),
        scratch_shapes=[pltpu.VMEM((rows, PAGE_SIZE), F32), pltpu.VMEM((rows, ATT_DIM), F32)],
    )
    out = pl.pallas_call(
        functools.partial(_attn_decode_body, npg=npg, nq=t),
        grid_spec=grid_spec,
        out_shape=jax.ShapeDtypeStruct((b, Q_PAD, ATT_DIM), BF16),
        compiler_params=_cparams(("arbitrary", "arbitrary")),
        name="attn_decode",
    )(page_table.reshape(-1), qbd, bias_b, kn, vn, *([cache_k] * npg), *([cache_v] * npg))
    return out[:, :t]


def _outproj_body(x_ref, yc_ref, yw_ref, bonus_ref, g_ref, ya_ref, lng_ref, lnb_ref, w_ref, o_ref):
    bd = _head_blockdiag(RWKV_DIM)
    if len(yw_ref.shape) == 3:
        r = lax.broadcasted_iota(jnp.int32, (HEAD_DIM, RWKV_DIM), 0)
        c = lax.broadcasted_iota(jnp.int32, (HEAD_DIM, RWKV_DIM), 1)
        y = sum(_dot_exact_rhs(yw_ref[h], (c == r + h * HEAD_DIM).astype(BF16)) for h in range(RWKV_HEADS))
    else:
        y = yw_ref[...]
    inv = 1.0 / HEAD_DIM
    d = y - _dot_exact_rhs(y, bd) * inv
    var = _dot_exact_rhs(d * d, bd) * inv
    yn = d * lax.rsqrt(var + GN_EPS) * lng_ref[...] + lnb_ref[...]
    yr = ((yn + bonus_ref[...]) * g_ref[...]).astype(BF16)
    c0, c1 = CONV_CH, CONV_CH + RWKV_DIM
    o_ref[...] = (x_ref[...] + _dot(yc_ref[...], w_ref[0:c0, :]) + _dot(yr, w_ref[c0:c1, :])
                  + _dot(ya_ref[...], w_ref[c1:D_MODEL, :]))


def _outproj(x, yc, yw, bonus, g, ya, ln_g, ln_b, w_b, tm):
    n = x.shape[0]
    if yw.ndim == 3:
        per_seq = yw.shape[1] // tm
        yw_spec = pl.BlockSpec((RWKV_HEADS, tm, HEAD_DIM), lambda i: (i // per_seq, i % per_seq, 0))
    else:
        yw_spec = _row_spec(tm, RWKV_DIM)
    return pl.pallas_call(
        _outproj_body,
        grid=(n // tm,),
        in_specs=[_row_spec(tm, D_MODEL), _row_spec(tm, CONV_CH), yw_spec, _row_spec(tm, RWKV_DIM),
                  _row_spec(tm, RWKV_DIM), _row_spec(tm, ATT_DIM), _const_spec((1, RWKV_DIM)),
                  _const_spec((1, RWKV_DIM)), _const_spec((D_MODEL, D_MODEL))],
        out_specs=_row_spec(tm, D_MODEL),
        out_shape=jax.ShapeDtypeStruct((n, D_MODEL), F32),
        compiler_params=_cparams(("arbitrary",)),
        name="outproj",
    )(x, yc, yw, bonus, g, ya, ln_g.reshape(1, -1), ln_b.reshape(1, -1), w_b)


SUBLANES = 8
KEY_GROUPS = N_KEYS // SUBLANES


def _oddeven_merge(lo, hi, r):
    step = r * 2
    if step < hi - lo:
        yield from _oddeven_merge(lo, hi, step)
        yield from _oddeven_merge(lo + r, hi, step)
        yield from [(i, i + r) for i in range(lo + r, hi - r, step)]
    else:
        yield (lo, lo + r)


def _oddeven_sort(lo, hi):
    if hi - lo >= 1:
        mid = lo + (hi - lo) // 2
        yield from _oddeven_sort(lo, mid)
        yield from _oddeven_sort(mid + 1, hi)
        yield from _oddeven_merge(lo, hi, 1)


SORT_NET = tuple(_oddeven_sort(0, PEER_TOPK - 1))
BITONIC_NET = tuple((i, i + s) for s in (8, 4, 2, 1) for i in range(PEER_TOPK) if not i & s)
SUM_SLOTS = tuple((a, b) for a in range(PEER_TOPK) for b in range(PEER_TOPK) if (a + 1) * (b + 1) <= PEER_TOPK)


def _exchange(vals, i, j):
    vals[i], vals[j] = jnp.maximum(vals[i], vals[j]), jnp.minimum(vals[i], vals[j])


def _sublane_max(x):
    for shift in (4, 2, 1):
        x = jnp.maximum(x, pltpu.roll(x, shift, 0))
    return x


def _top16_sorted(groups):
    vals = list(groups)
    for i, j in SORT_NET:
        _exchange(vals, i, j)
    for shift in (4, 2, 1):
        moved = [pltpu.roll(v, shift, 0) for v in vals]
        vals = [jnp.maximum(vals[i], moved[PEER_TOPK - 1 - i]) for i in range(PEER_TOPK)]
        for i, j in BITONIC_NET:
            _exchange(vals, i, j)
    return vals


def _peer_route_body(x_ref, g_ref, wq_ref, sk_ref, h_out, rank_out, e2_out, cnt_out, f1_out):
    tm = x_ref.shape[0]
    kk = PEER_TOPK
    hb = _rmsnorm(x_ref[...], g_ref[...]).astype(BF16)
    h_out[...] = hb
    sub = lax.broadcasted_iota(jnp.int32, (SUBLANES, tm), 0)
    split = lambda s: [s[i * SUBLANES:(i + 1) * SUBLANES, :] for i in range(KEY_GROUPS)]
    join = lambda parts: jnp.concatenate(parts, axis=0)
    neg_inf = jnp.full((SUBLANES, tm), -jnp.inf, F32)
    half = PEER_DK // 2
    for h in range(PEER_HEADS):
        q = _dot(hb, wq_ref[:, h * PEER_DK:(h + 1) * PEER_DK])
        s1 = split(_dot_nt(sk_ref[2 * h], q[:, :half].astype(BF16)))
        s2 = split(_dot_nt(sk_ref[2 * h + 1], q[:, half:].astype(BF16)))
        v1 = _top16_sorted(s1)
        v2 = _top16_sorted(s2)
        sums = {ab: v1[ab[0]] + v2[ab[1]] for ab in SUM_SLOTS}
        packs = []
        for j in range(0, len(SUM_SLOTS), SUBLANES):
            pack = neg_inf
            for s, ab in enumerate(SUM_SLOTS[j:j + SUBLANES]):
                pack = jnp.where(sub == s, sums[ab], pack)
            packs.append(pack)
        best = []
        for _ in range(kk):
            m = packs[0]
            for p in packs[1:]:
                m = jnp.maximum(m, p)
            m = _sublane_max(m)
            best.append(m)
            packs = [jnp.where(p == m, -jnp.inf, p) for p in packs]
        z = jnp.ones((SUBLANES, tm), F32)
        for i in range(1, kk):
            z = z + jnp.exp(best[i] - best[0])
        half_inv_z = 0.5 / z
        cnt = [jnp.zeros((SUBLANES, tm), F32) for _ in range(kk)]
        for a, b in SUM_SLOTS:
            cnt[a] = cnt[a] + jnp.where(sums[(a, b)] >= best[kk - 1], 1.0, 0.0)
        cnt_full, f1_full, rank_full, e2_full = [], [], [], []
        for g in range(KEY_GROUPS):
            c = jnp.zeros((SUBLANES, tm), F32)
            for a in range(kk):
                c = jnp.where(s1[g] == v1[a], cnt[a], c)
            cnt_full.append(c)
            f1_full.append(jnp.where(s1[g] >= v1[kk - 1], jnp.exp(s1[g] - v1[0]), 0.0) * half_inv_z)
            rk = jnp.full((SUBLANES, tm), float(kk), F32)
            for b in reversed(range(kk)):
                rk = jnp.where(s2[g] == v2[b], float(b), rk)
            rank_full.append(rk)
            e2_full.append(jnp.where(s2[g] >= v2[kk - 1], jnp.exp(s2[g] - v2[0]), 0.0))
        cnt_out[h] = join(cnt_full)
        f1_out[h] = join(f1_full)
        rank_out[h] = join(rank_full).astype(BF16)
        e2_out[h] = join(e2_full).astype(BF16)


def _peer_route(x, g, wq_b, sk_b, tm):
    n = x.shape[0]
    keyed = pl.BlockSpec((PEER_HEADS, N_KEYS, tm), lambda i: (0, 0, i))
    keyed_shape = lambda dt: jax.ShapeDtypeStruct((PEER_HEADS, N_KEYS, n), dt)
    return pl.pallas_call(
        _peer_route_body,
        grid=(n // tm,),
        in_specs=[_row_spec(tm, D_MODEL), _const_spec((1, D_MODEL)), _const_spec((D_MODEL, PEER_HEADS * PEER_DK)),
                  _const_spec((2 * PEER_HEADS, N_KEYS, PEER_DK // 2))],
        out_specs=[_row_spec(tm, D_MODEL), keyed, keyed, keyed, keyed],
        out_shape=[jax.ShapeDtypeStruct((n, D_MODEL), BF16), keyed_shape(BF16), keyed_shape(BF16),
                   keyed_shape(F32), keyed_shape(F32)],
        compiler_params=_cparams(("arbitrary",)),
        name="peer_route",
    )(x, g.reshape(1, -1), wq_b, sk_b)


GATE_ROWS = 16


GELU_A = 0.7978845608028654
GELU_B = GELU_A * 0.044715


def _gelu_tanh_doubled(x):
    return x + x * jnp.tanh(x * (GELU_A + GELU_B * (x * x)))


def _peer_expert_body(x_ref, h_ref, rank_ref, e2_ref, cnt_ref, f1_ref, u_ref, vt_ref, o_ref, acc_ref):
    ec = pl.program_id(1)

    @pl.when(ec == 0)
    def _():
        acc_ref[...] = jnp.zeros(acc_ref.shape, F32)

    tm = h_ref.shape[0]
    n_grp = 2 if tm % 512 == 0 else 1
    grp = tm // n_grp
    toks = [slice(g * grp, (g + 1) * grp) for g in range(n_grp)]
    hids = [_dot_nt(u_ref[...], h_ref[tk, :]) for tk in toks]
    heads = range(PEER_HEADS)
    for g, tk in enumerate(toks):
        weights = []
        for r in range(PEER_ROWS):
            bcast = lambda ref, h: jnp.broadcast_to(ref[h, 0, r:r + 1, tk], (GATE_ROWS, grp)).astype(BF16)
            cnts = [bcast(cnt_ref, h) for h in heads]
            f1s = [bcast(f1_ref, h) for h in heads]
            for j in range(N_KEYS // GATE_ROWS):
                keys = slice(j * GATE_ROWS, (j + 1) * GATE_ROWS)
                gate = jnp.zeros((GATE_ROWS, grp), BF16)
                for h in heads:
                    e2 = e2_ref[h, keys, tk]
                    gate = gate + jnp.where(rank_ref[h, keys, tk] < cnts[h], e2, jnp.zeros_like(e2)) * f1s[h]
                rows = slice(r * N_KEYS + j * GATE_ROWS, r * N_KEYS + (j + 1) * GATE_ROWS)
                weights.append(gate * _gelu_tanh_doubled(hids[g][rows]).astype(BF16))
        acc_ref[:, tk] += _dot(vt_ref[...], jnp.concatenate(weights, axis=0))

    @pl.when(ec == pl.num_programs(1) - 1)
    def _():
        o_ref[...] = x_ref[...] + acc_ref[...].T


def _peer_expert(x, hb, rank2, e2, cnt, f1, u_b, vt_b, tm):
    n = x.shape[0]
    te = PEER_ROWS * N_KEYS
    ne = u_b.shape[0] // te
    chunked = lambda a: a.reshape(PEER_HEADS, ne, PEER_ROWS, n)
    tok = lambda c: pl.BlockSpec((tm, c), lambda i, e: (i, 0))
    keyed = pl.BlockSpec((PEER_HEADS, N_KEYS, tm), lambda i, e: (0, 0, i))
    row = pl.BlockSpec((PEER_HEADS, 1, PEER_ROWS, tm), lambda i, e: (0, e, 0, i))
    return pl.pallas_call(
        _peer_expert_body,
        grid=(n // tm, ne),
        in_specs=[tok(D_MODEL), tok(D_MODEL), keyed, keyed, row, row,
                  pl.BlockSpec((te, D_MODEL), lambda i, e: (e, 0)),
                  pl.BlockSpec((D_MODEL, te), lambda i, e: (0, e))],
        out_specs=tok(D_MODEL),
        out_shape=jax.ShapeDtypeStruct((n, D_MODEL), F32),
        scratch_shapes=[pltpu.VMEM((D_MODEL, tm), F32)],
        compiler_params=_cparams(("arbitrary", "arbitrary")),
        name="peer_expert",
    )(x, hb, rank2, e2, chunked(cnt), chunked(f1), u_b, vt_b)


def _final_norm_body(x_ref, g_ref, o_ref):
    o_ref[...] = _rmsnorm(x_ref[...], g_ref[...])


def _final_norm(x, g, tm):
    n = x.shape[0]
    return pl.pallas_call(
        _final_norm_body,
        grid=(n // tm,),
        in_specs=[_row_spec(tm, D_MODEL), _const_spec((1, D_MODEL))],
        out_specs=_row_spec(tm, D_MODEL),
        out_shape=jax.ShapeDtypeStruct((n, D_MODEL), F32),
        compiler_params=_cparams(("arbitrary",)),
        name="final_norm",
    )(x, g.reshape(1, -1))


def _heads_first(a, b, t):
    h = a.shape[1] // HEAD_DIM
    return a.reshape(b, t, h, HEAD_DIM).transpose(0, 2, 1, 3).reshape(b * h, t, HEAD_DIM)


def _tokens_first(a, b, t):
    h = a.shape[0] // b
    return a.reshape(b, h, t, HEAD_DIM).transpose(0, 2, 1, 3).reshape(b * t, h * HEAD_DIM)


def _layer(x, b, t, conv_buf, shift_prev, wkv0, attend, lw, tm, conv_tile, wkv_chains, wkv_tile):
    p_conv, p_rwkv, k, v, qb, kb, vb = _inproj(x, lw["norm1_g"], lw["w_in"], tm)

    y_conv, conv_new = _conv_mixer(p_conv.reshape(b, t, CONV_COLS), conv_buf, lw["conv_w"], lw["conv_b"],
                                   lw["conv_ln_g"], lw["conv_ln_b"], conv_tile)

    p3 = p_rwkv.reshape(b, t, RWKV_COLS)
    pre = lambda prev, **kw: _rwkv_pre(p_rwkv, prev, lw["rwkv_mu"], lw["rwkv_w0"], lw["rwkv_w_up"], lw["rwkv_a0"],
                                       lw["rwkv_a_up"], lw["rwkv_g_up"], lw["rwkv_k_k"], lw["rwkv_k_a"],
                                       lw["rwkv_r_k"], tm, **kw)
    s0 = wkv0.reshape(b * RWKV_HEADS, HEAD_DIM, HEAD_DIM)
    if t % WKV_BLOCK == 0 and t % tm == 0:
        *steps, g, bonus = pre(shift_prev, seq_len=t)
        y_wkv, wkv_new = _wkv_chunked(*steps, s0, wkv_chains)
    else:
        prev = jnp.concatenate([shift_prev[:, None, :], p3[:, :-1]], axis=1).reshape(b * t, RWKV_COLS)
        *steps, g, bonus = pre(prev)
        y_wkv, wkv_new = _wkv(*(_heads_first(a, b, t) for a in steps), s0, wkv_chains, wkv_tile)
        y_wkv = _tokens_first(y_wkv, b, t)

    y_att = attend(qb, k, v, kb, vb)

    x = _outproj(x, y_conv.reshape(b * t, CONV_CH), y_wkv, bonus, g, y_att, lw["rwkv_ln_g"], lw["rwkv_ln_b"],
                 lw["w_out"], tm)
    routed = _peer_route(x, lw["norm2_g"], lw["peer_w_query"], lw["peer_sub_keys"], tm)
    x = _peer_expert(x, *routed, lw["peer_u"], lw["peer_vt"], tm)
    states = (k.reshape(b, t, ATT_HEADS, HEAD_DIM), v.reshape(b, t, ATT_HEADS, HEAD_DIM), conv_new,
              p3[:, -1], wkv_new.reshape(b, RWKV_HEADS, HEAD_DIM, HEAD_DIM))
    return x, states


def kernel(x_prompt, x_sample, cache_k, cache_v, state_conv, state_shift, state_wkv, page_table, norm1_g, w_in, conv_w, conv_b, conv_ln_g, conv_ln_b, rwkv_mu, rwkv_w0, rwkv_w_up, rwkv_a0, rwkv_a_up, rwkv_g_up, rwkv_k_k, rwkv_k_a, rwkv_r_k, rwkv_ln_g, rwkv_ln_b, att_bias, w_out, norm2_g, peer_w_query, peer_sub_keys, peer_u, peer_v, final_g):
    depth = w_in.shape[0]
    bp, tp, _ = x_prompt.shape
    bs, ts, _ = x_sample.shape
    n_pool = cache_k.shape[1]
    ck = cache_k.reshape(depth, n_pool, PAGE_SIZE, ATT_DIM)
    cv = cache_v.reshape(depth, n_pool, PAGE_SIZE, ATT_DIM)
    xp = x_prompt.reshape(bp * tp, D_MODEL)
    xs = x_sample.reshape(bs * ts, D_MODEL)
    st_p, st_s = [], []
    for li in range(depth):
        lw = dict(
            norm1_g=norm1_g[li], w_in=w_in[li].astype(BF16), conv_w=conv_w[li], conv_b=conv_b[li],
            conv_ln_g=conv_ln_g[li], conv_ln_b=conv_ln_b[li], rwkv_mu=rwkv_mu[li], rwkv_w0=rwkv_w0[li],
            rwkv_w_up=rwkv_w_up[li], rwkv_a0=rwkv_a0[li], rwkv_a_up=rwkv_a_up[li], rwkv_g_up=rwkv_g_up[li],
            rwkv_k_k=rwkv_k_k[li], rwkv_k_a=rwkv_k_a[li], rwkv_r_k=rwkv_r_k[li].reshape(-1),
            rwkv_ln_g=rwkv_ln_g[li], rwkv_ln_b=rwkv_ln_b[li], w_out=w_out[li].astype(BF16), norm2_g=norm2_g[li],
            peer_w_query=peer_w_query[li].astype(BF16),
            peer_sub_keys=peer_sub_keys[li].astype(BF16).reshape(2 * PEER_HEADS, N_KEYS, PEER_DK // 2),
            peer_u=peer_u[li].astype(BF16), peer_vt=peer_v[li].astype(BF16).T)
        bias = att_bias[li]

        def attend_prompt(qb, k, v, kb, vb):
            sq = lambda a: a.reshape(bp, tp, ATT_DIM)
            return _attn_prompt(sq(qb), sq(kb), sq(vb), bias).reshape(bp * tp, ATT_DIM)

        def attend_sample(qb, k, v, kb, vb):
            o = _attn_decode(li, qb.reshape(bs, ts, ATT_DIM), k.reshape(bs, ts, ATT_DIM),
                             v.reshape(bs, ts, ATT_DIM), ck, cv, page_table, bias)
            return o.reshape(bs * ts, ATT_DIM)

        xp, new_p = _layer(xp, bp, tp, jnp.zeros((bp, CONV_WIDTH - 1, CONV_CH), F32),
                           jnp.zeros((bp, RWKV_COLS), F32), jnp.zeros((bp, RWKV_HEADS, HEAD_DIM, HEAD_DIM), F32),
                           attend_prompt, lw, tm=512, conv_tile=512, wkv_chains=RWKV_HEADS, wkv_tile=256)
        xs, new_s = _layer(xs, bs, ts, state_conv[li], state_shift[li], state_wkv[li],
                           attend_sample, lw, tm=bs * ts, conv_tile=ts, wkv_chains=bp * RWKV_HEADS, wkv_tile=ts)
        st_p.append(new_p)
        st_s.append(new_s)
    y_prompt = _final_norm(xp, final_g, 512).reshape(bp, tp, D_MODEL)
    y_sample = _final_norm(xs, final_g, bs * ts).reshape(bs, ts, D_MODEL)
    stack = lambda st, j: jnp.stack([s[j] for s in st], axis=0)
    return (y_prompt, y_sample) + tuple(stack(st_p, j) for j in range(5)) + tuple(stack(st_s, j) for j in range(5))
```

```python
import functools

import jax
import jax.numpy as jnp
from jax import lax
from jax.experimental import pallas as pl
from jax.experimental.pallas import tpu as pltpu

F32 = jnp.float32
BF16 = jnp.bfloat16

D_MODEL = 1024
HEAD_DIM = 64
CONV_CH = 256
CONV_WIDTH = 31
CONV_HALO = 32
RWKV_HEADS = 6
RWKV_DIM = 384
ATT_HEADS = 6
ATT_DIM = 384
DECAY_LORA = 64
AAA_LORA = 64
GATE_LORA = 128
CONV_COLS = 512
RWKV_COLS = 1408
IN_COLS = 3072
PAGE_SIZE = 128
PEER_HEADS = 8
PEER_DK = 256
N_KEYS = 128
PEER_TOPK = 16
RMS_EPS = 1e-5
LN_EPS = 1e-5
GN_EPS = 64e-5

ATT_TILE = 256
PAGES_PER_STEP = 32
Q_PAD = 8
PEER_ROWS = 16
WKV_CHUNK_LOG2 = 6
WKV_CHUNK = 1 << WKV_CHUNK_LOG2
WKV_BLOCK = 128
VMEM_LIMIT = 56 * 1024 * 1024


def _cparams(sem):
    return pltpu.CompilerParams(dimension_semantics=sem, vmem_limit_bytes=VMEM_LIMIT)


def _sigmoid(x):
    return 1.0 / (1.0 + jnp.exp(-x))


def _softplus(x):
    return jnp.maximum(x, 0.0) + jnp.log(1.0 + jnp.exp(-jnp.abs(x)))


def _dot(a, b):
    return jnp.dot(a, b, preferred_element_type=F32)


def _dot_nt(a, b):
    return lax.dot_general(a, b, (((1,), (1,)), ((), ())), preferred_element_type=F32)


def _split2(x):
    hi = x.astype(BF16)
    lo = (x - hi.astype(F32)).astype(BF16)
    return hi, lo


def _dot3(a, b):
    ah, al = _split2(a)
    bh, bl = _split2(b)
    return _dot(ah, bh) + (_dot(ah, bl) + _dot(al, bh))


def _dot_exact_rhs(x, m):
    hi, lo = _split2(x)
    return _dot(hi, m) + _dot(lo, m)


def _rmsnorm(x, g):
    return x * lax.rsqrt(jnp.mean(x * x, axis=-1, keepdims=True) + RMS_EPS) * g


def _head_blockdiag(n):
    r = lax.broadcasted_iota(jnp.int32, (n, n), 0) // HEAD_DIM
    c = lax.broadcasted_iota(jnp.int32, (n, n), 1) // HEAD_DIM
    return (r == c).astype(BF16)


def _suffix_matrix(n):
    r = lax.broadcasted_iota(jnp.int32, (n, n), 0)
    c = lax.broadcasted_iota(jnp.int32, (n, n), 1)
    return jnp.where(r > c, -1.0, 0.0).astype(BF16)


def _row_spec(tm, cols):
    return pl.BlockSpec((tm, cols), lambda i: (i, 0))


def _const_spec(shape):
    return pl.BlockSpec(shape, lambda *_: (0,) * len(shape))


def _inproj_body(x_ref, g_ref, w_ref, conv_ref, rwkv_ref, k_ref, v_ref, qb_ref, kb_ref, vb_ref):
    h = _rmsnorm(x_ref[...], g_ref[...]).astype(BF16)
    o0, o1, o2, o3 = CONV_COLS, CONV_COLS + RWKV_COLS, CONV_COLS + RWKV_COLS + ATT_DIM, IN_COLS - ATT_DIM
    conv_ref[...] = _dot(h, w_ref[:, 0:o0])
    rwkv_ref[...] = _dot(h, w_ref[:, o0:o1])
    q = _dot(h, w_ref[:, o1:o2])
    k = _dot(h, w_ref[:, o2:o3])
    v = _dot(h, w_ref[:, o3:IN_COLS])
    k_ref[...] = k
    v_ref[...] = v
    qb_ref[...] = (q * (HEAD_DIM ** -0.5)).astype(BF16)
    kb_ref[...] = k.astype(BF16)
    vb_ref[...] = v.astype(BF16)


def _inproj(x, g, w_b, tm):
    n = x.shape[0]
    widths = (CONV_COLS, RWKV_COLS, ATT_DIM, ATT_DIM, ATT_DIM, ATT_DIM, ATT_DIM)
    dtypes = (F32, F32, F32, F32, BF16, BF16, BF16)
    return pl.pallas_call(
        _inproj_body,
        grid=(n // tm,),
        in_specs=[_row_spec(tm, D_MODEL), _const_spec((1, D_MODEL)), _const_spec((D_MODEL, IN_COLS))],
        out_specs=[_row_spec(tm, c) for c in widths],
        out_shape=[jax.ShapeDtypeStruct((n, c), d) for c, d in zip(widths, dtypes)],
        compiler_params=_cparams(("arbitrary",)),
        name="inproj",
    )(x, g.reshape(1, -1), w_b)


def _conv_body(p_ref, buf_ref, w_ref, cb_ref, lg_ref, lb_ref, y_ref, nb_ref, ext_ref, *, tt):
    keep = CONV_WIDTH - 1
    lead = CONV_HALO - keep

    @pl.when(pl.program_id(1) == 0)
    def _():
        ext_ref[0:CONV_HALO, :] = jnp.zeros((CONV_HALO, CONV_CH), F32)
        ext_ref[lead:CONV_HALO, :] = buf_ref[0]

    p = p_ref[0]
    ext_ref[CONV_HALO:CONV_HALO + tt, :] = p[:, :CONV_CH] * _sigmoid(p[:, CONV_CH:])
    acc = jnp.zeros((tt, CONV_CH), F32)
    for j in range(CONV_WIDTH):
        acc = acc + w_ref[j:j + 1, :] * ext_ref[lead + j:lead + j + tt, :]
    y = acc + cb_ref[...]
    mu = jnp.mean(y, axis=-1, keepdims=True)
    d = y - mu
    var = jnp.mean(d * d, axis=-1, keepdims=True)
    y = d * lax.rsqrt(var + LN_EPS) * lg_ref[...] + lb_ref[...]
    y_ref[0] = (y * _sigmoid(y)).astype(y_ref.dtype)
    nb_ref[0] = ext_ref[tt + lead:tt + CONV_HALO, :]
    tail = ext_ref[tt:tt + CONV_HALO, :]
    ext_ref[0:CONV_HALO, :] = tail


def _conv_mixer(p, buf, w, cb, lg, lb, tt):
    b, t, _ = p.shape
    keep = CONV_WIDTH - 1
    return pl.pallas_call(
        functools.partial(_conv_body, tt=tt),
        grid=(b, t // tt),
        in_specs=[pl.BlockSpec((1, tt, CONV_COLS), lambda i, j: (i, j, 0)),
                  pl.BlockSpec((1, keep, CONV_CH), lambda i, j: (i, 0, 0)),
                  _const_spec((CONV_WIDTH, CONV_CH)), _const_spec((1, CONV_CH)),
                  _const_spec((1, CONV_CH)), _const_spec((1, CONV_CH))],
        out_specs=[pl.BlockSpec((1, tt, CONV_CH), lambda i, j: (i, j, 0)),
                   pl.BlockSpec((1, keep, CONV_CH), lambda i, j: (i, 0, 0))],
        out_shape=[jax.ShapeDtypeStruct((b, t, CONV_CH), BF16),
                   jax.ShapeDtypeStruct((b, keep, CONV_CH), F32)],
        scratch_shapes=[pltpu.VMEM((tt + CONV_HALO, CONV_CH), F32)],
        compiler_params=_cparams(("arbitrary", "arbitrary")),
        name="conv_mixer",
    )(p, buf, w, cb.reshape(1, -1), lg.reshape(1, -1), lb.reshape(1, -1))


def _rwkv_pre_body(p_ref, prev_ref, mu_ref, w0_ref, wup_ref, a0_ref, aup_ref, gup_ref, kk_ref, ka_ref, rk_ref,
                   *outs, tiles_per_seq):
    heads_first = tiles_per_seq > 0
    p = p_ref[...]
    if heads_first:
        *outs, carry_ref = outs

        @pl.when(pl.program_id(0) % tiles_per_seq == 0)
        def _():
            carry_ref[...] = prev_ref[0]

        first_row = lax.broadcasted_iota(jnp.int32, p.shape, 0) == 0
        prev = jnp.where(first_row, carry_ref[...], pltpu.roll(p, 1, 0))
        carry_ref[...] = p[p.shape[0] - 1:p.shape[0], :]
    else:
        prev = prev_ref[...]
    pm = p + (prev - p) * mu_ref[...]
    d = RWKV_DIM
    r = pm[:, 0:d]
    k = pm[:, d:2 * d]
    v = pm[:, 2 * d:3 * d]
    xw = pm[:, 3 * d:3 * d + DECAY_LORA]
    xa = pm[:, 3 * d + DECAY_LORA:3 * d + DECAY_LORA + AAA_LORA]
    xg = pm[:, 3 * d + DECAY_LORA + AAA_LORA:RWKV_COLS]
    bd = _head_blockdiag(d)
    w_log = -_softplus(-(w0_ref[...] + _dot3(jnp.tanh(xw), wup_ref[...]))) - 0.5
    log_decay = -jnp.exp(w_log)
    a = _sigmoid(a0_ref[...] + _dot3(xa, aup_ref[...]))
    g = _dot3(_sigmoid(xg), gup_ref[...])
    kk = k * kk_ref[...]
    k = k * (1.0 + (a - 1.0) * ka_ref[...])
    kk = kk / jnp.maximum(jnp.sqrt(_dot_exact_rhs(kk * kk, bd)), 1e-12)
    steps = (r, log_decay, k, v, kk, kk * a)
    outs[-2][...] = g
    outs[-1][...] = _dot_exact_rhs(r * k * rk_ref[...], bd) * v
    if not heads_first:
        for ref, val in zip(outs[:6], steps):
            ref[...] = val
        return
    head = lambda h: slice(h * HEAD_DIM, (h + 1) * HEAD_DIM)
    for ref, val in zip(outs[:6], steps):
        for h in range(RWKV_HEADS):
            ref[h] = val[:, head(h)]
    for ref, val in zip(outs[6:9], (log_decay, kk, v)):
        val_t = val.T
        for h in range(RWKV_HEADS):
            ref[h] = val_t[head(h), :]


def _rwkv_pre(p, prev, mu, w0, w_up, a0, a_up, g_up, k_k, k_a, r_k, tm, seq_len=None):
    n = p.shape[0]
    vec = lambda a: a.reshape(1, -1)
    d = RWKV_DIM
    if seq_len is None:
        per_seq = 0
        prev_spec = _row_spec(tm, RWKV_COLS)
        scratch = []
        out_specs = [_row_spec(tm, d)] * 8
        out_shape = [jax.ShapeDtypeStruct((n, d), F32)] * 8
    else:
        per_seq = seq_len // tm
        prev = prev.reshape(-1, 1, RWKV_COLS)
        prev_spec = pl.BlockSpec((1, 1, RWKV_COLS), lambda i: (i // per_seq, 0, 0))
        scratch = [pltpu.VMEM((1, RWKV_COLS), F32)]
        chains = n // seq_len * RWKV_HEADS
        slab = pl.BlockSpec((RWKV_HEADS, tm, HEAD_DIM), lambda i: (i // per_seq, i % per_seq, 0))
        slab_t = pl.BlockSpec((RWKV_HEADS, HEAD_DIM, tm), lambda i: (i // per_seq, 0, i % per_seq))
        out_specs = [slab] * 6 + [slab_t] * 3 + [_row_spec(tm, d)] * 2
        out_shape = ([jax.ShapeDtypeStruct((chains, seq_len, HEAD_DIM), F32)] * 6
                     + [jax.ShapeDtypeStruct((chains, HEAD_DIM, seq_len), F32)] * 3
                     + [jax.ShapeDtypeStruct((n, d), F32)] * 2)
    return pl.pallas_call(
        functools.partial(_rwkv_pre_body, tiles_per_seq=per_seq),
        grid=(n // tm,),
        in_specs=[_row_spec(tm, RWKV_COLS), prev_spec, _const_spec((1, RWKV_COLS)),
                  _const_spec((1, d)), _const_spec((DECAY_LORA, d)), _const_spec((1, d)),
                  _const_spec((AAA_LORA, d)), _const_spec((GATE_LORA, d)),
                  _const_spec((1, d)), _const_spec((1, d)), _const_spec((1, d))],
        out_specs=out_specs,
        out_shape=out_shape,
        scratch_shapes=scratch,
        compiler_params=_cparams(("arbitrary",)),
        name="rwkv_pre",
    )(p, prev, vec(mu), vec(w0), w_up, vec(a0), a_up, g_up, vec(k_k), vec(k_a), vec(r_k))


def _wkv_body(r_ref, w_ref, k_ref, v_ref, kk_ref, b_ref, s0_ref, y_ref, sT_ref, s_scr, *, nc, tc):
    ti = pl.program_id(1)

    @pl.when(ti == 0)
    def _():
        s_scr[...] = s0_ref[...]

    eye = (lax.broadcasted_iota(jnp.int32, (HEAD_DIM, HEAD_DIM), 0)
           == lax.broadcasted_iota(jnp.int32, (HEAD_DIM, HEAD_DIM), 1))

    def step(t, carry):
        for c in range(nc):
            row = lambda ref: ref[c, pl.ds(t, 1), :]
            s = s_scr[c]
            sa = jnp.sum(s * row(kk_ref), axis=1, keepdims=True)
            vcol = jnp.sum(jnp.where(eye, row(v_ref), 0.0), axis=1, keepdims=True)
            s = s * jnp.exp(row(w_ref)) - sa * row(b_ref) + vcol * row(k_ref)
            s_scr[c] = s
            ycol = jnp.sum(s * row(r_ref), axis=1, keepdims=True)
            y_ref[c, pl.ds(t, 1), :] = jnp.sum(jnp.where(eye, ycol, 0.0), axis=0, keepdims=True)
        return carry

    lax.fori_loop(0, tc, step, 0)

    @pl.when(ti == pl.num_programs(1) - 1)
    def _():
        sT_ref[...] = s_scr[...]


def _wkv(r, w, k, v, kk, b, s0, nc, tc):
    n, t, _ = r.shape
    seq = pl.BlockSpec((nc, tc, HEAD_DIM), lambda i, j: (i, j, 0))
    st = pl.BlockSpec((nc, HEAD_DIM, HEAD_DIM), lambda i, j: (i, 0, 0))
    return pl.pallas_call(
        functools.partial(_wkv_body, nc=nc, tc=tc),
        grid=(n // nc, t // tc),
        in_specs=[seq] * 6 + [st],
        out_specs=[seq, st],
        out_shape=[jax.ShapeDtypeStruct((n, t, HEAD_DIM), F32),
                   jax.ShapeDtypeStruct((n, HEAD_DIM, HEAD_DIM), F32)],
        scratch_shapes=[pltpu.VMEM((nc, HEAD_DIM, HEAD_DIM), F32)],
        compiler_params=_cparams(("arbitrary", "arbitrary")),
        name="wkv",
    )(r, w, k, v, kk, b, s0)


def _mm(a, b):
    return _dot3(a, b)


def _mm_nt(a, b):
    ah, al = _split2(a)
    bh, bl = _split2(b)
    return _dot_nt(ah, bh) + (_dot_nt(ah, bl) + _dot_nt(al, bh))


def _split3(x):
    p1 = x.astype(BF16)
    r1 = x - p1.astype(F32)
    p2 = r1.astype(BF16)
    p3 = (r1 - p2.astype(F32)).astype(BF16)
    return p1, p2, p3


def _wkv_chunk_body(r_ref, lw_ref, k_ref, v_ref, kk_ref, b_ref, lwt_ref, kkt_ref, vt_ref, s0_ref,
                    y_ref, sT_ref, s_scr, *, nc):
    ti = pl.program_id(1)
    c_len = WKV_CHUNK

    @pl.when(ti == 0)
    def _():
        s_scr[...] = s0_ref[...]

    row = lax.broadcasted_iota(jnp.int32, (c_len, c_len), 0)
    col = lax.broadcasted_iota(jnp.int32, (c_len, c_len), 1)
    strict = row > col
    incl = row >= col
    lower = incl.astype(BF16)
    upper = (row <= col).astype(BF16)
    eye = (row == col).astype(F32)

    n_sub = WKV_BLOCK // c_len
    units = [(j, c) for j in range(n_sub) for c in range(nc)]
    each = lambda f, *lists: [f(*args) for args in zip(*lists)]
    ts = lambda j: slice(j * c_len, (j + 1) * c_len)
    rows = lambda ref: [ref[c, ts(j), :] for j, c in units]
    cols = lambda ref: [ref[c, :, ts(j)] for j, c in units]
    r, lw, k, v, kk, b = (rows(ref) for ref in (r_ref, lw_ref, k_ref, v_ref, kk_ref, b_ref))
    lwt, kkt, vt = (cols(ref) for ref in (lwt_ref, kkt_ref, vt_ref))
    cum = each(lambda x: sum(_dot(lower, piece) for piece in _split3(x)), lw)
    cumt = each(lambda x: sum(_dot(piece, upper) for piece in _split3(x)), lwt)
    w_incl = each(jnp.exp, cum)
    w_last = each(lambda w: w[c_len - 1:c_len, :], w_incl)
    inv = each(lambda x: jnp.exp(-x), cum)
    qt = each(lambda x, cs, l: x * jnp.exp(cs - l), kk, cum, lw)
    qtt = each(lambda x, cs, l: x * jnp.exp(cs - l), kkt, cumt, lwt)
    rt = each(jnp.multiply, r, w_incl)
    kt = each(jnp.multiply, k, inv)
    bt = each(jnp.multiply, b, inv)
    a_qb = each(lambda x, y: jnp.where(strict, _mm_nt(x, y), 0.0), qt, bt)
    a_qk = each(lambda x, y: jnp.where(strict, _mm_nt(x, y), 0.0), qt, kt)
    a_rk = each(lambda x, y: jnp.where(incl, _mm_nt(x, y), 0.0), rt, kt)
    a_rb = each(lambda x, y: jnp.where(incl, _mm_nt(x, y), 0.0), rt, bt)
    tinv = each(lambda n: eye - n, a_qb)
    power = a_qb
    for _ in range(WKV_CHUNK_LOG2 - 1):
        power = each(_mm, power, power)
        tinv = each(lambda t, p: t + _mm(t, p), tinv, power)
    tq = each(_mm, tinv, qt)
    av = each(_mm, a_qk, v)
    av_t = each(_mm_nt, vt, a_qk)
    tq_t = each(_mm_nt, qtt, tinv)
    tav = each(_mm, tinv, av)
    tav_t = each(_mm_nt, av_t, tinv)
    ark_v = each(_mm, a_rk, v)
    rq = each(lambda x, a, y: x - _mm(a, y), rt, a_rb, tq)
    yv = each(lambda x, a, y: x - _mm(a, y), ark_v, a_rb, tav)
    gain = each(lambda x, y, w: _mm(x, y * w), vt, kt, w_last)
    bw = each(jnp.multiply, bt, w_last)
    for j in range(n_sub):
        ids = [j * nc + c for c in range(nc)]
        s = [s_scr[c] for c in range(nc)]
        sa_t = [_mm(s[c], tq_t[i]) + tav_t[i] for c, i in enumerate(ids)]
        ys = [_mm_nt(rq[i], s[c]) + yv[i] for c, i in enumerate(ids)]
        new = [s[c] * w_last[i] + gain[i] - _mm(sa_t[c], bw[i]) for c, i in enumerate(ids)]
        for c, i in enumerate(ids):
            y_ref[c, ts(j), :] = ys[c]
            s_scr[c] = new[c]

    @pl.when(ti == pl.num_programs(1) - 1)
    def _():
        sT_ref[...] = s_scr[...]


def _wkv_chunked(r, lw, k, v, kk, b, lwt, kkt, vt, s0, nc):
    n, t, _ = r.shape
    seq = pl.BlockSpec((nc, WKV_BLOCK, HEAD_DIM), lambda i, j: (i, j, 0))
    seq_t = pl.BlockSpec((nc, HEAD_DIM, WKV_BLOCK), lambda i, j: (i, 0, j))
    st = pl.BlockSpec((nc, HEAD_DIM, HEAD_DIM), lambda i, j: (i, 0, 0))
    return pl.pallas_call(
        functools.partial(_wkv_chunk_body, nc=nc),
        grid=(n // nc, t // WKV_BLOCK),
        in_specs=[seq] * 6 + [seq_t] * 3 + [st],
        out_specs=[seq, st],
        out_shape=[jax.ShapeDtypeStruct((n, t, HEAD_DIM), F32),
                   jax.ShapeDtypeStruct((n, HEAD_DIM, HEAD_DIM), F32)],
        scratch_shapes=[pltpu.VMEM((nc, HEAD_DIM, HEAD_DIM), F32)],
        compiler_params=_cparams(("arbitrary", "arbitrary")),
        name="wkv_chunked",
    )(r, lw, k, v, kk, b, lwt, kkt, vt, s0)


def _sb_block(q, kb, vb, bias, rest, umat, mask):
    z = _dot_nt(q, kb) + bias
    sp = _softplus(z)
    counted = sp if mask is None else jnp.where(mask, sp, 0.0)
    after = _dot(counted.astype(BF16), umat)
    a = jnp.exp((z - sp) + after + rest)
    if mask is not None:
        a = jnp.where(mask, a, 0.0)
    out = _dot(a.astype(BF16), vb)
    return out, after[:, 0:1] - counted[:, 0:1]


def _attn_prompt_body(q_ref, k_ref, v_ref, bias_ref, o_ref, *, tile):
    qi = pl.program_id(1)
    heads = range(ATT_HEADS)
    hs = lambda a, h: a[:, h * HEAD_DIM:(h + 1) * HEAD_DIM]
    q_all = q_ref[0]
    qs = [hs(q_all, h) for h in heads]
    biases = [bias_ref[h] for h in heads]
    umat = _suffix_matrix(tile)
    causal = (lax.broadcasted_iota(jnp.int32, (tile, tile), 1)
              < lax.broadcasted_iota(jnp.int32, (tile, tile), 0))

    def visit(j, accs, rests, mask):
        start = pl.multiple_of(j * tile, tile)
        kb = k_ref[0, pl.ds(start, tile), :]
        vb = v_ref[0, pl.ds(start, tile), :]
        zs = [_dot_nt(qs[h], hs(kb, h)) + biases[h] for h in heads]
        sps = [_softplus(z) for z in zs]
        counted = sps if mask is None else [jnp.where(mask, sp, 0.0) for sp in sps]
        afters = [_dot(sp.astype(BF16), umat) for sp in counted]
        probs = [jnp.exp((zs[h] - sps[h]) + afters[h] + rests[h]) for h in heads]
        if mask is not None:
            probs = [jnp.where(mask, a, 0.0) for a in probs]
        outs = [_dot(probs[h].astype(BF16), hs(vb, h)) for h in heads]
        new_accs = tuple(accs[h] + outs[h] for h in heads)
        new_rests = tuple(rests[h] + afters[h][:, 0:1] - counted[h][:, 0:1] for h in heads)
        return new_accs, new_rests

    accs = tuple(jnp.zeros((tile, HEAD_DIM), F32) for _ in heads)
    rests = tuple(jnp.zeros((tile, 1), F32) for _ in heads)
    accs, rests = visit(qi, accs, rests, causal)
    accs, rests = lax.fori_loop(0, qi, lambda jj, c: visit(qi - 1 - jj, c[0], c[1], None), (accs, rests))
    r = lax.broadcasted_iota(jnp.int32, (HEAD_DIM, ATT_DIM), 0)
    c = lax.broadcasted_iota(jnp.int32, (HEAD_DIM, ATT_DIM), 1)
    out = jnp.zeros((tile, ATT_DIM), F32)
    for h in heads:
        out = out + _dot(accs[h].astype(BF16), (c == r + h * HEAD_DIM).astype(BF16))
    o_ref[0] = out.astype(o_ref.dtype)


def _attn_prompt(qb, kb, vb, bias):
    b, t, d = qb.shape
    tile = ATT_TILE
    bias_b = jnp.broadcast_to(bias.astype(F32)[:, None, None], (ATT_HEADS, 1, tile))
    qspec = pl.BlockSpec((1, tile, d), lambda i, l: (i, l, 0))
    kvspec = pl.BlockSpec((1, t, d), lambda i, l: (i, 0, 0))
    return pl.pallas_call(
        functools.partial(_attn_prompt_body, tile=tile),
        grid=(b, t // tile),
        in_specs=[qspec, kvspec, kvspec, _const_spec((ATT_HEADS, 1, tile))],
        out_specs=qspec,
        out_shape=jax.ShapeDtypeStruct((b, t, d), BF16),
        compiler_params=_cparams(("arbitrary", "arbitrary")),
        name="attn_prompt",
    )(qb, kb, vb, bias_b)


def _attn_decode_body(pt_ref, q_ref, bias_ref, kn_ref, vn_ref, *refs, npg, nq):
    k_refs = refs[:npg]
    v_refs = refs[npg:2 * npg]
    o_ref, rest_ref, acc_ref = refs[2 * npg:]
    jb = pl.program_id(1)
    rows = ATT_HEADS * Q_PAD
    pages = range(npg)
    spread = lambda tot: jnp.broadcast_to(tot, (rows, PAGE_SIZE))
    q = q_ref[0]
    bias = bias_ref[...]
    umat = _suffix_matrix(PAGE_SIZE)

    @pl.when(jb == 0)
    def _():
        kidx = lax.broadcasted_iota(jnp.int32, (rows, PAGE_SIZE), 1)
        qidx = lax.broadcasted_iota(jnp.int32, (rows, PAGE_SIZE), 0) % Q_PAD
        out, tot = _sb_block(q, kn_ref[0].astype(BF16), vn_ref[0].astype(BF16), bias,
                             jnp.zeros((rows, 1), F32), umat, (kidx < qidx) & (kidx < nq))
        acc_ref[...] = out
        rest_ref[...] = spread(tot)

    zs = [_dot_nt(q, k_refs[s][0, 0].astype(BF16)) + bias for s in pages]
    sps = [_softplus(z) for z in zs]
    afters = [_dot(sp.astype(BF16), umat) for sp in sps]
    rest = rest_ref[...]
    probs = []
    for s in pages:
        probs.append(jnp.exp((zs[s] - sps[s]) + afters[s] + rest).astype(BF16))
        rest = rest + spread(afters[s][:, 0:1] - sps[s][:, 0:1])
    rest_ref[...] = rest
    out = _dot(probs[0], v_refs[0][0, 0].astype(BF16))
    for s in pages[1:]:
        out = out + _dot(probs[s], v_refs[s][0, 0].astype(BF16))
    acc_ref[...] += out

    @pl.when(jb == pl.num_programs(1) - 1)
    def _():
        head_of_row = lax.broadcasted_iota(jnp.int32, (rows, ATT_DIM), 0) // Q_PAD
        head_of_col = lax.broadcasted_iota(jnp.int32, (rows, ATT_DIM), 1) // HEAD_DIM
        kept = jnp.where(head_of_row == head_of_col, acc_ref[...], 0.0)
        out = kept[0:Q_PAD]
        for h in range(1, ATT_HEADS):
            out = out + kept[h * Q_PAD:(h + 1) * Q_PAD]
        o_ref[0] = out.astype(o_ref.dtype)


def _attn_decode(li, qb, k_new, v_new, cache_k, cache_v, page_table, bias):
    b, t, _ = qb.shape
    n_pages = page_table.shape[1]
    npg = PAGES_PER_STEP
    rows = ATT_HEADS * Q_PAD
    qpad = jnp.pad(qb, ((0, 0), (0, Q_PAD - t), (0, 0)))
    head_of_col = jnp.arange(ATT_DIM) // HEAD_DIM
    qbd = jnp.where(head_of_col[None, None, None, :] == jnp.arange(ATT_HEADS)[None, :, None, None],
                    qpad[:, None], jnp.zeros((), BF16)).reshape(b, rows, ATT_DIM)
    bias_b = jnp.broadcast_to(jnp.repeat(bias.astype(F32), Q_PAD)[:, None], (rows, PAGE_SIZE))
    kn = jnp.pad(k_new, ((0, 0), (0, PAGE_SIZE - t), (0, 0)))
    vn = jnp.pad(v_new, ((0, 0), (0, PAGE_SIZE - t), (0, 0)))

    def page_spec(s):
        return pl.BlockSpec((1, 1, PAGE_SIZE, ATT_DIM),
                            lambda i, j, pt: (li, pt[i * n_pages + n_pages - 1 - (j * npg + s)], 0, 0))

    per_b = lambda r, c: pl.BlockSpec((1, r, c), lambda i, j, pt: (i, 0, 0))
    grid_spec = pltpu.PrefetchScalarGridSpec(
        num_scalar_prefetch=1,
        grid=(b, n_pages // npg),
        in_specs=[per_b(rows, ATT_DIM), pl.BlockSpec((rows, PAGE_SIZE), lambda i, j, pt: (0, 0)),
                  per_b(PAGE_SIZE, ATT_DIM), per_b(PAGE_SIZE, ATT_DIM)]
                 + [page_spec(s) for s in range(npg)] * 2,
        out_specs=per_b(Q_PAD, ATT_DIM),
        scratch_shapes=[pltpu.VMEM((rows, PAGE_SIZE), F32), pltpu.VMEM((rows, ATT_DIM), F32)],
    )
    out = pl.pallas_call(
        functools.partial(_attn_decode_body, npg=npg, nq=t),
        grid_spec=grid_spec,
        out_shape=jax.ShapeDtypeStruct((b, Q_PAD, ATT_DIM), BF16),
        compiler_params=_cparams(("arbitrary", "arbitrary")),
        name="attn_decode",
    )(page_table.reshape(-1), qbd, bias_b, kn, vn, *([cache_k] * npg), *([cache_v] * npg))
    return out[:, :t]


def _outproj_body(x_ref, yc_ref, yw_ref, bonus_ref, g_ref, ya_ref, lng_ref, lnb_ref, w_ref, o_ref):
    bd = _head_blockdiag(RWKV_DIM)
    if len(yw_ref.shape) == 3:
        r = lax.broadcasted_iota(jnp.int32, (HEAD_DIM, RWKV_DIM), 0)
        c = lax.broadcasted_iota(jnp.int32, (HEAD_DIM, RWKV_DIM), 1)
        y = sum(_dot_exact_rhs(yw_ref[h], (c == r + h * HEAD_DIM).astype(BF16)) for h in range(RWKV_HEADS))
    else:
        y = yw_ref[...]
    inv = 1.0 / HEAD_DIM
    d = y - _dot_exact_rhs(y, bd) * inv
    var = _dot_exact_rhs(d * d, bd) * inv
    yn = d * lax.rsqrt(var + GN_EPS) * lng_ref[...] + lnb_ref[...]
    yr = ((yn + bonus_ref[...]) * g_ref[...]).astype(BF16)
    c0, c1 = CONV_CH, CONV_CH + RWKV_DIM
    o_ref[...] = (x_ref[...] + _dot(yc_ref[...], w_ref[0:c0, :]) + _dot(yr, w_ref[c0:c1, :])
                  + _dot(ya_ref[...], w_ref[c1:D_MODEL, :]))


def _outproj(x, yc, yw, bonus, g, ya, ln_g, ln_b, w_b, tm):
    n = x.shape[0]
    if yw.ndim == 3:
        per_seq = yw.shape[1] // tm
        yw_spec = pl.BlockSpec((RWKV_HEADS, tm, HEAD_DIM), lambda i: (i // per_seq, i % per_seq, 0))
    else:
        yw_spec = _row_spec(tm, RWKV_DIM)
    return pl.pallas_call(
        _outproj_body,
        grid=(n // tm,),
        in_specs=[_row_spec(tm, D_MODEL), _row_spec(tm, CONV_CH), yw_spec, _row_spec(tm, RWKV_DIM),
                  _row_spec(tm, RWKV_DIM), _row_spec(tm, ATT_DIM), _const_spec((1, RWKV_DIM)),
                  _const_spec((1, RWKV_DIM)), _const_spec((D_MODEL, D_MODEL))],
        out_specs=_row_spec(tm, D_MODEL),
        out_shape=jax.ShapeDtypeStruct((n, D_MODEL), F32),
        compiler_params=_cparams(("arbitrary",)),
        name="outproj",
    )(x, yc, yw, bonus, g, ya, ln_g.reshape(1, -1), ln_b.reshape(1, -1), w_b)


SUBLANES = 8
KEY_GROUPS = N_KEYS // SUBLANES


def _oddeven_merge(lo, hi, r):
    step = r * 2
    if step < hi - lo:
        yield from _oddeven_merge(lo, hi, step)
        yield from _oddeven_merge(lo + r, hi, step)
        yield from [(i, i + r) for i in range(lo + r, hi - r, step)]
    else:
        yield (lo, lo + r)


def _oddeven_sort(lo, hi):
    if hi - lo >= 1:
        mid = lo + (hi - lo) // 2
        yield from _oddeven_sort(lo, mid)
        yield from _oddeven_sort(mid + 1, hi)
        yield from _oddeven_merge(lo, hi, 1)


SORT_NET = tuple(_oddeven_sort(0, PEER_TOPK - 1))
BITONIC_NET = tuple((i, i + s) for s in (8, 4, 2, 1) for i in range(PEER_TOPK) if not i & s)
SUM_SLOTS = tuple((a, b) for a in range(PEER_TOPK) for b in range(PEER_TOPK) if (a + 1) * (b + 1) <= PEER_TOPK)


def _exchange(vals, i, j):
    vals[i], vals[j] = jnp.maximum(vals[i], vals[j]), jnp.minimum(vals[i], vals[j])


def _sublane_max(x):
    for shift in (4, 2, 1):
        x = jnp.maximum(x, pltpu.roll(x, shift, 0))
    return x


def _top16_sorted(groups):
    vals = list(groups)
    for i, j in SORT_NET:
        _exchange(vals, i, j)
    for shift in (4, 2, 1):
        moved = [pltpu.roll(v, shift, 0) for v in vals]
        vals = [jnp.maximum(vals[i], moved[PEER_TOPK - 1 - i]) for i in range(PEER_TOPK)]
        for i, j in BITONIC_NET:
            _exchange(vals, i, j)
    return vals


def _peer_route_body(x_ref, g_ref, wq_ref, sk_ref, h_out, rank_out, e2_out, cnt_out, f1_out):
    tm = x_ref.shape[0]
    kk = PEER_TOPK
    hb = _rmsnorm(x_ref[...], g_ref[...]).astype(BF16)
    h_out[...] = hb
    sub = lax.broadcasted_iota(jnp.int32, (SUBLANES, tm), 0)
    split = lambda s: [s[i * SUBLANES:(i + 1) * SUBLANES, :] for i in range(KEY_GROUPS)]
    join = lambda parts: jnp.concatenate(parts, axis=0)
    neg_inf = jnp.full((SUBLANES, tm), -jnp.inf, F32)
    half = PEER_DK // 2
    for h in range(PEER_HEADS):
        q = _dot(hb, wq_ref[:, h * PEER_DK:(h + 1) * PEER_DK])
        s1 = split(_dot_nt(sk_ref[2 * h], q[:, :half].astype(BF16)))
        s2 = split(_dot_nt(sk_ref[2 * h + 1], q[:, half:].astype(BF16)))
        v1 = _top16_sorted(s1)
        v2 = _top16_sorted(s2)
        sums = {ab: v1[ab[0]] + v2[ab[1]] for ab in SUM_SLOTS}
        packs = []
        for j in range(0, len(SUM_SLOTS), SUBLANES):
            pack = neg_inf
            for s, ab in enumerate(SUM_SLOTS[j:j + SUBLANES]):
                pack = jnp.where(sub == s, sums[ab], pack)
            packs.append(pack)
        best = []
        for _ in range(kk):
            m = packs[0]
            for p in packs[1:]:
                m = jnp.maximum(m, p)
            m = _sublane_max(m)
            best.append(m)
            packs = [jnp.where(p == m, -jnp.inf, p) for p in packs]
        z = jnp.ones((SUBLANES, tm), F32)
        for i in range(1, kk):
            z = z + jnp.exp(best[i] - best[0])
        half_inv_z = 0.5 / z
        cnt = [jnp.zeros((SUBLANES, tm), F32) for _ in range(kk)]
        for a, b in SUM_SLOTS:
            cnt[a] = cnt[a] + jnp.where(sums[(a, b)] >= best[kk - 1], 1.0, 0.0)
        cnt_full, f1_full, rank_full, e2_full = [], [], [], []
        for g in range(KEY_GROUPS):
            c = jnp.zeros((SUBLANES, tm), F32)
            for a in range(kk):
                c = jnp.where(s1[g] == v1[a], cnt[a], c)
            cnt_full.append(c)
            f1_full.append(jnp.where(s1[g] >= v1[kk - 1], jnp.exp(s1[g] - v1[0]), 0.0) * half_inv_z)
            rk = jnp.full((SUBLANES, tm), float(kk), F32)
            for b in reversed(range(kk)):
                rk = jnp.where(s2[g] == v2[b], float(b), rk)
            rank_full.append(rk)
            e2_full.append(jnp.where(s2[g] >= v2[kk - 1], jnp.exp(s2[g] - v2[0]), 0.0))
        cnt_out[h] = join(cnt_full)
        f1_out[h] = join(f1_full)
        rank_out[h] = join(rank_full).astype(BF16)
        e2_out[h] = join(e2_full).astype(BF16)


def _peer_route(x, g, wq_b, sk_b, tm):
    n = x.shape[0]
    keyed = pl.BlockSpec((PEER_HEADS, N_KEYS, tm), lambda i: (0, 0, i))
    keyed_shape = lambda dt: jax.ShapeDtypeStruct((PEER_HEADS, N_KEYS, n), dt)
    return pl.pallas_call(
        _peer_route_body,
        grid=(n // tm,),
        in_specs=[_row_spec(tm, D_MODEL), _const_spec((1, D_MODEL)), _const_spec((D_MODEL, PEER_HEADS * PEER_DK)),
                  _const_spec((2 * PEER_HEADS, N_KEYS, PEER_DK // 2))],
        out_specs=[_row_spec(tm, D_MODEL), keyed, keyed, keyed, keyed],
        out_shape=[jax.ShapeDtypeStruct((n, D_MODEL), BF16), keyed_shape(BF16), keyed_shape(BF16),
                   keyed_shape(F32), keyed_shape(F32)],
        compiler_params=_cparams(("arbitrary",)),
        name="peer_route",
    )(x, g.reshape(1, -1), wq_b, sk_b)


GATE_ROWS = 16


GELU_A = 0.7978845608028654
GELU_B = GELU_A * 0.044715


def _gelu_tanh_doubled(x):
    return x + x * jnp.tanh(x * (GELU_A + GELU_B * (x * x)))


def _peer_expert_body(x_ref, h_ref, rank_ref, e2_ref, cnt_ref, f1_ref, u_ref, vt_ref, o_ref, acc_ref):
    ec = pl.program_id(1)

    @pl.when(ec == 0)
    def _():
        acc_ref[...] = jnp.zeros(acc_ref.shape, F32)

    tm = h_ref.shape[0]
    n_grp = 2 if tm % 512 == 0 else 1
    grp = tm // n_grp
    toks = [slice(g * grp, (g + 1) * grp) for g in range(n_grp)]
    hids = [_dot_nt(u_ref[...], h_ref[tk, :]) for tk in toks]
    heads = range(PEER_HEADS)
    for g, tk in enumerate(toks):
        weights = []
        for r in range(PEER_ROWS):
            bcast = lambda ref, h: jnp.broadcast_to(ref[h, 0, r:r + 1, tk], (GATE_ROWS, grp)).astype(BF16)
            cnts = [bcast(cnt_ref, h) for h in heads]
            f1s = [bcast(f1_ref, h) for h in heads]
            for j in range(N_KEYS // GATE_ROWS):
                keys = slice(j * GATE_ROWS, (j + 1) * GATE_ROWS)
                gate = jnp.zeros((GATE_ROWS, grp), BF16)
                for h in heads:
                    e2 = e2_ref[h, keys, tk]
                    gate = gate + jnp.where(rank_ref[h, keys, tk] < cnts[h], e2, jnp.zeros_like(e2)) * f1s[h]
                rows = slice(r * N_KEYS + j * GATE_ROWS, r * N_KEYS + (j + 1) * GATE_ROWS)
                weights.append(gate * _gelu_tanh_doubled(hids[g][rows]).astype(BF16))
        acc_ref[:, tk] += _dot(vt_ref[...], jnp.concatenate(weights, axis=0))

    @pl.when(ec == pl.num_programs(1) - 1)
    def _():
        o_ref[...] = x_ref[...] + acc_ref[...].T


def _peer_expert(x, hb, rank2, e2, cnt, f1, u_b, vt_b, tm):
    n = x.shape[0]
    te = PEER_ROWS * N_KEYS
    ne = u_b.shape[0] // te
    chunked = lambda a: a.reshape(PEER_HEADS, ne, PEER_ROWS, n)
    tok = lambda c: pl.BlockSpec((tm, c), lambda i, e: (i, 0))
    keyed = pl.BlockSpec((PEER_HEADS, N_KEYS, tm), lambda i, e: (0, 0, i))
    row = pl.BlockSpec((PEER_HEADS, 1, PEER_ROWS, tm), lambda i, e: (0, e, 0, i))
    return pl.pallas_call(
        _peer_expert_body,
        grid=(n // tm, ne),
        in_specs=[tok(D_MODEL), tok(D_MODEL), keyed, keyed, row, row,
                  pl.BlockSpec((te, D_MODEL), lambda i, e: (e, 0)),
                  pl.BlockSpec((D_MODEL, te), lambda i, e: (0, e))],
        out_specs=tok(D_MODEL),
        out_shape=jax.ShapeDtypeStruct((n, D_MODEL), F32),
        scratch_shapes=[pltpu.VMEM((D_MODEL, tm), F32)],
        compiler_params=_cparams(("arbitrary", "arbitrary")),
        name="peer_expert",
    )(x, hb, rank2, e2, chunked(cnt), chunked(f1), u_b, vt_b)


def _final_norm_body(x_ref, g_ref, o_ref):
    o_ref[...] = _rmsnorm(x_ref[...], g_ref[...])


def _final_norm(x, g, tm):
    n = x.shape[0]
    return pl.pallas_call(
        _final_norm_body,
        grid=(n // tm,),
        in_specs=[_row_spec(tm, D_MODEL), _const_spec((1, D_MODEL))],
        out_specs=_row_spec(tm, D_MODEL),
        out_shape=jax.ShapeDtypeStruct((n, D_MODEL), F32),
        compiler_params=_cparams(("arbitrary",)),
        name="final_norm",
    )(x, g.reshape(1, -1))


def _heads_first(a, b, t):
    h = a.shape[1] // HEAD_DIM
    return a.reshape(b, t, h, HEAD_DIM).transpose(0, 2, 1, 3).reshape(b * h, t, HEAD_DIM)


def _tokens_first(a, b, t):
    h = a.shape[0] // b
    return a.reshape(b, h, t, HEAD_DIM).transpose(0, 2, 1, 3).reshape(b * t, h * HEAD_DIM)


def _layer(x, b, t, conv_buf, shift_prev, wkv0, attend, lw, tm, conv_tile, wkv_chains, wkv_tile):
    p_conv, p_rwkv, k, v, qb, kb, vb = _inproj(x, lw["norm1_g"], lw["w_in"], tm)

    y_conv, conv_new = _conv_mixer(p_conv.reshape(b, t, CONV_COLS), conv_buf, lw["conv_w"], lw["conv_b"],
                                   lw["conv_ln_g"], lw["conv_ln_b"], conv_tile)

    p3 = p_rwkv.reshape(b, t, RWKV_COLS)
    pre = lambda prev, **kw: _rwkv_pre(p_rwkv, prev, lw["rwkv_mu"], lw["rwkv_w0"], lw["rwkv_w_up"], lw["rwkv_a0"],
                                       lw["rwkv_a_up"], lw["rwkv_g_up"], lw["rwkv_k_k"], lw["rwkv_k_a"],
                                       lw["rwkv_r_k"], tm, **kw)
    s0 = wkv0.reshape(b * RWKV_HEADS, HEAD_DIM, HEAD_DIM)
    if t % WKV_BLOCK == 0 and t % tm == 0:
        *steps, g, bonus = pre(shift_prev, seq_len=t)
        y_wkv, wkv_new = _wkv_chunked(*steps, s0, wkv_chains)
    else:
        prev = jnp.concatenate([shift_prev[:, None, :], p3[:, :-1]], axis=1).reshape(b * t, RWKV_COLS)
        *steps, g, bonus = pre(prev)
        y_wkv, wkv_new = _wkv(*(_heads_first(a, b, t) for a in steps), s0, wkv_chains, wkv_tile)
        y_wkv = _tokens_first(y_wkv, b, t)

    y_att = attend(qb, k, v, kb, vb)

    x = _outproj(x, y_conv.reshape(b * t, CONV_CH), y_wkv, bonus, g, y_att, lw["rwkv_ln_g"], lw["rwkv_ln_b"],
                 lw["w_out"], tm)
    routed = _peer_route(x, lw["norm2_g"], lw["peer_w_query"], lw["peer_sub_keys"], tm)
    x = _peer_expert(x, *routed, lw["peer_u"], lw["peer_vt"], tm)
    states = (k.reshape(b, t, ATT_HEADS, HEAD_DIM), v.reshape(b, t, ATT_HEADS, HEAD_DIM), conv_new,
              p3[:, -1], wkv_new.reshape(b, RWKV_HEADS, HEAD_DIM, HEAD_DIM))
    return x, states


def kernel(x_prompt, x_sample, cache_k, cache_v, state_conv, state_shift, state_wkv, page_table, norm1_g, w_in, conv_w, conv_b, conv_ln_g, conv_ln_b, rwkv_mu, rwkv_w0, rwkv_w_up, rwkv_a0, rwkv_a_up, rwkv_g_up, rwkv_k_k, rwkv_k_a, rwkv_r_k, rwkv_ln_g, rwkv_ln_b, att_bias, w_out, norm2_g, peer_w_query, peer_sub_keys, peer_u, peer_v, final_g):
    depth = w_in.shape[0]
    bp, tp, _ = x_prompt.shape
    bs, ts, _ = x_sample.shape
    n_pool = cache_k.shape[1]
    ck = cache_k.reshape(depth, n_pool, PAGE_SIZE, ATT_DIM)
    cv = cache_v.reshape(depth, n_pool, PAGE_SIZE, ATT_DIM)
    xp = x_prompt.reshape(bp * tp, D_MODEL)
    xs = x_sample.reshape(bs * ts, D_MODEL)
    st_p, st_s = [], []
    for li in range(depth):
        lw = dict(
            norm1_g=norm1_g[li], w_in=w_in[li].astype(BF16), conv_w=conv_w[li], conv_b=conv_b[li],
            conv_ln_g=conv_ln_g[li], conv_ln_b=conv_ln_b[li], rwkv_mu=rwkv_mu[li], rwkv_w0=rwkv_w0[li],
            rwkv_w_up=rwkv_w_up[li], rwkv_a0=rwkv_a0[li], rwkv_a_up=rwkv_a_up[li], rwkv_g_up=rwkv_g_up[li],
            rwkv_k_k=rwkv_k_k[li], rwkv_k_a=rwkv_k_a[li], rwkv_r_k=rwkv_r_k[li].reshape(-1),
            rwkv_ln_g=rwkv_ln_g[li], rwkv_ln_b=rwkv_ln_b[li], w_out=w_out[li].astype(BF16), norm2_g=norm2_g[li],
            peer_w_query=peer_w_query[li].astype(BF16),
            peer_sub_keys=peer_sub_keys[li].astype(BF16).reshape(2 * PEER_HEADS, N_KEYS, PEER_DK // 2),
            peer_u=peer_u[li].astype(BF16), peer_vt=peer_v[li].astype(BF16).T)
        bias = att_bias[li]

        def attend_prompt(qb, k, v, kb, vb):
            sq = lambda a: a.reshape(bp, tp, ATT_DIM)
            return _attn_prompt(sq(qb), sq(kb), sq(vb), bias).reshape(bp * tp, ATT_DIM)

        def attend_sample(qb, k, v, kb, vb):
            o = _attn_decode(li, qb.reshape(bs, ts, ATT_DIM), k.reshape(bs, ts, ATT_DIM),
                             v.reshape(bs, ts, ATT_DIM), ck, cv, page_table, bias)
            return o.reshape(bs * ts, ATT_DIM)

        xp, new_p = _layer(xp, bp, tp, jnp.zeros((bp, CONV_WIDTH - 1, CONV_CH), F32),
                           jnp.zeros((bp, RWKV_COLS), F32), jnp.zeros((bp, RWKV_HEADS, HEAD_DIM, HEAD_DIM), F32),
                           attend_prompt, lw, tm=512, conv_tile=512, wkv_chains=bp * RWKV_HEADS, wkv_tile=256)
        xs, new_s = _layer(xs, bs, ts, state_conv[li], state_shift[li], state_wkv[li],
                           attend_sample, lw, tm=bs * ts, conv_tile=ts, wkv_chains=bp * RWKV_HEADS, wkv_tile=ts)
        st_p.append(new_p)
        st_s.append(new_s)
    y_prompt = _final_norm(xp, final_g, 512).reshape(bp, tp, D_MODEL)
    y_sample = _final_norm(xs, final_g, bs * ts).reshape(bs, ts, D_MODEL)
    stack = lambda st, j: jnp.stack([s[j] for s in st], axis=0)
    return (y_prompt, y_sample) + tuple(stack(st_p, j) for j in range(5)) + tuple(stack(st_s, j) for j in range(5))
```
